```python
import math
import jax, jax.numpy as jnp
from jax import lax
import numpy as np

D_MODEL = 1024
BATCH = 1
SEQ = 16384
DEPTH = 1

FOX_HEAD_DIM = 64
FOX_HEADS = (D_MODEL // 2) // FOX_HEAD_DIM
FOX_WIDTH = FOX_HEADS * FOX_HEAD_DIM
RET_HEAD_DIM = 128
RET_HEADS = (D_MODEL - FOX_WIDTH) // RET_HEAD_DIM
RET_WIDTH = RET_HEADS * RET_HEAD_DIM
MIX_WIDTH = FOX_WIDTH + RET_WIDTH
D_FF = 256 * int(math.ceil(8 * D_MODEL / 3 / 256))
CONV_WIDTH = 3
Q_BLOCK = 128
RET_CHUNK = 128
ROPE_BASE = 10000.0
LN_EPS = 1e-5
GN_EPS = 1e-6
ALPHA = (2 * DEPTH) ** 0.25
BETA = (8 * DEPTH) ** -0.25
IN_SIZES = [FOX_WIDTH, FOX_WIDTH, FOX_WIDTH, FOX_HEADS,
            RET_WIDTH, RET_WIDTH, RET_WIDTH, RET_WIDTH]
IN_COLS = sum(IN_SIZES)
IN_SPLITS = [int(v) for v in np.cumsum(IN_SIZES)[:-1]]

kernel_name = "fox_retnet_hymba_deepnorm_adaln_layer"


def layer_norm(x, g, b):
    xf = x.astype(jnp.float32)
    mu = jnp.mean(xf, axis=-1, keepdims=True)
    var = jnp.mean(jnp.square(xf - mu), axis=-1, keepdims=True)
    y = (xf - mu) * lax.rsqrt(var + LN_EPS)
    return (y * g + b).astype(x.dtype)


def forgetting_attention(q, k, v, log_f):
    B, S, H, Dh = q.shape
    nb = S // Q_BLOCK
    scale = Dh ** -0.5
    cum = jnp.cumsum(log_f.astype(jnp.float32), axis=1).transpose(0, 2, 1)
    kh = k.transpose(0, 2, 1, 3)
    vh = v.transpose(0, 2, 1, 3)
    qb = q.reshape(B, nb, Q_BLOCK, H, Dh).transpose(1, 0, 3, 2, 4)
    cb = cum.reshape(B, H, nb, Q_BLOCK).transpose(2, 0, 1, 3)
    key_pos = jnp.arange(S)

    def block(args):
        qi, ci, start = args
        s = jnp.einsum('bhqd,bhkd->bhqk', qi, kh,
                       preferred_element_type=jnp.float32) * scale
        s = s + ci[..., None] - cum[:, :, None, :]
        q_pos = start + jnp.arange(Q_BLOCK)
        s = jnp.where(key_pos[None, :] <= q_pos[:, None], s, -jnp.inf)
        p = jax.nn.softmax(s, axis=-1)
        return jnp.einsum('bhqk,bhkd->bhqd', p.astype(vh.dtype), vh)

    out = lax.map(block, (qb, cb, jnp.arange(nb) * Q_BLOCK))
    return out.transpose(1, 0, 3, 2, 4).reshape(B, S, H * Dh)


def rotate_half(x):
    x1, x2 = jnp.split(x, 2, axis=-1)
    return jnp.concatenate([-x2, x1], axis=-1)


def retention(q, k, v, g):
    B, S, H, dk = q.shape
    dv = v.shape[-1]
    C = RET_CHUNK
    nc = S // C
    dt = q.dtype
    pos = jnp.arange(S, dtype=jnp.float32)
    inv_freq = ROPE_BASE ** (-jnp.arange(0, dk, 2, dtype=jnp.float32) / dk)
    ang = pos[:, None] * inv_freq[None, :]
    ang = jnp.concatenate([ang, ang], axis=-1)
    cos = jnp.cos(ang)[None, :, None, :].astype(dt)
    sin = jnp.sin(ang)[None, :, None, :].astype(dt)
    q = q * cos + rotate_half(q) * sin
    k = (k * cos + rotate_half(k) * sin) * (dk ** -0.5)
    log_gamma = jnp.log1p(-jnp.exp2(-5.0 - jnp.arange(H, dtype=jnp.float32)))
    idx = jnp.arange(C, dtype=jnp.float32)
    diff = idx[:, None] - idx[None, :]
    inner = jnp.where(diff[None] >= 0,
                      jnp.exp(jnp.maximum(diff, 0.0)[None] * log_gamma[:, None, None]),
                      0.0).astype(dt)
    xi = jnp.exp((idx[None, :] + 1.0) * log_gamma[:, None]).astype(dt)
    zeta = jnp.exp((C - 1.0 - idx[None, :]) * log_gamma[:, None]).astype(dt)
    g_chunk = jnp.exp(C * log_gamma).astype(dt)

    def to_chunks(t):
        return t.reshape(B, nc, C, H, t.shape[-1]).transpose(1, 0, 3, 2, 4)

    qc, kc, vc = to_chunks(q), to_chunks(k), to_chunks(v)

    def step(R, xs):
        qi, ki, vi = xs
        s = jnp.einsum('bhnd,bhmd->bhnm', qi, ki) * inner[None]
        o = (jnp.einsum('bhnm,bhmv->bhnv', s, vi)
             + jnp.einsum('bhnd,bhdv->bhnv', qi, R) * xi[None, :, :, None])
        R = (R * g_chunk[None, :, None, None]
             + jnp.einsum('bhmd,bhmv->bhdv', ki * zeta[None, :, :, None], vi))
        return R, o

    R0 = jnp.zeros((B, H, dk, dv), dt)
    _, o = lax.scan(step, R0, (qc, kc, vc))
    o = o.transpose(1, 0, 3, 2, 4).reshape(B, S, H, dv)
    of = o.astype(jnp.float32)
    mu = jnp.mean(of, axis=-1, keepdims=True)
    var = jnp.mean(jnp.square(of - mu), axis=-1, keepdims=True)
    o = ((of - mu) * lax.rsqrt(var + GN_EPS)).astype(dt).reshape(B, S, H * dv)
    return jax.nn.silu(g) * o


def causal_depthwise_conv(u, w, b):
    S = u.shape[1]
    up = jnp.pad(u, ((0, 0), (CONV_WIDTH - 1, 0), (0, 0)))
    y = b
    for i in range(CONV_WIDTH):
        y = y + up[:, i:i + S, :] * w[i]
    return y


def setup_inputs(seed: int = 0) -> dict:
    key = jax.random.key(seed)
    ks = jax.random.split(key, 16)
    f32 = jnp.float32
    x = jax.random.normal(ks[0], (BATCH, SEQ, D_MODEL), f32)
    c = jax.random.normal(ks[1], (BATCH, D_MODEL), f32)
    w_ada = jax.random.normal(ks[2], (DEPTH, D_MODEL, 6 * D_MODEL), f32) * D_MODEL ** -0.5
    b_ada = 0.01 * jax.random.normal(ks[3], (DEPTH, 6 * D_MODEL), f32)
    col_scale = np.ones((IN_COLS,), np.float32)
    fv0 = 2 * FOX_WIDTH
    col_scale[fv0:fv0 + FOX_WIDTH] = BETA
    rv0 = 3 * FOX_WIDTH + FOX_HEADS + 2 * RET_WIDTH
    col_scale[rv0:rv0 + RET_WIDTH] = BETA
    w_in = (jax.random.normal(ks[4], (DEPTH, D_MODEL, IN_COLS), f32)
            * D_MODEL ** -0.5 * jnp.asarray(col_scale))
    b_f = 2.0 + 0.1 * jax.random.normal(ks[5], (DEPTH, FOX_HEADS), f32)
    w_out = jax.random.normal(ks[6], (DEPTH, MIX_WIDTH, D_MODEL), f32) * MIX_WIDTH ** -0.5 * BETA
    ln1_g = 1.0 + 0.02 * jax.random.normal(ks[7], (DEPTH, D_MODEL), f32)
    ln1_b = 0.02 * jax.random.normal(ks[8], (DEPTH, D_MODEL), f32)
    w_up = jax.random.normal(ks[9], (DEPTH, D_MODEL, 2 * D_FF), f32) * D_MODEL ** -0.5 * BETA
    conv_w = jax.random.normal(ks[10], (DEPTH, CONV_WIDTH, 2 * D_FF), f32) * CONV_WIDTH ** -0.5
    conv_b = 0.02 * jax.random.normal(ks[11], (DEPTH, 2 * D_FF), f32)
    w_down = jax.random.normal(ks[12], (DEPTH, D_FF, D_MODEL), f32) * D_FF ** -0.5 * BETA
    ln2_g = 1.0 + 0.02 * jax.random.normal(ks[13], (DEPTH, D_MODEL), f32)
    ln2_b = 0.02 * jax.random.normal(ks[14], (DEPTH, D_MODEL), f32)
    return {"x": x, "c": c, "w_ada": w_ada, "b_ada": b_ada, "w_in": w_in,
            "b_f": b_f, "w_out": w_out, "ln1_g": ln1_g, "ln1_b": ln1_b,
            "w_up": w_up, "conv_w": conv_w, "conv_b": conv_b, "w_down": w_down,
            "ln2_g": ln2_g, "ln2_b": ln2_b}


def reference(x, c, w_ada, b_ada, w_in, b_f, w_out, ln1_g, ln1_b,
              w_up, conv_w, conv_b, w_down, ln2_g, ln2_b):
    B, S, D = x.shape
    for l in range(DEPTH):
        mod = jax.nn.silu(c) @ w_ada[l] + b_ada[l]
        sh1, sc1, g1, sh2, sc2, g2 = jnp.split(mod[:, None, :], 6, axis=-1)
        h = x * (1.0 + sc1) + sh1
        proj = h @ w_in[l]
        fq, fk, fv, ff, rq, rk, rv, rg = jnp.split(proj, IN_SPLITS, axis=-1)
        log_f = jax.nn.log_sigmoid(ff + b_f[l])
        fox = forgetting_attention(
            fq.reshape(B, S, FOX_HEADS, FOX_HEAD_DIM),
            fk.reshape(B, S, FOX_HEADS, FOX_HEAD_DIM),
            fv.reshape(B, S, FOX_HEADS, FOX_HEAD_DIM), log_f)
        ret = retention(
            rq.reshape(B, S, RET_HEADS, RET_HEAD_DIM),
            rk.reshape(B, S, RET_HEADS, RET_HEAD_DIM),
            rv.reshape(B, S, RET_HEADS, RET_HEAD_DIM), rg)
        mix = jnp.concatenate([fox, ret], axis=-1) @ w_out[l]
        x = layer_norm(ALPHA * x + g1 * mix, ln1_g[l], ln1_b[l])
        h = x * (1.0 + sc2) + sh2
        u = causal_depthwise_conv(h @ w_up[l], conv_w[l], conv_b[l])
        a, bv = jnp.split(u, 2, axis=-1)
        y = (jax.nn.gelu(a, approximate=False) * bv) @ w_down[l]
        x = layer_norm(ALPHA * x + g2 * y, ln2_g[l], ln2_b[l])
    return x
```

```python
import functools
import math

import jax
import jax.numpy as jnp
import numpy as np
from jax import lax
from jax.experimental import pallas as pl
from jax.experimental.pallas import tpu as pltpu

F32 = jnp.float32
BF16 = jnp.bfloat16

D_MODEL = 1024
SEQ = 16384
FOX_HEADS = 8
FOX_HEAD_DIM = 64
FOX_WIDTH = FOX_HEADS * FOX_HEAD_DIM
RET_HEADS = 4
RET_HEAD_DIM = 128
RET_WIDTH = RET_HEADS * RET_HEAD_DIM
D_FF = 2816
CONV_WIDTH = 3
RET_CHUNK = 128
ROPE_BASE = 10000.0
LN_EPS = 1e-5
GN_EPS = 1e-6
DEPTH = 1
ALPHA = (2 * DEPTH) ** 0.25

LANES = 128
SUBLANES = 8
VMEM_LIMIT = 56 * 1024 * 1024

FOX_PAD = FOX_HEADS * LANES
Q_CUM_LANE = FOX_HEAD_DIM
Q_ONE_LANE = FOX_HEAD_DIM + 3
K_ONE_LANE = FOX_HEAD_DIM
K_CUM_LANE = FOX_HEAD_DIM + 3
V_ONE_LANE = FOX_HEAD_DIM
NEG_BIG = -1e30

COL_Q = 0
COL_K = COL_Q + FOX_PAD
COL_V = COL_K + FOX_PAD
COL_F = COL_V + FOX_PAD
COL_RQ = COL_F + LANES
COL_RK = COL_RQ + RET_WIDTH
COL_RV = COL_RK + RET_WIDTH
COL_RG = COL_RV + RET_WIDTH
N_COLS = COL_RG + RET_WIDTH

TM_PROJ = 256
TQ = 256
TK = 256
TR = 512
TM_OUT = 512
TM_FFN = 256
FF_CHUNK = 256


def _const_spec(shape):
    return pl.BlockSpec(shape, lambda *_: (0,) * len(shape), pipeline_mode=pl.Buffered(1))


def _split3(x):
    p0 = x.astype(BF16).astype(F32)
    r1 = x - p0
    p1 = r1.astype(BF16).astype(F32)
    p2 = (r1 - p1).astype(BF16).astype(F32)
    return p0, p1, p2


def _layer_norm(y, g, b):
    mu = jnp.mean(y, axis=-1, keepdims=True)
    d = y - mu
    var = jnp.mean(d * d, axis=-1, keepdims=True)
    return d * lax.rsqrt(var + LN_EPS) * g + b


def _mod_kernel(c_ref, w_ref, b_ref, o_ref):
    c = c_ref[...]
    sc = c * jax.nn.sigmoid(c)
    o_ref[...] = jnp.sum(w_ref[...] * sc, axis=0, keepdims=True) + b_ref[...]


def _adaln_mod(c_col, w_ada, b_ada):
    d, n = w_ada.shape
    tn = 512
    return pl.pallas_call(
        _mod_kernel,
        grid=(n // tn,),
        in_specs=[pl.BlockSpec((d, 1), lambda j: (0, 0)),
                  pl.BlockSpec((d, tn), lambda j: (0, j)),
                  pl.BlockSpec((1, tn), lambda j: (0, j))],
        out_specs=pl.BlockSpec((1, tn), lambda j: (0, j)),
        out_shape=jax.ShapeDtypeStruct((1, n), F32),
        compiler_params=pltpu.CompilerParams(dimension_semantics=("arbitrary",)),
        name="adaln_mod",
    )(c_col, w_ada, b_ada)


def _in_proj_kernel(x_ref, sc_ref, sh_ref, w_ref, bf_ref, cos_ref, sin_ref, e_ref,
                    oq_ref, ok_ref, ov_ref,
                    q_out, k_out, v_out, rq_out, rk_out, rv_out, rg_out,
                    carry_ref):
    tm = x_ref.shape[0]

    @pl.when(pl.program_id(0) == 0)
    def _():
        carry_ref[...] = jnp.zeros_like(carry_ref)

    h = (x_ref[...] * (1.0 + sc_ref[...]) + sh_ref[...]).astype(BF16)

    def proj(c0, width):
        return jnp.dot(h, w_ref[:, c0:c0 + width], preferred_element_type=F32)

    lane = lax.broadcasted_iota(jnp.int32, (tm, LANES), 1)
    head_lane = lane < FOX_HEADS
    ff = proj(COL_F, LANES) + bf_ref[...]
    logf = jnp.minimum(ff, 0.0) - jnp.log1p(jnp.exp(-jnp.abs(ff)))
    logf = jnp.where(head_lane, logf, 0.0)
    p0, p1, p2 = _split3(logf)
    pieces = p0 + pltpu.roll(p1, FOX_HEADS, 1) + pltpu.roll(p2, 2 * FOX_HEADS, 1)
    row = lax.broadcasted_iota(jnp.int32, (tm, tm), 0)
    col = lax.broadcasted_iota(jnp.int32, (tm, tm), 1)
    tril = jnp.where(row >= col, 1.0, 0.0).astype(BF16)
    c3 = jnp.dot(tril, pieces.astype(BF16), preferred_element_type=F32)
    cum = c3 + pltpu.roll(c3, LANES - FOX_HEADS, 1) + pltpu.roll(c3, LANES - 2 * FOX_HEADS, 1)
    cum = jnp.where(head_lane, cum, 0.0) + carry_ref[0:1, :]
    carry_ref[0:1, :] = cum[tm - 1:tm, :]
    c0_, c1_, c2_ = _split3(cum)
    cum_pieces = (c0_ + pltpu.roll(c1_, FOX_HEADS, 1) + pltpu.roll(c2_, 2 * FOX_HEADS, 1)).astype(BF16)
    aug = jnp.dot(cum_pieces, e_ref[...], preferred_element_type=F32)

    scale = FOX_HEAD_DIM ** -0.5
    q_out[...] = (proj(COL_Q, FOX_PAD) * scale + aug[:, :FOX_PAD] + oq_ref[...]).astype(BF16)
    k_out[...] = (proj(COL_K, FOX_PAD) + aug[:, FOX_PAD:] + ok_ref[...]).astype(BF16)
    v_out[...] = (proj(COL_V, FOX_PAD) + ov_ref[...]).astype(BF16)

    cos = cos_ref[...]
    sin = sin_ref[...]
    half = RET_HEAD_DIM // 2
    kscale = RET_HEAD_DIM ** -0.5
    for hh in range(RET_HEADS):
        lo = hh * RET_HEAD_DIM
        a = proj(COL_RQ + lo, RET_HEAD_DIM)
        rq_out[:, lo:lo + RET_HEAD_DIM] = (a * cos + pltpu.roll(a, half, 1) * sin).astype(BF16)
        b = proj(COL_RK + lo, RET_HEAD_DIM)
        rk_out[:, lo:lo + RET_HEAD_DIM] = ((b * cos + pltpu.roll(b, half, 1) * sin) * kscale).astype(BF16)
    rv_out[...] = proj(COL_RV, RET_WIDTH).astype(BF16)
    rg_out[...] = proj(COL_RG, RET_WIDTH).astype(BF16)


def _in_proj(x, sc1, sh1, w_all, bf_row, cos, sin_s, e_mat, ones_q, ones_k, ones_v):
    s, d = x.shape
    tm = TM_PROJ
    row = lambda w: pl.BlockSpec((tm, w), lambda i: (i, 0))
    out_shapes = ([jax.ShapeDtypeStruct((s, FOX_PAD), BF16)] * 3
                  + [jax.ShapeDtypeStruct((s, RET_WIDTH), BF16)] * 4)
    return pl.pallas_call(
        _in_proj_kernel,
        grid=(s // tm,),
        in_specs=[row(d), _const_spec((1, d)), _const_spec((1, d)),
                  _const_spec((d, N_COLS)), _const_spec((1, LANES)),
                  row(RET_HEAD_DIM), row(RET_HEAD_DIM),
                  _const_spec((LANES, 2 * FOX_PAD)),
                  _const_spec((1, FOX_PAD)), _const_spec((1, FOX_PAD)), _const_spec((1, FOX_PAD))],
        out_specs=[row(FOX_PAD)] * 3 + [row(RET_WIDTH)] * 4,
        out_shape=out_shapes,
        scratch_shapes=[pltpu.VMEM((SUBLANES, LANES), F32)],
        compiler_params=pltpu.CompilerParams(dimension_semantics=("arbitrary",),
                                             vmem_limit_bytes=VMEM_LIMIT),
        name="in_proj",
    )(x, sc1, sh1, w_all, bf_row, cos, sin_s, e_mat, ones_q, ones_k, ones_v)


def _fox_kernel(q_ref, k_ref, v_ref, o_ref, m_ref, acc_ref):
    i = pl.program_id(1)
    q = q_ref[...]
    m_ref[...] = jnp.full_like(m_ref, NEG_BIG)
    acc_ref[...] = jnp.zeros_like(acc_ref)

    def step(j, masked):
        start = pl.multiple_of(j * TK, TK)
        k = k_ref[pl.ds(start, TK), :]
        v = v_ref[pl.ds(start, TK), :]
        s = lax.dot_general(q, k, (((1,), (1,)), ((), ())), preferred_element_type=F32)
        if masked:
            r = lax.broadcasted_iota(jnp.int32, (TQ, TK), 0)
            c = lax.broadcasted_iota(jnp.int32, (TQ, TK), 1)
            s = jnp.where(c <= r, s, NEG_BIG)
        m_prev = m_ref[...]
        m_new = jnp.maximum(m_prev, jnp.max(s, axis=1, keepdims=True))
        alpha = jnp.exp(m_prev - m_new)
        p = jnp.exp(s - m_new).astype(BF16)
        acc_ref[...] = alpha * acc_ref[...] + jnp.dot(p, v, preferred_element_type=F32)
        m_ref[...] = m_new

    def body(j, carry):
        step(j, False)
        return carry

    lax.fori_loop(0, i, body, 0)
    step(i, True)

    acc = acc_ref[...]
    o_ref[...] = (acc / acc[:, V_ONE_LANE:V_ONE_LANE + 1]).astype(o_ref.dtype)


def _fox_attention(q_aug, k_aug, v_aug):
    s = q_aug.shape[0]
    assert TQ == TK
    return pl.pallas_call(
        _fox_kernel,
        grid=(FOX_HEADS, s // TQ),
        in_specs=[pl.BlockSpec((TQ, LANES), lambda h, i: (i, h)),
                  pl.BlockSpec((s, LANES), lambda h, i: (0, h)),
                  pl.BlockSpec((s, LANES), lambda h, i: (0, h))],
        out_specs=pl.BlockSpec((TQ, LANES), lambda h, i: (i, h)),
        out_shape=jax.ShapeDtypeStruct((s, FOX_PAD), BF16),
        scratch_shapes=[pltpu.VMEM((TQ, 1), F32), pltpu.VMEM((TQ, LANES), F32)],
        compiler_params=pltpu.CompilerParams(dimension_semantics=("arbitrary", "arbitrary"),
                                             vmem_limit_bytes=VMEM_LIMIT),
        name="fox_attention",
    )(q_aug, k_aug, v_aug)


def _ret_kernel(q_ref, k_ref, v_ref, g_ref, inner_ref, xi_ref, zeta_ref, gc_ref, o_ref, r_ref):
    @pl.when(pl.program_id(1) == 0)
    def _():
        r_ref[...] = jnp.zeros_like(r_ref)

    inner = inner_ref[0]
    xi = xi_ref[0]
    zeta = zeta_ref[0]
    gc = gc_ref[0]
    c = RET_CHUNK
    for ci in range(q_ref.shape[0] // c):
        rows = slice(ci * c, (ci + 1) * c)
        q = q_ref[rows, :]
        k = k_ref[rows, :]
        v = v_ref[rows, :]
        r_state = r_ref[...]
        s = lax.dot_general(q, k, (((1,), (1,)), ((), ())), preferred_element_type=F32) * inner
        o = (jnp.dot(s.astype(BF16), v, preferred_element_type=F32)
             + jnp.dot(q, r_state.astype(BF16), preferred_element_type=F32) * xi)
        kz = (k.astype(F32) * zeta).astype(BF16)
        r_ref[...] = r_state * gc + lax.dot_general(kz, v, (((0,), (0,)), ((), ())),
                                                    preferred_element_type=F32)
        mu = jnp.mean(o, axis=-1, keepdims=True)
        d = o - mu
        var = jnp.mean(d * d, axis=-1, keepdims=True)
        on = d * lax.rsqrt(var + GN_EPS)
        g = g_ref[rows, :].astype(F32)
        o_ref[rows, :] = (g * jax.nn.sigmoid(g) * on).astype(o_ref.dtype)


def _retention(rq, rk, rv, rg, inner, xi_b, zeta_b, gc_b):
    s = rq.shape[0]
    blk = pl.BlockSpec((TR, RET_HEAD_DIM), lambda h, i: (i, h))
    tab = pl.BlockSpec((1, RET_CHUNK, RET_CHUNK), lambda h, i: (h, 0, 0))
    return pl.pallas_call(
        _ret_kernel,
        grid=(RET_HEADS, s // TR),
        in_specs=[blk, blk, blk, blk, tab, tab, tab,
                  pl.BlockSpec((1, 1, RET_HEAD_DIM), lambda h, i: (h, 0, 0))],
        out_specs=blk,
        out_shape=jax.ShapeDtypeStruct((s, RET_WIDTH), BF16),
        scratch_shapes=[pltpu.VMEM((RET_HEAD_DIM, RET_HEAD_DIM), F32)],
        compiler_params=pltpu.CompilerParams(dimension_semantics=("arbitrary", "arbitrary")),
        name="retention",
    )(rq, rk, rv, rg, inner, xi_b, zeta_b, gc_b)


def _out_proj_kernel(fox_ref, ret_ref, x_ref, wf_ref, wr_ref, g1_ref, lg_ref, lb_ref, o_ref):
    mix = (jnp.dot(fox_ref[...], wf_ref[...], preferred_element_type=F32)
           + jnp.dot(ret_ref[...], wr_ref[...], preferred_element_type=F32))
    y = ALPHA * x_ref[...] + g1_ref[...] * mix
    o_ref[...] = _layer_norm(y, lg_ref[...], lb_ref[...])


def _out_proj_ln(fox, ret, x, w_fox, w_ret, g1, ln_g, ln_b):
    s, d = x.shape
    tm = TM_OUT
    row = lambda w: pl.BlockSpec((tm, w), lambda i: (i, 0))
    return pl.pallas_call(
        _out_proj_kernel,
        grid=(s // tm,),
        in_specs=[row(FOX_PAD), row(RET_WIDTH), row(d),
                  _const_spec((FOX_PAD, d)), _const_spec((RET_WIDTH, d)),
                  _const_spec((1, d)), _const_spec((1, d)), _const_spec((1, d))],
        out_specs=row(d),
        out_shape=jax.ShapeDtypeStruct((s, d), F32),
        compiler_params=pltpu.CompilerParams(dimension_semantics=("arbitrary",),
                                             vmem_limit_bytes=VMEM_LIMIT),
        name="out_proj_ln",
    )(fox, ret, x, w_fox, w_ret, g1, ln_g, ln_b)


def _ffn_kernel(x_ref, sc_ref, sh_ref, g2_ref, wu_ref, cw_ref, cb_ref, wd_ref, lg_ref, lb_ref,
                o_ref, carry_ref, buf_ref):
    tm = x_ref.shape[0]

    @pl.when(pl.program_id(0) == 0)
    def _():
        carry_ref[...] = jnp.zeros_like(carry_ref)

    x = x_ref[...]
    h = (x * (1.0 + sc_ref[...]) + sh_ref[...]).astype(BF16)

    def conv_up(c0, slot):
        cols = slice(c0, c0 + FF_CHUNK)
        up = jnp.dot(h, wu_ref[:, cols], preferred_element_type=F32)
        buf_ref[slot, 0:SUBLANES, :] = carry_ref[:, cols]
        buf_ref[slot, SUBLANES:SUBLANES + tm, :] = up
        carry_ref[:, cols] = up[tm - SUBLANES:tm, :]
        u = cb_ref[:, cols] + buf_ref[slot, SUBLANES - 2:SUBLANES - 2 + tm, :] * cw_ref[0:1, cols]
        u = u + buf_ref[slot, SUBLANES - 1:SUBLANES - 1 + tm, :] * cw_ref[1:2, cols]
        return u + up * cw_ref[2:3, cols]

    acc = jnp.zeros((tm, D_MODEL), F32)
    for ci in range(D_FF // FF_CHUNK):
        a = conv_up(ci * FF_CHUNK, 0)
        b = conv_up(D_FF + ci * FF_CHUNK, 1)
        y = (0.5 * a * (1.0 + lax.erf(a * (2.0 ** -0.5))) * b).astype(BF16)
        acc = acc + jnp.dot(y, wd_ref[ci * FF_CHUNK:(ci + 1) * FF_CHUNK, :],
                            preferred_element_type=F32)

    y = ALPHA * x + g2_ref[...] * acc
    o_ref[...] = _layer_norm(y, lg_ref[...], lb_ref[...])


def _ffn_ln(x1, sc2, sh2, g2, w_up, conv_w, conv_b, w_down, ln_g, ln_b):
    s, d = x1.shape
    tm = TM_FFN
    row = pl.BlockSpec((tm, d), lambda i: (i, 0))
    return pl.pallas_call(
        _ffn_kernel,
        grid=(s // tm,),
        in_specs=[row, _const_spec((1, d)), _const_spec((1, d)), _const_spec((1, d)),
                  _const_spec((d, 2 * D_FF)), _const_spec((CONV_WIDTH, 2 * D_FF)),
                  _const_spec((1, 2 * D_FF)), _const_spec((D_FF, d)),
                  _const_spec((1, d)), _const_spec((1, d))],
        out_specs=row,
        out_shape=jax.ShapeDtypeStruct((s, d), F32),
        scratch_shapes=[pltpu.VMEM((SUBLANES, 2 * D_FF), F32),
                        pltpu.VMEM((2, SUBLANES + tm, FF_CHUNK), F32)],
        compiler_params=pltpu.CompilerParams(dimension_semantics=("arbitrary",),
                                             vmem_limit_bytes=VMEM_LIMIT),
        name="ffn_ln",
    )(x1, sc2, sh2, g2, w_up, conv_w, conv_b, w_down, ln_g, ln_b)


def _placement_tables():
    e = np.zeros((LANES, 2 * FOX_PAD), np.float32)
    ones_q = np.zeros((1, FOX_PAD), np.float32)
    ones_k = np.zeros((1, FOX_PAD), np.float32)
    ones_v = np.zeros((1, FOX_PAD), np.float32)
    for h in range(FOX_HEADS):
        base = h * LANES
        for j in range(3):
            e[j * FOX_HEADS + h, base + Q_CUM_LANE + j] = 1.0
            e[j * FOX_HEADS + h, FOX_PAD + base + K_CUM_LANE + j] = -1.0
            ones_q[0, base + Q_ONE_LANE + j] = 1.0
            ones_k[0, base + K_ONE_LANE + j] = 1.0
        ones_v[0, base + V_ONE_LANE] = 1.0
    return jnp.asarray(e, BF16), jnp.asarray(ones_q), jnp.asarray(ones_k), jnp.asarray(ones_v)


def _rotation_tables(s):
    dk = RET_HEAD_DIM
    pos = jnp.arange(s, dtype=F32)
    inv_freq = ROPE_BASE ** (-jnp.arange(0, dk, 2, dtype=F32) / dk)
    ang = pos[:, None] * inv_freq[None, :]
    ang = jnp.concatenate([ang, ang], axis=-1)
    sign = jnp.concatenate([-jnp.ones((dk // 2,), F32), jnp.ones((dk // 2,), F32)])
    return jnp.cos(ang), jnp.sin(ang) * sign[None, :]


def _decay_tables():
    c = RET_CHUNK
    log_gamma = jnp.log1p(-jnp.exp2(-5.0 - jnp.arange(RET_HEADS, dtype=F32)))
    idx = jnp.arange(c, dtype=F32)
    diff = idx[:, None] - idx[None, :]
    inner = jnp.where(diff[None] >= 0,
                      jnp.exp(jnp.maximum(diff, 0.0)[None] * log_gamma[:, None, None]), 0.0)
    xi = jnp.exp((idx[None, :] + 1.0) * log_gamma[:, None])
    zeta = jnp.exp((c - 1.0 - idx[None, :]) * log_gamma[:, None])
    g_chunk = jnp.exp(c * log_gamma)
    bshape = (RET_HEADS, c, RET_HEAD_DIM)
    return (inner,
            jnp.broadcast_to(xi[:, :, None], bshape),
            jnp.broadcast_to(zeta[:, :, None], bshape),
            jnp.broadcast_to(g_chunk[:, None, None], (RET_HEADS, 1, RET_HEAD_DIM)))


def _pad_heads_cols(w):
    d = w.shape[0]
    w = w.reshape(d, FOX_HEADS, FOX_HEAD_DIM)
    w = jnp.pad(w, ((0, 0), (0, 0), (0, LANES - FOX_HEAD_DIM)))
    return w.reshape(d, FOX_PAD)


def kernel(x, c, w_ada, b_ada, w_in, b_f, w_out, ln1_g, ln1_b, w_up, conv_w, conv_b, w_down, ln2_g, ln2_b):
    b, s, d = x.shape
    assert (b, s, d) == (1, SEQ, D_MODEL) and w_ada.shape[0] == DEPTH
    xs = x[0]

    wi = w_in[0]
    o = 0
    parts = {}
    for name, width in (("fq", FOX_WIDTH), ("fk", FOX_WIDTH), ("fv", FOX_WIDTH), ("ff", FOX_HEADS),
                        ("rq", RET_WIDTH), ("rk", RET_WIDTH), ("rv", RET_WIDTH), ("rg", RET_WIDTH)):
        parts[name] = wi[:, o:o + width]
        o += width
    w_all = jnp.concatenate(
        [_pad_heads_cols(parts["fq"]), _pad_heads_cols(parts["fk"]), _pad_heads_cols(parts["fv"]),
         jnp.pad(parts["ff"], ((0, 0), (0, LANES - FOX_HEADS))),
         parts["rq"], parts["rk"], parts["rv"], parts["rg"]], axis=1).astype(BF16)
    bf_row = jnp.pad(b_f[0][None, :], ((0, 0), (0, LANES - FOX_HEADS)))
    wo = w_out[0]
    w_fox = jnp.pad(wo[:FOX_WIDTH].reshape(FOX_HEADS, FOX_HEAD_DIM, d),
                    ((0, 0), (0, LANES - FOX_HEAD_DIM), (0, 0))).reshape(FOX_PAD, d).astype(BF16)
    w_ret = wo[FOX_WIDTH:].astype(BF16)

    e_mat, ones_q, ones_k, ones_v = _placement_tables()
    cos, sin_s = _rotation_tables(s)
    inner, xi_b, zeta_b, gc_b = _decay_tables()

    mod = _adaln_mod(c.reshape(d, 1), w_ada[0], b_ada[0][None, :])
    sh1, sc1, g1, sh2, sc2, g2 = [mod[:, i * d:(i + 1) * d] for i in range(6)]

    q_aug, k_aug, v_aug, rq, rk, rv, rg = _in_proj(
        xs, sc1, sh1, w_all, bf_row, cos, sin_s, e_mat, ones_q, ones_k, ones_v)
    fox = _fox_attention(q_aug, k_aug, v_aug)
    ret = _retention(rq, rk, rv, rg, inner, xi_b, zeta_b, gc_b)
    x1 = _out_proj_ln(fox, ret, xs, w_fox, w_ret, g1, ln1_g[0][None, :], ln1_b[0][None, :])
    out = _ffn_ln(x1, sc2, sh2, g2, w_up[0].astype(BF16), conv_w[0], conv_b[0][None, :],
                  w_down[0].astype(BF16), ln2_g[0][None, :], ln2_b[0][None, :])
    return out[None]
```

```python
import functools
import math

import jax
import jax.numpy as jnp
import numpy as np
from jax import lax
from jax.experimental import pallas as pl
from jax.experimental.pallas import tpu as pltpu

F32 = jnp.float32
BF16 = jnp.bfloat16

D_MODEL = 1024
SEQ = 16384
FOX_HEADS = 8
FOX_HEAD_DIM = 64
FOX_WIDTH = FOX_HEADS * FOX_HEAD_DIM
RET_HEADS = 4
RET_HEAD_DIM = 128
RET_WIDTH = RET_HEADS * RET_HEAD_DIM
D_FF = 2816
CONV_WIDTH = 3
RET_CHUNK = 128
ROPE_BASE = 10000.0
LN_EPS = 1e-5
GN_EPS = 1e-6
DEPTH = 1
ALPHA = (2 * DEPTH) ** 0.25

LANES = 128
SUBLANES = 8
VMEM_LIMIT = 56 * 1024 * 1024

FOX_PAD = FOX_HEADS * LANES
Q_CUM_LANE = FOX_HEAD_DIM
Q_ONE_LANE = FOX_HEAD_DIM + 3
K_ONE_LANE = FOX_HEAD_DIM
K_CUM_LANE = FOX_HEAD_DIM + 3
V_ONE_LANE = FOX_HEAD_DIM
NEG_BIG = -1e30
LOG2_E = math.log2(math.e)

COL_Q = 0
COL_K = COL_Q + FOX_PAD
COL_F = COL_K + FOX_PAD
COL_RQ = COL_F + LANES
COL_RK = COL_RQ + RET_WIDTH
COL_RV = COL_RK + RET_WIDTH
COL_RG = COL_RV + RET_WIDTH
N_COLS = COL_RG + RET_WIDTH

TM_PROJ = 256
TQ = 512
TK = 1024
TR = 512
TM_OUT = 512
TM_FFN = 256
FF_CHUNK = 256


def _const_spec(shape):
    return pl.BlockSpec(shape, lambda *_: (0,) * len(shape), pipeline_mode=pl.Buffered(1))


def _split3(x):
    p0 = x.astype(BF16).astype(F32)
    r1 = x - p0
    p1 = r1.astype(BF16).astype(F32)
    p2 = (r1 - p1).astype(BF16).astype(F32)
    return p0, p1, p2


def _layer_norm(y, g, b):
    mu = jnp.mean(y, axis=-1, keepdims=True)
    d = y - mu
    var = jnp.mean(d * d, axis=-1, keepdims=True)
    return d * lax.rsqrt(var + LN_EPS) * g + b


def _mod_kernel(c_ref, w_ref, b_ref, o_ref):
    c = c_ref[...]
    sc = c * jax.nn.sigmoid(c)
    o_ref[...] = jnp.sum(w_ref[...] * sc, axis=0, keepdims=True) + b_ref[...]


def _adaln_mod(c_col, w_ada, b_ada):
    d, n = w_ada.shape
    tn = 512
    return pl.pallas_call(
        _mod_kernel,
        grid=(n // tn,),
        in_specs=[pl.BlockSpec((d, 1), lambda j: (0, 0)),
                  pl.BlockSpec((d, tn), lambda j: (0, j)),
                  pl.BlockSpec((1, tn), lambda j: (0, j))],
        out_specs=pl.BlockSpec((1, tn), lambda j: (0, j)),
        out_shape=jax.ShapeDtypeStruct((1, n), F32),
        compiler_params=pltpu.CompilerParams(dimension_semantics=("arbitrary",)),
        name="adaln_mod",
    )(c_col, w_ada, b_ada)


def _in_proj_kernel(x_ref, sc_ref, sh_ref, w_ref, wvt_ref, bf_ref, cos_ref, sin_ref, e_ref,
                    oq_ref, ok_ref,
                    q_out, k_out, vt_out, rq_out, rk_out, rv_out, rg_out,
                    carry_ref):
    tm = x_ref.shape[0]

    @pl.when(pl.program_id(0) == 0)
    def _():
        carry_ref[...] = jnp.zeros_like(carry_ref)

    h = (x_ref[...] * (1.0 + sc_ref[...]) + sh_ref[...]).astype(BF16)

    def proj(c0, width):
        return jnp.dot(h, w_ref[:, c0:c0 + width], preferred_element_type=F32)

    lane = lax.broadcasted_iota(jnp.int32, (tm, LANES), 1)
    head_lane = lane < FOX_HEADS
    ff = proj(COL_F, LANES) + bf_ref[...]
    logf = jnp.minimum(ff, 0.0) - jnp.log1p(jnp.exp(-jnp.abs(ff)))
    logf = jnp.where(head_lane, logf, 0.0)
    p0, p1, p2 = _split3(logf)
    pieces = p0 + pltpu.roll(p1, FOX_HEADS, 1) + pltpu.roll(p2, 2 * FOX_HEADS, 1)
    row = lax.broadcasted_iota(jnp.int32, (tm, tm), 0)
    col = lax.broadcasted_iota(jnp.int32, (tm, tm), 1)
    tril = jnp.where(row >= col, 1.0, 0.0).astype(BF16)
    c3 = jnp.dot(tril, pieces.astype(BF16), preferred_element_type=F32)
    cum = c3 + pltpu.roll(c3, LANES - FOX_HEADS, 1) + pltpu.roll(c3, LANES - 2 * FOX_HEADS, 1)
    cum = jnp.where(head_lane, cum, 0.0) + carry_ref[0:1, :]
    carry_ref[0:1, :] = cum[tm - 1:tm, :]
    c0_, c1_, c2_ = _split3(cum * LOG2_E)
    cum_pieces = (c0_ + pltpu.roll(c1_, FOX_HEADS, 1) + pltpu.roll(c2_, 2 * FOX_HEADS, 1)).astype(BF16)
    aug = jnp.dot(cum_pieces, e_ref[...], preferred_element_type=F32)

    scale = FOX_HEAD_DIM ** -0.5 * LOG2_E
    q_out[...] = (proj(COL_Q, FOX_PAD) * scale + aug[:, :FOX_PAD] + oq_ref[...]).astype(BF16)
    k_out[...] = (proj(COL_K, FOX_PAD) + aug[:, FOX_PAD:] + ok_ref[...]).astype(BF16)
    vt = lax.dot_general(wvt_ref[...], h, (((1,), (1,)), ((), ())), preferred_element_type=F32)
    vrow = lax.broadcasted_iota(jnp.int32, (FOX_PAD, tm), 0)
    vt_out[...] = jnp.where((vrow & (LANES - 1)) == V_ONE_LANE, 1.0, vt).astype(BF16)

    cos = cos_ref[...]
    sin = sin_ref[...]
    half = RET_HEAD_DIM // 2
    kscale = RET_HEAD_DIM ** -0.5
    for hh in range(RET_HEADS):
        lo = hh * RET_HEAD_DIM
        a = proj(COL_RQ + lo, RET_HEAD_DIM)
        rq_out[:, lo:lo + RET_HEAD_DIM] = (a * cos + pltpu.roll(a, half, 1) * sin).astype(BF16)
        b = proj(COL_RK + lo, RET_HEAD_DIM)
        rk_out[:, lo:lo + RET_HEAD_DIM] = ((b * cos + pltpu.roll(b, half, 1) * sin) * kscale).astype(BF16)
    rv_out[...] = proj(COL_RV, RET_WIDTH).astype(BF16)
    rg_out[...] = proj(COL_RG, RET_WIDTH).astype(BF16)


def _in_proj(x, sc1, sh1, w_all, wvt, bf_row, cos, sin_s, e_mat, ones_q, ones_k):
    s, d = x.shape
    tm = TM_PROJ
    row = lambda w: pl.BlockSpec((tm, w), lambda i: (i, 0))
    out_shapes = ([jax.ShapeDtypeStruct((s, FOX_PAD), BF16)] * 2
                  + [jax.ShapeDtypeStruct((FOX_PAD, s), BF16)]
                  + [jax.ShapeDtypeStruct((s, RET_WIDTH), BF16)] * 4)
    return pl.pallas_call(
        _in_proj_kernel,
        grid=(s // tm,),
        in_specs=[row(d), _const_spec((1, d)), _const_spec((1, d)),
                  _const_spec((d, N_COLS)), _const_spec((FOX_PAD, d)), _const_spec((1, LANES)),
                  row(RET_HEAD_DIM), row(RET_HEAD_DIM),
                  _const_spec((LANES, 2 * FOX_PAD)),
                  _const_spec((1, FOX_PAD)), _const_spec((1, FOX_PAD))],
        out_specs=([row(FOX_PAD)] * 2 + [pl.BlockSpec((FOX_PAD, tm), lambda i: (0, i))]
                   + [row(RET_WIDTH)] * 4),
        out_shape=out_shapes,
        scratch_shapes=[pltpu.VMEM((SUBLANES, LANES), F32)],
        compiler_params=pltpu.CompilerParams(dimension_semantics=("arbitrary",),
                                             vmem_limit_bytes=VMEM_LIMIT),
        name="in_proj",
    )(x, sc1, sh1, w_all, wvt, bf_row, cos, sin_s, e_mat, ones_q, ones_k)


def _fox_kernel(q_ref, k_ref, vt_ref, o_ref, m_ref, acc_ref, sa_ref, ca_ref, sb_ref, cb_ref):
    i = pl.program_id(1)
    q = q_ref[...]
    m_ref[...] = jnp.full_like(m_ref, NEG_BIG)
    acc_ref[...] = jnp.zeros_like(acc_ref)
    bufs = ((sa_ref, ca_ref), (sb_ref, cb_ref))

    def scores(j, buf):
        s_ref, c_ref = buf
        start = pl.multiple_of(j * TK, TK)
        k = k_ref[pl.ds(start, TK), :]
        st = lax.dot_general(k, q, (((1,), (1,)), ((), ())), preferred_element_type=F32)
        s_ref[...] = st
        c_ref[...] = jnp.max(st, axis=0, keepdims=True)

    def accumulate(j, buf, masked):
        s_ref, c_ref = buf
        start = pl.multiple_of(j * TK, TK)
        vt = vt_ref[:, pl.ds(start, TK)]
        st = s_ref[...]
        if masked:
            key = lax.broadcasted_iota(jnp.int32, (TK, TQ), 0) + start
            qry = lax.broadcasted_iota(jnp.int32, (TK, TQ), 1) + i * TQ
            st = jnp.where(key <= qry, st, NEG_BIG)
            cmax = jnp.max(st, axis=0, keepdims=True)
        else:
            cmax = c_ref[...]
        m_prev = m_ref[...]
        m_new = jnp.maximum(m_prev, cmax)
        alpha = jnp.exp2(m_prev - m_new)
        pt = jnp.exp2(st - m_new).astype(BF16)
        acc_ref[...] = alpha * acc_ref[...] + jnp.dot(vt, pt, preferred_element_type=F32)
        m_ref[...] = m_new

    nfull = (i * TQ) // TK
    scores(0, bufs[0])

    def pair(jj, carry):
        j = 2 * jj
        scores(j + 1, bufs[1])
        accumulate(j, bufs[0], False)
        scores(j + 2, bufs[0])
        accumulate(j + 1, bufs[1], False)
        return carry

    lax.fori_loop(0, nfull // 2, pair, 0)

    @pl.when(nfull % 2 == 1)
    def _():
        scores(nfull, bufs[1])
        accumulate(nfull - 1, bufs[0], False)
        accumulate(nfull, bufs[1], True)

    @pl.when(nfull % 2 == 0)
    def _():
        accumulate(nfull, bufs[0], True)

    acc = acc_ref[...]
    ot = acc / acc[V_ONE_LANE:V_ONE_LANE + 1, :]
    o_ref[...] = ot.T.astype(o_ref.dtype)


def _fox_attention(q_aug, k_aug, vt_aug):
    s = q_aug.shape[0]
    assert TK % TQ == 0 and s % TK == 0
    return pl.pallas_call(
        _fox_kernel,
        grid=(FOX_HEADS, s // TQ),
        in_specs=[pl.BlockSpec((TQ, LANES), lambda h, i: (i, h)),
                  pl.BlockSpec((s, LANES), lambda h, i: (0, h)),
                  pl.BlockSpec((LANES, s), lambda h, i: (h, 0))],
        out_specs=pl.BlockSpec((TQ, LANES), lambda h, i: (i, h)),
        out_shape=jax.ShapeDtypeStruct((s, FOX_PAD), BF16),
        scratch_shapes=[pltpu.VMEM((1, TQ), F32), pltpu.VMEM((LANES, TQ), F32),
                        pltpu.VMEM((TK, TQ), F32), pltpu.VMEM((1, TQ), F32),
                        pltpu.VMEM((TK, TQ), F32), pltpu.VMEM((1, TQ), F32)],
        compiler_params=pltpu.CompilerParams(dimension_semantics=("arbitrary", "arbitrary"),
                                             vmem_limit_bytes=VMEM_LIMIT),
        name="fox_attention",
    )(q_aug, k_aug, vt_aug)


def _ret_kernel(q_ref, k_ref, v_ref, g_ref, inner_ref, xi_ref, zeta_ref, gc_ref, o_ref, r_ref):
    @pl.when(pl.program_id(1) == 0)
    def _():
        r_ref[...] = jnp.zeros_like(r_ref)

    inner = inner_ref[0]
    xi = xi_ref[0]
    zeta = zeta_ref[0]
    gc = gc_ref[0]
    c = RET_CHUNK
    for ci in range(q_ref.shape[0] // c):
        rows = slice(ci * c, (ci + 1) * c)
        q = q_ref[rows, :]
        k = k_ref[rows, :]
        v = v_ref[rows, :]
        r_state = r_ref[...]
        s = lax.dot_general(q, k, (((1,), (1,)), ((), ())), preferred_element_type=F32) * inner
        o = (jnp.dot(s.astype(BF16), v, preferred_element_type=F32)
             + jnp.dot(q, r_state.astype(BF16), preferred_element_type=F32) * xi)
        kz = (k.astype(F32) * zeta).astype(BF16)
        r_ref[...] = r_state * gc + lax.dot_general(kz, v, (((0,), (0,)), ((), ())),
                                                    preferred_element_type=F32)
        mu = jnp.mean(o, axis=-1, keepdims=True)
        d = o - mu
        var = jnp.mean(d * d, axis=-1, keepdims=True)
        on = d * lax.rsqrt(var + GN_EPS)
        g = g_ref[rows, :].astype(F32)
        o_ref[rows, :] = (g * jax.nn.sigmoid(g) * on).astype(o_ref.dtype)


def _retention(rq, rk, rv, rg, inner, xi_b, zeta_b, gc_b):
    s = rq.shape[0]
    blk = pl.BlockSpec((TR, RET_HEAD_DIM), lambda h, i: (i, h))
    tab = pl.BlockSpec((1, RET_CHUNK, RET_CHUNK), lambda h, i: (h, 0, 0))
    return pl.pallas_call(
        _ret_kernel,
        grid=(RET_HEADS, s // TR),
        in_specs=[blk, blk, blk, blk, tab, tab, tab,
                  pl.BlockSpec((1, 1, RET_HEAD_DIM), lambda h, i: (h, 0, 0))],
        out_specs=blk,
        out_shape=jax.ShapeDtypeStruct((s, RET_WIDTH), BF16),
        scratch_shapes=[pltpu.VMEM((RET_HEAD_DIM, RET_HEAD_DIM), F32)],
        compiler_params=pltpu.CompilerParams(dimension_semantics=("arbitrary", "arbitrary")),
        name="retention",
    )(rq, rk, rv, rg, inner, xi_b, zeta_b, gc_b)


def _out_proj_kernel(fox_ref, ret_ref, x_ref, wf_ref, wr_ref, g1_ref, lg_ref, lb_ref, o_ref):
    mix = (jnp.dot(fox_ref[...], wf_ref[...], preferred_element_type=F32)
           + jnp.dot(ret_ref[...], wr_ref[...], preferred_element_type=F32))
    y = ALPHA * x_ref[...] + g1_ref[...] * mix
    o_ref[...] = _layer_norm(y, lg_ref[...], lb_ref[...])


def _out_proj_ln(fox, ret, x, w_fox, w_ret, g1, ln_g, ln_b):
    s, d = x.shape
    tm = TM_OUT
    row = lambda w: pl.BlockSpec((tm, w), lambda i: (i, 0))
    return pl.pallas_call(
        _out_proj_kernel,
        grid=(s // tm,),
        in_specs=[row(FOX_PAD), row(RET_WIDTH), row(d),
                  _const_spec((FOX_PAD, d)), _const_spec((RET_WIDTH, d)),
                  _const_spec((1, d)), _const_spec((1, d)), _const_spec((1, d))],
        out_specs=row(d),
        out_shape=jax.ShapeDtypeStruct((s, d), F32),
        compiler_params=pltpu.CompilerParams(dimension_semantics=("arbitrary",),
                                             vmem_limit_bytes=VMEM_LIMIT),
        name="out_proj_ln",
    )(fox, ret, x, w_fox, w_ret, g1, ln_g, ln_b)


def _ffn_kernel(x_ref, sc_ref, sh_ref, g2_ref, wu_ref, cw_ref, cb_ref, wd_ref, lg_ref, lb_ref,
                o_ref, carry_ref, buf_ref):
    tm = x_ref.shape[0]

    @pl.when(pl.program_id(0) == 0)
    def _():
        carry_ref[...] = jnp.zeros_like(carry_ref)

    x = x_ref[...]
    h = (x * (1.0 + sc_ref[...]) + sh_ref[...]).astype(BF16)

    def conv_up(c0, slot):
        cols = slice(c0, c0 + FF_CHUNK)
        up = jnp.dot(h, wu_ref[:, cols], preferred_element_type=F32)
        buf_ref[slot, 0:SUBLANES, :] = carry_ref[:, cols]
        buf_ref[slot, SUBLANES:SUBLANES + tm, :] = up
        carry_ref[:, cols] = up[tm - SUBLANES:tm, :]
        u = cb_ref[:, cols] + buf_ref[slot, SUBLANES - 2:SUBLANES - 2 + tm, :] * cw_ref[0:1, cols]
        u = u + buf_ref[slot, SUBLANES - 1:SUBLANES - 1 + tm, :] * cw_ref[1:2, cols]
        return u + up * cw_ref[2:3, cols]

    acc = jnp.zeros((tm, D_MODEL), F32)
    for ci in range(D_FF // FF_CHUNK):
        a = conv_up(ci * FF_CHUNK, 0)
        b = conv_up(D_FF + ci * FF_CHUNK, 1)
        y = (0.5 * a * (1.0 + lax.erf(a * (2.0 ** -0.5))) * b).astype(BF16)
        acc = acc + jnp.dot(y, wd_ref[ci * FF_CHUNK:(ci + 1) * FF_CHUNK, :],
                            preferred_element_type=F32)

    y = ALPHA * x + g2_ref[...] * acc
    o_ref[...] = _layer_norm(y, lg_ref[...], lb_ref[...])


def _ffn_ln(x1, sc2, sh2, g2, w_up, conv_w, conv_b, w_down, ln_g, ln_b):
    s, d = x1.shape
    tm = TM_FFN
    row = pl.BlockSpec((tm, d), lambda i: (i, 0))
    return pl.pallas_call(
        _ffn_kernel,
        grid=(s // tm,),
        in_specs=[row, _const_spec((1, d)), _const_spec((1, d)), _const_spec((1, d)),
                  _const_spec((d, 2 * D_FF)), _const_spec((CONV_WIDTH, 2 * D_FF)),
                  _const_spec((1, 2 * D_FF)), _const_spec((D_FF, d)),
                  _const_spec((1, d)), _const_spec((1, d))],
        out_specs=row,
        out_shape=jax.ShapeDtypeStruct((s, d), F32),
        scratch_shapes=[pltpu.VMEM((SUBLANES, 2 * D_FF), F32),
                        pltpu.VMEM((2, SUBLANES + tm, FF_CHUNK), F32)],
        compiler_params=pltpu.CompilerParams(dimension_semantics=("arbitrary",),
                                             vmem_limit_bytes=VMEM_LIMIT),
        name="ffn_ln",
    )(x1, sc2, sh2, g2, w_up, conv_w, conv_b, w_down, ln_g, ln_b)


def _placement_tables():
    e = np.zeros((LANES, 2 * FOX_PAD), np.float32)
    ones_q = np.zeros((1, FOX_PAD), np.float32)
    ones_k = np.zeros((1, FOX_PAD), np.float32)
    for h in range(FOX_HEADS):
        base = h * LANES
        for j in range(3):
            e[j * FOX_HEADS + h, base + Q_CUM_LANE + j] = 1.0
            e[j * FOX_HEADS + h, FOX_PAD + base + K_CUM_LANE + j] = -1.0
            ones_q[0, base + Q_ONE_LANE + j] = 1.0
            ones_k[0, base + K_ONE_LANE + j] = 1.0
    return jnp.asarray(e, BF16), jnp.asarray(ones_q), jnp.asarray(ones_k)


def _rotation_tables(s):
    dk = RET_HEAD_DIM
    pos = jnp.arange(s, dtype=F32)
    inv_freq = ROPE_BASE ** (-jnp.arange(0, dk, 2, dtype=F32) / dk)
    ang = pos[:, None] * inv_freq[None, :]
    ang = jnp.concatenate([ang, ang], axis=-1)
    sign = jnp.concatenate([-jnp.ones((dk // 2,), F32), jnp.ones((dk // 2,), F32)])
    return jnp.cos(ang), jnp.sin(ang) * sign[None, :]


def _decay_tables():
    c = RET_CHUNK
    log_gamma = jnp.log1p(-jnp.exp2(-5.0 - jnp.arange(RET_HEADS, dtype=F32)))
    idx = jnp.arange(c, dtype=F32)
    diff = idx[:, None] - idx[None, :]
    inner = jnp.where(diff[None] >= 0,
                      jnp.exp(jnp.maximum(diff, 0.0)[None] * log_gamma[:, None, None]), 0.0)
    xi = jnp.exp((idx[None, :] + 1.0) * log_gamma[:, None])
    zeta = jnp.exp((c - 1.0 - idx[None, :]) * log_gamma[:, None])
    g_chunk = jnp.exp(c * log_gamma)
    bshape = (RET_HEADS, c, RET_HEAD_DIM)
    return (inner,
            jnp.broadcast_to(xi[:, :, None], bshape),
            jnp.broadcast_to(zeta[:, :, None], bshape),
            jnp.broadcast_to(g_chunk[:, None, None], (RET_HEADS, 1, RET_HEAD_DIM)))


def _pad_heads_cols(w):
    d = w.shape[0]
    w = w.reshape(d, FOX_HEADS, FOX_HEAD_DIM)
    w = jnp.pad(w, ((0, 0), (0, 0), (0, LANES - FOX_HEAD_DIM)))
    return w.reshape(d, FOX_PAD)


def kernel(x, c, w_ada, b_ada, w_in, b_f, w_out, ln1_g, ln1_b, w_up, conv_w, conv_b, w_down, ln2_g, ln2_b):
    b, s, d = x.shape
    assert (b, s, d) == (1, SEQ, D_MODEL) and w_ada.shape[0] == DEPTH
    xs = x[0]

    wi = w_in[0]
    o = 0
    parts = {}
    for name, width in (("fq", FOX_WIDTH), ("fk", FOX_WIDTH), ("fv", FOX_WIDTH), ("ff", FOX_HEADS),
                        ("rq", RET_WIDTH), ("rk", RET_WIDTH), ("rv", RET_WIDTH), ("rg", RET_WIDTH)):
        parts[name] = wi[:, o:o + width]
        o += width
    w_all = jnp.concatenate(
        [_pad_heads_cols(parts["fq"]), _pad_heads_cols(parts["fk"]),
         jnp.pad(parts["ff"], ((0, 0), (0, LANES - FOX_HEADS))),
         parts["rq"], parts["rk"], parts["rv"], parts["rg"]], axis=1).astype(BF16)
    wvt = _pad_heads_cols(parts["fv"]).T.astype(BF16)
    bf_row = jnp.pad(b_f[0][None, :], ((0, 0), (0, LANES - FOX_HEADS)))
    wo = w_out[0]
    w_fox = jnp.pad(wo[:FOX_WIDTH].reshape(FOX_HEADS, FOX_HEAD_DIM, d),
                    ((0, 0), (0, LANES - FOX_HEAD_DIM), (0, 0))).reshape(FOX_PAD, d).astype(BF16)
    w_ret = wo[FOX_WIDTH:].astype(BF16)

    e_mat, ones_q, ones_k = _placement_tables()
    cos, sin_s = _rotation_tables(s)
    inner, xi_b, zeta_b, gc_b = _decay_tables()

    mod = _adaln_mod(c.reshape(d, 1), w_ada[0], b_ada[0][None, :])
    sh1, sc1, g1, sh2, sc2, g2 = [mod[:, i * d:(i + 1) * d] for i in range(6)]

    q_aug, k_aug, vt_aug, rq, rk, rv, rg = _in_proj(
        xs, sc1, sh1, w_all, wvt, bf_row, cos, sin_s, e_mat, ones_q, ones_k)
    fox = _fox_attention(q_aug, k_aug, vt_aug)
    ret = _retention(rq, rk, rv, rg, inner, xi_b, zeta_b, gc_b)
    x1 = _out_proj_ln(fox, ret, xs, w_fox, w_ret, g1, ln1_g[0][None, :], ln1_b[0][None, :])
    out = _ffn_ln(x1, sc2, sh2, g2, w_up[0].astype(BF16), conv_w[0], conv_b[0][None, :],
                  w_down[0].astype(BF16), ln2_g[0][None, :], ln2_b[0][None, :])
    return out[None]
```

```python
import math

import jax
import jax.numpy as jnp
import numpy as np
from jax import lax
from jax.experimental import pallas as pl
from jax.experimental.pallas import tpu as pltpu

F32 = jnp.float32
BF16 = jnp.bfloat16

D_MODEL = 1024
SEQ = 16384
FOX_HEADS = 8
FOX_HEAD_DIM = 64
FOX_WIDTH = FOX_HEADS * FOX_HEAD_DIM
RET_HEADS = 4
RET_HEAD_DIM = 128
RET_WIDTH = RET_HEADS * RET_HEAD_DIM
D_FF = 2816
CONV_WIDTH = 3
RET_CHUNK = 128
ROPE_BASE = 10000.0
LN_EPS = 1e-5
GN_EPS = 1e-6
DEPTH = 1
ALPHA = (2 * DEPTH) ** 0.25

LANES = 128
SUBLANES = 8
BF16_ROWS = 16
VMEM_LIMIT = 56 * 1024 * 1024

FOX_PAIRS = FOX_HEADS // 2
BIAS_CUM = 0
BIAS_ONE = 3
N_PIECES = 3
PV_ROWS = FOX_HEAD_DIM + BF16_ROWS
NEG_BIG = -1e30
LOG2_E = math.log2(math.e)
RSQRT2 = 2.0 ** -0.5

COL_Q = 0
COL_K = COL_Q + FOX_WIDTH
COL_F = COL_K + FOX_WIDTH
COL_RQ = COL_F + LANES
COL_RK = COL_RQ + RET_WIDTH
COL_RV = COL_RK + RET_WIDTH
COL_RG = COL_RV + RET_WIDTH
N_COLS = COL_RG + RET_WIDTH

TM_PROJ = 256
TQ = 256
TK = 1024
TR = 512
TM_FFN = 512
FF_CHUNK = 256


def _const_spec(shape):
    return pl.BlockSpec(shape, lambda *_: (0,) * len(shape), pipeline_mode=pl.Buffered(1))


def _split3(x):
    p0 = x.astype(BF16).astype(F32)
    r1 = x - p0
    p1 = r1.astype(BF16).astype(F32)
    p2 = (r1 - p1).astype(BF16).astype(F32)
    return p0, p1, p2


def _layer_norm(y, g, b):
    mu = jnp.mean(y, axis=-1, keepdims=True)
    d = y - mu
    var = jnp.mean(d * d, axis=-1, keepdims=True)
    return d * lax.rsqrt(var + LN_EPS) * g + b


def _mod_kernel(c_ref, w_ref, b_ref, o_ref):
    c = c_ref[...]
    sc = c * jax.nn.sigmoid(c)
    o_ref[...] = jnp.sum(w_ref[...] * sc, axis=0, keepdims=True) + b_ref[...]


def _adaln_mod(c_col, w_ada, b_ada):
    d, n = w_ada.shape
    tn = 512
    return pl.pallas_call(
        _mod_kernel,
        grid=(n // tn,),
        in_specs=[pl.BlockSpec((d, 1), lambda j: (0, 0)),
                  pl.BlockSpec((d, tn), lambda j: (0, j)),
                  pl.BlockSpec((1, tn), lambda j: (0, j))],
        out_specs=pl.BlockSpec((1, tn), lambda j: (0, j)),
        out_shape=jax.ShapeDtypeStruct((1, n), F32),
        compiler_params=pltpu.CompilerParams(dimension_semantics=("arbitrary",)),
        name="adaln_mod",
    )(c_col, w_ada, b_ada)


def _in_proj_kernel(x_ref, sc_ref, sh_ref, w_ref, wvt_ref, bf_ref, cos_ref, sin_ref, e_ref,
                    oq_ref, ok_ref,
                    q_out, qb_out, k_out, kb_out, vt_out, rq_out, rk_out, rv_out, rg_out,
                    carry_ref):
    tm = x_ref.shape[0]

    @pl.when(pl.program_id(0) == 0)
    def _():
        carry_ref[...] = jnp.zeros_like(carry_ref)

    h = (x_ref[...] * (1.0 + sc_ref[...]) + sh_ref[...]).astype(BF16)

    def proj(c0, width):
        return jnp.dot(h, w_ref[:, c0:c0 + width], preferred_element_type=F32)

    lane = lax.broadcasted_iota(jnp.int32, (tm, LANES), 1)
    head_lane = lane < FOX_HEADS
    ff = proj(COL_F, LANES) + bf_ref[...]
    logf = jnp.minimum(ff, 0.0) - jnp.log1p(jnp.exp(-jnp.abs(ff)))
    logf = jnp.where(head_lane, logf, 0.0)
    p0, p1, p2 = _split3(logf)
    pieces = p0 + pltpu.roll(p1, FOX_HEADS, 1) + pltpu.roll(p2, 2 * FOX_HEADS, 1)
    row = lax.broadcasted_iota(jnp.int32, (tm, tm), 0)
    col = lax.broadcasted_iota(jnp.int32, (tm, tm), 1)
    tril = jnp.where(row >= col, 1.0, 0.0).astype(BF16)
    c3 = jnp.dot(tril, pieces.astype(BF16), preferred_element_type=F32)
    cum = c3 + pltpu.roll(c3, LANES - FOX_HEADS, 1) + pltpu.roll(c3, LANES - 2 * FOX_HEADS, 1)
    cum = jnp.where(head_lane, cum, 0.0) + carry_ref[0:1, :]
    carry_ref[0:1, :] = cum[tm - 1:tm, :]
    c0_, c1_, c2_ = _split3(cum * LOG2_E)
    cum_pieces = (c0_ + pltpu.roll(c1_, FOX_HEADS, 1) + pltpu.roll(c2_, 2 * FOX_HEADS, 1)).astype(BF16)
    bias = jnp.dot(cum_pieces, e_ref[...], preferred_element_type=F32)
    qb_out[...] = (bias[:, :FOX_WIDTH] + oq_ref[...]).astype(BF16)
    kb_out[...] = (bias[:, FOX_WIDTH:] + ok_ref[...]).astype(BF16)

    q_out[...] = (proj(COL_Q, FOX_WIDTH) * (FOX_HEAD_DIM ** -0.5 * LOG2_E)).astype(BF16)
    k_out[...] = proj(COL_K, FOX_WIDTH).astype(BF16)
    vt_out[...] = lax.dot_general(wvt_ref[...], h, (((1,), (1,)), ((), ())),
                                  preferred_element_type=F32).astype(BF16)

    cos = cos_ref[...]
    sin = sin_ref[...]
    half = RET_HEAD_DIM // 2
    kscale = RET_HEAD_DIM ** -0.5
    two = 2 * RET_HEAD_DIM
    for pair in range(RET_HEADS // 2):
        a2 = proj(COL_RQ + pair * two, two)
        b2 = proj(COL_RK + pair * two, two)
        for e in range(2):
            lo = pair * two + e * RET_HEAD_DIM
            a = a2[:, e * RET_HEAD_DIM:(e + 1) * RET_HEAD_DIM]
            b = b2[:, e * RET_HEAD_DIM:(e + 1) * RET_HEAD_DIM]
            rq_out[:, lo:lo + RET_HEAD_DIM] = (a * cos + pltpu.roll(a, half, 1) * sin).astype(BF16)
            rk_out[:, lo:lo + RET_HEAD_DIM] = ((b * cos + pltpu.roll(b, half, 1) * sin) * kscale).astype(BF16)
    rv_out[...] = proj(COL_RV, RET_WIDTH).astype(BF16)
    rg_out[...] = proj(COL_RG, RET_WIDTH).astype(BF16)


def _in_proj(x, sc1, sh1, w_all, wvt, bf_row, cos, sin_s, e_mat, ones_q, ones_k):
    s, d = x.shape
    tm = TM_PROJ
    row = lambda w: pl.BlockSpec((tm, w), lambda i: (i, 0))
    out_shapes = ([jax.ShapeDtypeStruct((s, FOX_WIDTH), BF16)] * 4
                  + [jax.ShapeDtypeStruct((FOX_WIDTH, s), BF16)]
                  + [jax.ShapeDtypeStruct((s, RET_WIDTH), BF16)] * 4)
    return pl.pallas_call(
        _in_proj_kernel,
        grid=(s // tm,),
        in_specs=[row(d), _const_spec((1, d)), _const_spec((1, d)),
                  _const_spec((d, N_COLS)), _const_spec((FOX_WIDTH, d)), _const_spec((1, LANES)),
                  row(RET_HEAD_DIM), row(RET_HEAD_DIM),
                  _const_spec((LANES, 2 * FOX_WIDTH)),
                  _const_spec((1, FOX_WIDTH)), _const_spec((1, FOX_WIDTH))],
        out_specs=([row(FOX_WIDTH)] * 4 + [pl.BlockSpec((FOX_WIDTH, tm), lambda i: (0, i))]
                   + [row(RET_WIDTH)] * 4),
        out_shape=out_shapes,
        scratch_shapes=[pltpu.VMEM((SUBLANES, LANES), F32)],
        compiler_params=pltpu.CompilerParams(dimension_semantics=("arbitrary",),
                                             vmem_limit_bytes=VMEM_LIMIT),
        name="in_proj",
    )(x, sc1, sh1, w_all, wvt, bf_row, cos, sin_s, e_mat, ones_q, ones_k)


def _fox_kernel(q_ref, qb_ref, k_ref, kb_ref, vt_ref, o_ref, kaug_ref, m_ref, acc_ref,
                sa_ref, ca_ref, sb_ref, cb_ref):
    i = pl.program_id(1)
    lane = lax.broadcasted_iota(jnp.int32, (1, LANES), 1)
    own = (lane < FOX_HEAD_DIM, lane >= FOX_HEAD_DIM)

    @pl.when(i == 0)
    def _():
        for e in range(2):
            kaug_ref[e] = jnp.where(own[e], k_ref[...], kb_ref[...])

    q = [jnp.where(own[e], q_ref[...], qb_ref[...]) for e in range(2)]
    m_ref[...] = jnp.full_like(m_ref, NEG_BIG)
    acc_ref[...] = jnp.zeros_like(acc_ref)
    ones_rows = jnp.ones((BF16_ROWS, TK), BF16)

    def scores(j, e, s_ref, c_ref):
        start = pl.multiple_of(j * TK, TK)
        k = kaug_ref[e, pl.ds(start, TK), :]
        st = lax.dot_general(k, q[e], (((1,), (1,)), ((), ())), preferred_element_type=F32)
        s_ref[e] = st
        c_ref[e] = jnp.max(st, axis=0, keepdims=True)

    def accumulate(j, e, s_ref, c_ref, masked):
        start = pl.multiple_of(j * TK, TK)
        vt = jnp.concatenate(
            [vt_ref[e * FOX_HEAD_DIM:(e + 1) * FOX_HEAD_DIM, pl.ds(start, TK)], ones_rows], axis=0)
        st = s_ref[e]
        if masked:
            key = lax.broadcasted_iota(jnp.int32, (TK, TQ), 0) + start
            qry = lax.broadcasted_iota(jnp.int32, (TK, TQ), 1) + i * TQ
            st = jnp.where(key <= qry, st, NEG_BIG)
            cmax = jnp.max(st, axis=0, keepdims=True)
        else:
            cmax = c_ref[e]
        m_prev = m_ref[e]
        m_new = jnp.maximum(m_prev, cmax)
        alpha = jnp.exp2(m_prev - m_new)
        pt = jnp.exp2(st - m_new).astype(BF16)
        acc_ref[e] = alpha * acc_ref[e] + jnp.dot(vt, pt, preferred_element_type=F32)
        m_ref[e] = m_new

    buf_a = (sa_ref, ca_ref)
    buf_b = (sb_ref, cb_ref)

    nfull = (i * TQ) // TK
    for e in range(2):
        scores(0, e, *buf_a)

    def pair(jj, carry):
        j = 2 * jj
        for e in range(2):
            scores(j + 1, e, *buf_b)
            accumulate(j, e, *buf_a, False)
        for e in range(2):
            scores(j + 2, e, *buf_a)
            accumulate(j + 1, e, *buf_b, False)
        return carry

    lax.fori_loop(0, nfull // 2, pair, 0)

    @pl.when(nfull % 2 == 1)
    def _():
        for e in range(2):
            scores(nfull, e, *buf_b)
            accumulate(nfull - 1, e, *buf_a, False)
        for e in range(2):
            accumulate(nfull, e, *buf_b, True)

    @pl.when(nfull % 2 == 0)
    def _():
        for e in range(2):
            accumulate(nfull, e, *buf_a, True)

    outs = []
    for e in range(2):
        acc = acc_ref[e]
        outs.append(acc[:FOX_HEAD_DIM] / acc[FOX_HEAD_DIM:FOX_HEAD_DIM + 1])
    o_ref[...] = jnp.concatenate(outs, axis=0).T.astype(o_ref.dtype)


def _fox_attention(q, qb, k, kb, vt):
    s = q.shape[0]
    assert TK % TQ == 0 and s % TK == 0
    blk = pl.BlockSpec((TQ, LANES), lambda p, i: (i, p))
    res = pl.BlockSpec((s, LANES), lambda p, i: (0, p), pipeline_mode=pl.Buffered(1))
    return pl.pallas_call(
        _fox_kernel,
        grid=(FOX_PAIRS, s // TQ),
        in_specs=[blk, blk, res, res,
                  pl.BlockSpec((LANES, s), lambda p, i: (p, 0), pipeline_mode=pl.Buffered(1))],
        out_specs=blk,
        out_shape=jax.ShapeDtypeStruct((s, FOX_WIDTH), BF16),
        scratch_shapes=[pltpu.VMEM((2, s, LANES), BF16),
                        pltpu.VMEM((2, 1, TQ), F32), pltpu.VMEM((2, PV_ROWS, TQ), F32),
                        pltpu.VMEM((2, TK, TQ), F32), pltpu.VMEM((2, 1, TQ), F32),
                        pltpu.VMEM((2, TK, TQ), F32), pltpu.VMEM((2, 1, TQ), F32)],
        compiler_params=pltpu.CompilerParams(dimension_semantics=("arbitrary", "arbitrary"),
                                             vmem_limit_bytes=VMEM_LIMIT),
        name="fox_attention",
    )(q, qb, k, kb, vt)


def _ret_kernel(q_ref, k_ref, v_ref, g_ref, inner_ref, xi_ref, zeta_ref, gc_ref, o_ref, r_ref):
    @pl.when(pl.program_id(0) == 0)
    def _():
        r_ref[...] = jnp.zeros_like(r_ref)

    c = RET_CHUNK
    for hh in range(RET_HEADS):
        cols = slice(hh * RET_HEAD_DIM, (hh + 1) * RET_HEAD_DIM)
        inner = inner_ref[hh]
        xi = xi_ref[hh]
        zeta = zeta_ref[hh]
        gc = gc_ref[hh]
        r_state = r_ref[hh]
        for ci in range(q_ref.shape[0] // c):
            rows = slice(ci * c, (ci + 1) * c)
            q = q_ref[rows, cols]
            k = k_ref[rows, cols]
            v = v_ref[rows, cols]
            s = lax.dot_general(q, k, (((1,), (1,)), ((), ())), preferred_element_type=F32) * inner
            o = (jnp.dot(s.astype(BF16), v, preferred_element_type=F32)
                 + jnp.dot(q, r_state.astype(BF16), preferred_element_type=F32) * xi)
            kz = (k.astype(F32) * zeta).astype(BF16)
            r_state = r_state * gc + lax.dot_general(kz, v, (((0,), (0,)), ((), ())),
                                                     preferred_element_type=F32)
            mu = jnp.mean(o, axis=-1, keepdims=True)
            d = o - mu
            var = jnp.mean(d * d, axis=-1, keepdims=True)
            on = d * lax.rsqrt(var + GN_EPS)
            g = g_ref[rows, cols].astype(F32)
            o_ref[rows, cols] = (g * jax.nn.sigmoid(g) * on).astype(o_ref.dtype)
        r_ref[hh] = r_state


def _retention(rq, rk, rv, rg, inner, xi_b, zeta_b, gc_b):
    s = rq.shape[0]
    blk = pl.BlockSpec((TR, RET_WIDTH), lambda i: (i, 0))
    tab = _const_spec((RET_HEADS, RET_CHUNK, RET_CHUNK))
    return pl.pallas_call(
        _ret_kernel,
        grid=(s // TR,),
        in_specs=[blk, blk, blk, blk, tab, tab, tab, _const_spec((RET_HEADS, 1, RET_HEAD_DIM))],
        out_specs=blk,
        out_shape=jax.ShapeDtypeStruct((s, RET_WIDTH), BF16),
        scratch_shapes=[pltpu.VMEM((RET_HEADS, RET_HEAD_DIM, RET_HEAD_DIM), F32)],
        compiler_params=pltpu.CompilerParams(dimension_semantics=("arbitrary",)),
        name="retention",
    )(rq, rk, rv, rg, inner, xi_b, zeta_b, gc_b)


def _mix_ffn_kernel(fox_ref, ret_ref, x_ref, wf_ref, wr_ref, g1_ref, l1g_ref, l1b_ref,
                    sc_ref, sh_ref, g2_ref, wu_ref, cw_ref, cb_ref, wd_ref, l2g_ref, l2b_ref,
                    o_ref, carry_ref, slab_ref, y_ref):
    tm = x_ref.shape[0]
    grp = tm // SUBLANES
    n_slab = D_MODEL // LANES

    @pl.when(pl.program_id(0) == 0)
    def _():
        carry_ref[...] = jnp.zeros_like(carry_ref)

    mix = (jnp.dot(fox_ref[...], wf_ref[...], preferred_element_type=F32)
           + jnp.dot(ret_ref[...], wr_ref[...], preferred_element_type=F32))
    x1_nat = _layer_norm(ALPHA * x_ref[...] + g1_ref[...] * mix, l1g_ref[...], l1b_ref[...])

    for c in range(n_slab):
        slab_ref[c] = x1_nat[:, c * LANES:(c + 1) * LANES]
    x1 = jnp.concatenate(
        [jnp.concatenate([slab_ref[c, pl.ds(v, SUBLANES, stride=grp), :] for v in range(grp)], axis=0)
         for c in range(n_slab)], axis=1)

    h = (x1 * (1.0 + sc_ref[...]) + sh_ref[...]).astype(BF16)
    first_sublane = lax.broadcasted_iota(jnp.int32, (SUBLANES, FF_CHUNK), 0) == 0

    def conv_up(c0):
        cols = slice(c0, c0 + FF_CHUNK)
        up = jnp.dot(h, wu_ref[:, cols], preferred_element_type=F32)
        prev = carry_ref[:, cols]
        tail = []
        for g in range(2):
            rows = slice(tm - (2 - g) * SUBLANES, tm - (1 - g) * SUBLANES)
            tail.append(jnp.where(first_sublane,
                                  pltpu.roll(prev[g * SUBLANES:(g + 1) * SUBLANES], 1, 0),
                                  pltpu.roll(up[rows], 1, 0)))
        carry_ref[:, cols] = up[tm - 2 * SUBLANES:tm]
        back1 = jnp.concatenate([tail[1], up[:tm - SUBLANES]], axis=0)
        back2 = jnp.concatenate([tail[0], tail[1], up[:tm - 2 * SUBLANES]], axis=0)
        cw = cw_ref[:, cols] * RSQRT2
        return cb_ref[:, cols] * RSQRT2 + back2 * cw[0:1] + back1 * cw[1:2] + up * cw[2:3]

    for ci in range(D_FF // FF_CHUNK):
        a = conv_up(ci * FF_CHUNK)
        b = conv_up(D_FF + ci * FF_CHUNK)
        y_ref[:, ci * FF_CHUNK:(ci + 1) * FF_CHUNK] = (a * (1.0 + lax.erf(a)) * b).astype(BF16)

    ffn = jnp.dot(y_ref[...], wd_ref[...], preferred_element_type=F32)
    out = _layer_norm(ALPHA * x1 + g2_ref[...] * ffn, l2g_ref[...], l2b_ref[...])

    for c in range(n_slab):
        for v in range(grp):
            slab_ref[c, pl.ds(v, SUBLANES, stride=grp), :] = out[v * SUBLANES:(v + 1) * SUBLANES,
                                                                 c * LANES:(c + 1) * LANES]
    for c in range(n_slab):
        o_ref[:, c * LANES:(c + 1) * LANES] = slab_ref[c]


def _mix_ffn(fox, ret, x, w_fox, w_ret, g1, ln1_g, ln1_b, sc2, sh2, g2, w_up, conv_w, conv_b, w_down,
             ln2_g, ln2_b):
    s, d = x.shape
    tm = TM_FFN
    row = lambda w: pl.BlockSpec((tm, w), lambda i: (i, 0))
    vec = _const_spec((1, d))
    return pl.pallas_call(
        _mix_ffn_kernel,
        grid=(s // tm,),
        in_specs=[row(FOX_WIDTH), row(RET_WIDTH), row(d),
                  _const_spec((FOX_WIDTH, d)), _const_spec((RET_WIDTH, d)), vec, vec, vec,
                  vec, vec, vec,
                  _const_spec((d, 2 * D_FF)), _const_spec((CONV_WIDTH, 2 * D_FF)),
                  _const_spec((1, 2 * D_FF)), _const_spec((D_FF, d)), vec, vec],
        out_specs=row(d),
        out_shape=jax.ShapeDtypeStruct((s, d), F32),
        scratch_shapes=[pltpu.VMEM((2 * SUBLANES, 2 * D_FF), F32),
                        pltpu.VMEM((d // LANES, tm, LANES), F32),
                        pltpu.VMEM((tm, D_FF), BF16)],
        compiler_params=pltpu.CompilerParams(dimension_semantics=("arbitrary",),
                                             vmem_limit_bytes=VMEM_LIMIT),
        name="mix_ffn",
    )(fox, ret, x, w_fox, w_ret, g1, ln1_g, ln1_b, sc2, sh2, g2, w_up, conv_w, conv_b, w_down,
      ln2_g, ln2_b)


def _placement_tables():
    e = np.zeros((LANES, 2 * FOX_WIDTH), np.float32)
    ones_q = np.zeros((1, FOX_WIDTH), np.float32)
    ones_k = np.zeros((1, FOX_WIDTH), np.float32)
    for h in range(FOX_HEADS):
        base = (h // 2) * LANES + (FOX_HEAD_DIM if h % 2 == 0 else 0)
        for j in range(N_PIECES):
            e[j * FOX_HEADS + h, base + BIAS_CUM + j] = 1.0
            ones_q[0, base + BIAS_ONE + j] = 1.0
            ones_k[0, base + BIAS_CUM + j] = 1.0
            e[j * FOX_HEADS + h, FOX_WIDTH + base + BIAS_ONE + j] = -1.0
    return jnp.asarray(e, BF16), jnp.asarray(ones_q), jnp.asarray(ones_k)


def _rotation_tables(s):
    dk = RET_HEAD_DIM
    pos = jnp.arange(s, dtype=F32)
    inv_freq = ROPE_BASE ** (-jnp.arange(0, dk, 2, dtype=F32) / dk)
    ang = pos[:, None] * inv_freq[None, :]
    ang = jnp.concatenate([ang, ang], axis=-1)
    sign = jnp.concatenate([-jnp.ones((dk // 2,), F32), jnp.ones((dk // 2,), F32)])
    return jnp.cos(ang), jnp.sin(ang) * sign[None, :]


def _decay_tables():
    c = RET_CHUNK
    log_gamma = jnp.log1p(-jnp.exp2(-5.0 - jnp.arange(RET_HEADS, dtype=F32)))
    idx = jnp.arange(c, dtype=F32)
    diff = idx[:, None] - idx[None, :]
    inner = jnp.where(diff[None] >= 0,
                      jnp.exp(jnp.maximum(diff, 0.0)[None] * log_gamma[:, None, None]), 0.0)
    xi = jnp.exp((idx[None, :] + 1.0) * log_gamma[:, None])
    zeta = jnp.exp((c - 1.0 - idx[None, :]) * log_gamma[:, None])
    g_chunk = jnp.exp(c * log_gamma)
    bshape = (RET_HEADS, c, RET_HEAD_DIM)
    return (inner,
            jnp.broadcast_to(xi[:, :, None], bshape),
            jnp.broadcast_to(zeta[:, :, None], bshape),
            jnp.broadcast_to(g_chunk[:, None, None], (RET_HEADS, 1, RET_HEAD_DIM)))


def kernel(x, c, w_ada, b_ada, w_in, b_f, w_out, ln1_g, ln1_b, w_up, conv_w, conv_b, w_down, ln2_g, ln2_b):
    b, s, d = x.shape
    assert (b, s, d) == (1, SEQ, D_MODEL) and w_ada.shape[0] == DEPTH
    xs = x[0]

    wi = w_in[0]
    o = 0
    parts = {}
    for name, width in (("fq", FOX_WIDTH), ("fk", FOX_WIDTH), ("fv", FOX_WIDTH), ("ff", FOX_HEADS),
                        ("rq", RET_WIDTH), ("rk", RET_WIDTH), ("rv", RET_WIDTH), ("rg", RET_WIDTH)):
        parts[name] = wi[:, o:o + width]
        o += width
    w_all = jnp.concatenate(
        [parts["fq"], parts["fk"], jnp.pad(parts["ff"], ((0, 0), (0, LANES - FOX_HEADS))),
         parts["rq"], parts["rk"], parts["rv"], parts["rg"]], axis=1).astype(BF16)
    wvt = parts["fv"].T.astype(BF16)
    bf_row = jnp.pad(b_f[0][None, :], ((0, 0), (0, LANES - FOX_HEADS)))
    wo = w_out[0].astype(BF16)

    e_mat, ones_q, ones_k = _placement_tables()
    cos, sin_s = _rotation_tables(s)
    inner, xi_b, zeta_b, gc_b = _decay_tables()

    mod = _adaln_mod(c.reshape(d, 1), w_ada[0], b_ada[0][None, :])
    sh1, sc1, g1, sh2, sc2, g2 = [mod[:, i * d:(i + 1) * d] for i in range(6)]

    q, qb, k, kb, vt, rq, rk, rv, rg = _in_proj(
        xs, sc1, sh1, w_all, wvt, bf_row, cos, sin_s, e_mat, ones_q, ones_k)
    fox = _fox_attention(q, qb, k, kb, vt)
    ret = _retention(rq, rk, rv, rg, inner, xi_b, zeta_b, gc_b)
    out = _mix_ffn(fox, ret, xs, wo[:FOX_WIDTH], wo[FOX_WIDTH:], g1, ln1_g[0][None, :], ln1_b[0][None, :],
                   sc2, sh2, g2, w_up[0].astype(BF16), conv_w[0], conv_b[0][None, :],
                   w_down[0].astype(BF16), ln2_g[0][None, :], ln2_b[0][None, :])
    return out[None]
```

```python
import math

import jax
import jax.numpy as jnp
import numpy as np
from jax import lax
from jax.experimental import pallas as pl
from jax.experimental.pallas import tpu as pltpu

F32 = jnp.float32
BF16 = jnp.bfloat16

D_MODEL = 1024
SEQ = 16384
FOX_HEADS = 8
FOX_HEAD_DIM = 64
FOX_WIDTH = FOX_HEADS * FOX_HEAD_DIM
RET_HEADS = 4
RET_HEAD_DIM = 128
RET_WIDTH = RET_HEADS * RET_HEAD_DIM
D_FF = 2816
CONV_WIDTH = 3
RET_CHUNK = 128
ROPE_BASE = 10000.0
LN_EPS = 1e-5
GN_EPS = 1e-6
DEPTH = 1
ALPHA = (2 * DEPTH) ** 0.25

LANES = 128
SUBLANES = 8
BF16_ROWS = 16
VMEM_LIMIT = 56 * 1024 * 1024

FOX_PAIRS = FOX_HEADS // 2
BIAS_CUM = 0
BIAS_ONE = 3
N_PIECES = 3
PV_ROWS = FOX_HEAD_DIM + BF16_ROWS
NEG_BIG = -1e30
LOG2_E = math.log2(math.e)
RSQRT2 = 2.0 ** -0.5

COL_Q = 0
COL_K = COL_Q + FOX_WIDTH
COL_F = COL_K + FOX_WIDTH
COL_RQ = COL_F + LANES
COL_RV = COL_RQ + RET_WIDTH
COL_RG = COL_RV + RET_WIDTH
N_COLS = COL_RG + RET_WIDTH
ROW_VT = 0
ROW_RKT = ROW_VT + FOX_WIDTH
N_ROWS_T = ROW_RKT + RET_WIDTH

TM_PROJ = 512
TQ = 512
TK = 1024
TR = 512
TM_FFN = 512
FF_CHUNK = 256


def _const_spec(shape):
    return pl.BlockSpec(shape, lambda *_: (0,) * len(shape), pipeline_mode=pl.Buffered(1))


def _split3(x):
    p0 = x.astype(BF16).astype(F32)
    r1 = x - p0
    p1 = r1.astype(BF16).astype(F32)
    p2 = (r1 - p1).astype(BF16).astype(F32)
    return p0, p1, p2


def _layer_norm(y, g, b):
    mu = jnp.mean(y, axis=-1, keepdims=True)
    d = y - mu
    var = jnp.mean(d * d, axis=-1, keepdims=True)
    return d * lax.rsqrt(var + LN_EPS) * g + b


def _mod_kernel(c_ref, w_ref, b_ref, o_ref):
    c = c_ref[...]
    sc = c * jax.nn.sigmoid(c)
    o_ref[...] = jnp.sum(w_ref[...] * sc, axis=0, keepdims=True) + b_ref[...]


def _adaln_mod(c_col, w_ada, b_ada):
    d, n = w_ada.shape
    tn = 512
    return pl.pallas_call(
        _mod_kernel,
        grid=(n // tn,),
        in_specs=[pl.BlockSpec((d, 1), lambda j: (0, 0)),
                  pl.BlockSpec((d, tn), lambda j: (0, j)),
                  pl.BlockSpec((1, tn), lambda j: (0, j))],
        out_specs=pl.BlockSpec((1, tn), lambda j: (0, j)),
        out_shape=jax.ShapeDtypeStruct((1, n), F32),
        compiler_params=pltpu.CompilerParams(dimension_semantics=("arbitrary",)),
        name="adaln_mod",
    )(c_col, w_ada, b_ada)


def _in_proj_kernel(x_ref, sc_ref, sh_ref, w_ref, wt_ref, bf_ref, cos_ref, sin_ref, cost_ref, sint_ref,
                    e_ref, oq_ref, ok_ref,
                    q_out, qb_out, k_out, kb_out, vt_out, rq_out, rkt_out, rv_out, rg_out,
                    carry_ref):
    tm = x_ref.shape[0]

    @pl.when(pl.program_id(0) == 0)
    def _():
        carry_ref[...] = jnp.zeros_like(carry_ref)

    h = (x_ref[...] * (1.0 + sc_ref[...]) + sh_ref[...]).astype(BF16)

    def proj(c0, width):
        return jnp.dot(h, w_ref[:, c0:c0 + width], preferred_element_type=F32)

    lane = lax.broadcasted_iota(jnp.int32, (tm, LANES), 1)
    head_lane = lane < FOX_HEADS
    ff = proj(COL_F, LANES) + bf_ref[...]
    logf = jnp.minimum(ff, 0.0) - jnp.log1p(jnp.exp(-jnp.abs(ff)))
    logf = jnp.where(head_lane, logf, 0.0)
    p0, p1, p2 = _split3(logf)
    pieces = p0 + pltpu.roll(p1, FOX_HEADS, 1) + pltpu.roll(p2, 2 * FOX_HEADS, 1)
    row = lax.broadcasted_iota(jnp.int32, (tm, tm), 0)
    col = lax.broadcasted_iota(jnp.int32, (tm, tm), 1)
    tril = jnp.where(row >= col, 1.0, 0.0).astype(BF16)
    c3 = jnp.dot(tril, pieces.astype(BF16), preferred_element_type=F32)
    cum = c3 + pltpu.roll(c3, LANES - FOX_HEADS, 1) + pltpu.roll(c3, LANES - 2 * FOX_HEADS, 1)
    cum = jnp.where(head_lane, cum, 0.0) + carry_ref[0:1, :]
    carry_ref[0:1, :] = cum[tm - 1:tm, :]
    c0_, c1_, c2_ = _split3(cum * LOG2_E)
    cum_pieces = (c0_ + pltpu.roll(c1_, FOX_HEADS, 1) + pltpu.roll(c2_, 2 * FOX_HEADS, 1)).astype(BF16)
    bias = jnp.dot(cum_pieces, e_ref[...], preferred_element_type=F32)
    qb_out[...] = (bias[:, :FOX_WIDTH] + oq_ref[...]).astype(BF16)
    kb_out[...] = (bias[:, FOX_WIDTH:] + ok_ref[...]).astype(BF16)

    q_out[...] = (proj(COL_Q, FOX_WIDTH) * (FOX_HEAD_DIM ** -0.5 * LOG2_E)).astype(BF16)
    k_out[...] = proj(COL_K, FOX_WIDTH).astype(BF16)
    tp = lax.dot_general(wt_ref[...], h, (((1,), (1,)), ((), ())), preferred_element_type=F32)
    vt_out[...] = tp[ROW_VT:ROW_VT + FOX_WIDTH].astype(BF16)

    half = RET_HEAD_DIM // 2
    cos = jnp.concatenate([cos_ref[...]] * 2, axis=1)
    sin = jnp.concatenate([-sin_ref[...], sin_ref[...]], axis=1)
    cost = jnp.concatenate([cost_ref[...]] * 2, axis=0)
    sint = jnp.concatenate([-sint_ref[...], sint_ref[...]], axis=0)
    kscale = RET_HEAD_DIM ** -0.5
    two = 2 * RET_HEAD_DIM
    for pair in range(RET_HEADS // 2):
        a2 = proj(COL_RQ + pair * two, two)
        for e in range(2):
            lo = pair * two + e * RET_HEAD_DIM
            a = a2[:, e * RET_HEAD_DIM:(e + 1) * RET_HEAD_DIM]
            rq_out[:, lo:lo + RET_HEAD_DIM] = (a * cos + pltpu.roll(a, half, 1) * sin).astype(BF16)
    for hh in range(RET_HEADS):
        lo = hh * RET_HEAD_DIM
        b = tp[ROW_RKT + lo:ROW_RKT + lo + RET_HEAD_DIM]
        b_rot = jnp.concatenate([b[half:], b[:half]], axis=0)
        rkt_out[lo:lo + RET_HEAD_DIM, :] = ((b * cost + b_rot * sint) * kscale).astype(BF16)
    rv_out[...] = proj(COL_RV, RET_WIDTH).astype(BF16)
    rg_out[...] = proj(COL_RG, RET_WIDTH).astype(BF16)


def _in_proj(x, sc1, sh1, w_all, w_t, bf_row, cos, sin, cos_t, sin_t, e_mat, ones_q, ones_k):
    s, d = x.shape
    tm = TM_PROJ
    half = RET_HEAD_DIM // 2
    row = lambda w: pl.BlockSpec((tm, w), lambda i: (i, 0))
    col = lambda h: pl.BlockSpec((h, tm), lambda i: (0, i))
    out_shapes = ([jax.ShapeDtypeStruct((s, FOX_WIDTH), BF16)] * 4
                  + [jax.ShapeDtypeStruct((FOX_WIDTH, s), BF16),
                     jax.ShapeDtypeStruct((s, RET_WIDTH), BF16),
                     jax.ShapeDtypeStruct((RET_WIDTH, s), BF16)]
                  + [jax.ShapeDtypeStruct((s, RET_WIDTH), BF16)] * 2)
    return pl.pallas_call(
        _in_proj_kernel,
        grid=(s // tm,),
        in_specs=[row(d), _const_spec((1, d)), _const_spec((1, d)),
                  _const_spec((d, N_COLS)), _const_spec((N_ROWS_T, d)), _const_spec((1, LANES)),
                  row(half), row(half), col(half), col(half),
                  _const_spec((LANES, 2 * FOX_WIDTH)),
                  _const_spec((1, FOX_WIDTH)), _const_spec((1, FOX_WIDTH))],
        out_specs=([row(FOX_WIDTH)] * 4
                   + [col(FOX_WIDTH), row(RET_WIDTH), col(RET_WIDTH), row(RET_WIDTH), row(RET_WIDTH)]),
        out_shape=out_shapes,
        scratch_shapes=[pltpu.VMEM((SUBLANES, LANES), F32)],
        compiler_params=pltpu.CompilerParams(dimension_semantics=("arbitrary",),
                                             vmem_limit_bytes=VMEM_LIMIT),
        name="in_proj",
    )(x, sc1, sh1, w_all, w_t, bf_row, cos, sin, cos_t, sin_t, e_mat, ones_q, ones_k)


def _fox_kernel(q_ref, qb_ref, k_ref, kb_ref, vt_ref, o_ref, kaug_ref, m_ref, acc_ref,
                sa_ref, ca_ref, sb_ref, cb_ref):
    i = pl.program_id(1)
    lane = lax.broadcasted_iota(jnp.int32, (1, LANES), 1)
    own = (lane < FOX_HEAD_DIM, lane >= FOX_HEAD_DIM)

    @pl.when(i == 0)
    def _():
        for e in range(2):
            kaug_ref[e] = jnp.where(own[e], k_ref[...], kb_ref[...])

    q = [jnp.where(own[e], q_ref[...], qb_ref[...]) for e in range(2)]
    m_ref[...] = jnp.full_like(m_ref, NEG_BIG)
    acc_ref[...] = jnp.zeros_like(acc_ref)
    ones_rows = jnp.ones((BF16_ROWS, TK), BF16)

    def scores(j, e, s_ref, c_ref):
        start = pl.multiple_of(j * TK, TK)
        k = kaug_ref[e, pl.ds(start, TK), :]
        st = lax.dot_general(k, q[e], (((1,), (1,)), ((), ())), preferred_element_type=F32)
        s_ref[e] = st
        c_ref[e] = jnp.max(st, axis=0, keepdims=True)

    def accumulate(j, e, s_ref, c_ref, masked):
        start = pl.multiple_of(j * TK, TK)
        vt = jnp.concatenate(
            [vt_ref[e * FOX_HEAD_DIM:(e + 1) * FOX_HEAD_DIM, pl.ds(start, TK)], ones_rows], axis=0)
        st = s_ref[e]
        if masked:
            key = lax.broadcasted_iota(jnp.int32, (TK, TQ), 0) + start
            qry = lax.broadcasted_iota(jnp.int32, (TK, TQ), 1) + i * TQ
            st = jnp.where(key <= qry, st, NEG_BIG)
            cmax = jnp.max(st, axis=0, keepdims=True)
        else:
            cmax = c_ref[e]
        m_prev = m_ref[e]
        m_new = jnp.maximum(m_prev, cmax)
        alpha = jnp.exp2(m_prev - m_new)
        pt = jnp.exp2(st - m_new).astype(BF16)
        acc_ref[e] = alpha * acc_ref[e] + jnp.dot(vt, pt, preferred_element_type=F32)
        m_ref[e] = m_new

    buf_a = (sa_ref, ca_ref)
    buf_b = (sb_ref, cb_ref)

    nfull = (i * TQ) // TK
    for e in range(2):
        scores(0, e, *buf_a)

    def pair(jj, carry):
        j = 2 * jj
        for e in range(2):
            scores(j + 1, e, *buf_b)
            accumulate(j, e, *buf_a, False)
        for e in range(2):
            scores(j + 2, e, *buf_a)
            accumulate(j + 1, e, *buf_b, False)
        return carry

    lax.fori_loop(0, nfull // 2, pair, 0)

    @pl.when(nfull % 2 == 1)
    def _():
        for e in range(2):
            scores(nfull, e, *buf_b)
            accumulate(nfull - 1, e, *buf_a, False)
        for e in range(2):
            accumulate(nfull, e, *buf_b, True)

    @pl.when(nfull % 2 == 0)
    def _():
        for e in range(2):
            accumulate(nfull, e, *buf_a, True)

    outs = []
    for e in range(2):
        acc = acc_ref[e]
        outs.append(acc[:FOX_HEAD_DIM] / acc[FOX_HEAD_DIM:FOX_HEAD_DIM + 1])
    o_ref[...] = jnp.concatenate(outs, axis=0).T.astype(o_ref.dtype)


def _fox_attention(q, qb, k, kb, vt):
    s = q.shape[0]
    assert TK % TQ == 0 and s % TK == 0
    blk = pl.BlockSpec((TQ, LANES), lambda p, i: (i, p))
    res = pl.BlockSpec((s, LANES), lambda p, i: (0, p), pipeline_mode=pl.Buffered(1))
    return pl.pallas_call(
        _fox_kernel,
        grid=(FOX_PAIRS, s // TQ),
        in_specs=[blk, blk, res, res,
                  pl.BlockSpec((LANES, s), lambda p, i: (p, 0), pipeline_mode=pl.Buffered(1))],
        out_specs=blk,
        out_shape=jax.ShapeDtypeStruct((s, FOX_WIDTH), BF16),
        scratch_shapes=[pltpu.VMEM((2, s, LANES), BF16),
                        pltpu.VMEM((2, 1, TQ), F32), pltpu.VMEM((2, PV_ROWS, TQ), F32),
                        pltpu.VMEM((2, TK, TQ), F32), pltpu.VMEM((2, 1, TQ), F32),
                        pltpu.VMEM((2, TK, TQ), F32), pltpu.VMEM((2, 1, TQ), F32)],
        compiler_params=pltpu.CompilerParams(dimension_semantics=("arbitrary", "arbitrary"),
                                             vmem_limit_bytes=VMEM_LIMIT),
        name="fox_attention",
    )(q, qb, k, kb, vt)


def _ret_kernel(q_ref, kt_ref, v_ref, g_ref, inner_ref, xi_ref, zeta_ref, gc_ref, o_ref, r_ref):
    @pl.when(pl.program_id(0) == 0)
    def _():
        r_ref[...] = jnp.zeros_like(r_ref)

    c = RET_CHUNK
    n_chunks = q_ref.shape[0] // c
    tiles = [(hh, ci) for hh in range(RET_HEADS) for ci in range(n_chunks)]

    def head_cols(hh):
        return slice(hh * RET_HEAD_DIM, (hh + 1) * RET_HEAD_DIM)

    def chunk_rows(ci):
        return slice(ci * c, (ci + 1) * c)

    intra, kv = {}, {}
    for hh, ci in tiles:
        q = q_ref[chunk_rows(ci), head_cols(hh)]
        kt = kt_ref[head_cols(hh), chunk_rows(ci)]
        v = v_ref[chunk_rows(ci), head_cols(hh)]
        intra[hh, ci] = (jnp.dot(q, kt, preferred_element_type=F32) * inner_ref[hh]).astype(BF16)
        kzt = (kt.astype(F32) * zeta_ref[hh]).astype(BF16)
        kv[hh, ci] = jnp.dot(kzt, v, preferred_element_type=F32)

    state = {}
    for hh in range(RET_HEADS):
        r_state = r_ref[hh]
        for ci in range(n_chunks):
            state[hh, ci] = r_state.astype(BF16)
            r_state = r_state * gc_ref[hh] + kv[hh, ci]
        r_ref[hh] = r_state

    for hh, ci in tiles:
        q = q_ref[chunk_rows(ci), head_cols(hh)]
        v = v_ref[chunk_rows(ci), head_cols(hh)]
        o = (jnp.dot(intra[hh, ci], v, preferred_element_type=F32)
             + jnp.dot(q, state[hh, ci], preferred_element_type=F32) * xi_ref[hh])
        mu = jnp.mean(o, axis=-1, keepdims=True)
        d = o - mu
        var = jnp.mean(d * d, axis=-1, keepdims=True)
        on = d * lax.rsqrt(var + GN_EPS)
        g = g_ref[chunk_rows(ci), head_cols(hh)].astype(F32)
        o_ref[chunk_rows(ci), head_cols(hh)] = (g * jax.nn.sigmoid(g) * on).astype(o_ref.dtype)


def _retention(rq, rkt, rv, rg, inner, xi_b, zeta_b, gc_b):
    s = rq.shape[0]
    blk = pl.BlockSpec((TR, RET_WIDTH), lambda i: (i, 0))
    blk_t = pl.BlockSpec((RET_WIDTH, TR), lambda i: (0, i))
    tab = _const_spec((RET_HEADS, RET_CHUNK, RET_CHUNK))
    return pl.pallas_call(
        _ret_kernel,
        grid=(s // TR,),
        in_specs=[blk, blk_t, blk, blk, tab, tab, tab, _const_spec((RET_HEADS, 1, RET_HEAD_DIM))],
        out_specs=blk,
        out_shape=jax.ShapeDtypeStruct((s, RET_WIDTH), BF16),
        scratch_shapes=[pltpu.VMEM((RET_HEADS, RET_HEAD_DIM, RET_HEAD_DIM), F32)],
        compiler_params=pltpu.CompilerParams(dimension_semantics=("arbitrary",)),
        name="retention",
    )(rq, rkt, rv, rg, inner, xi_b, zeta_b, gc_b)


def _mix_ffn_kernel(fox_ref, ret_ref, x_ref, wf_ref, wr_ref, g1_ref, l1g_ref, l1b_ref,
                    sc_ref, sh_ref, g2_ref, wu_ref, cw_ref, cb_ref, wd_ref, l2g_ref, l2b_ref,
                    o_ref, carry_ref, slab_ref, y_ref):
    tm = x_ref.shape[0]
    grp = tm // SUBLANES
    n_slab = D_MODEL // LANES

    @pl.when(pl.program_id(0) == 0)
    def _():
        carry_ref[...] = jnp.zeros_like(carry_ref)

    mix = (jnp.dot(fox_ref[...], wf_ref[...], preferred_element_type=F32)
           + jnp.dot(ret_ref[...], wr_ref[...], preferred_element_type=F32))
    x1_nat = _layer_norm(ALPHA * x_ref[...] + g1_ref[...] * mix, l1g_ref[...], l1b_ref[...])

    for c in range(n_slab):
        slab_ref[c] = x1_nat[:, c * LANES:(c + 1) * LANES]
    x1 = jnp.concatenate(
        [jnp.concatenate([slab_ref[c, pl.ds(v, SUBLANES, stride=grp), :] for v in range(grp)], axis=0)
         for c in range(n_slab)], axis=1)

    h = (x1 * (1.0 + sc_ref[...]) + sh_ref[...]).astype(BF16)
    first_sublane = lax.broadcasted_iota(jnp.int32, (SUBLANES, FF_CHUNK), 0) == 0

    def conv_up(c0):
        cols = slice(c0, c0 + FF_CHUNK)
        up = jnp.dot(h, wu_ref[:, cols], preferred_element_type=F32)
        prev = carry_ref[:, cols]
        tail = []
        for g in range(2):
            rows = slice(tm - (2 - g) * SUBLANES, tm - (1 - g) * SUBLANES)
            tail.append(jnp.where(first_sublane,
                                  pltpu.roll(prev[g * SUBLANES:(g + 1) * SUBLANES], 1, 0),
                                  pltpu.roll(up[rows], 1, 0)))
        carry_ref[:, cols] = up[tm - 2 * SUBLANES:tm]
        back1 = jnp.concatenate([tail[1], up[:tm - SUBLANES]], axis=0)
        back2 = jnp.concatenate([tail[0], tail[1], up[:tm - 2 * SUBLANES]], axis=0)
        cw = cw_ref[:, cols] * RSQRT2
        return cb_ref[:, cols] * RSQRT2 + back2 * cw[0:1] + back1 * cw[1:2] + up * cw[2:3]

    for ci in range(D_FF // FF_CHUNK):
        a = conv_up(ci * FF_CHUNK)
        b = conv_up(D_FF + ci * FF_CHUNK)
        y_ref[:, ci * FF_CHUNK:(ci + 1) * FF_CHUNK] = (a * (1.0 + lax.erf(a)) * b).astype(BF16)

    ffn = jnp.dot(y_ref[...], wd_ref[...], preferred_element_type=F32)
    out = _layer_norm(ALPHA * x1 + g2_ref[...] * ffn, l2g_ref[...], l2b_ref[...])

    for c in range(n_slab):
        for v in range(grp):
            slab_ref[c, pl.ds(v, SUBLANES, stride=grp), :] = out[v * SUBLANES:(v + 1) * SUBLANES,
                                                                 c * LANES:(c + 1) * LANES]
    for c in range(n_slab):
        o_ref[:, c * LANES:(c + 1) * LANES] = slab_ref[c]


def _mix_ffn(fox, ret, x, w_fox, w_ret, g1, ln1_g, ln1_b, sc2, sh2, g2, w_up, conv_w, conv_b, w_down,
             ln2_g, ln2_b):
    s, d = x.shape
    tm = TM_FFN
    row = lambda w: pl.BlockSpec((tm, w), lambda i: (i, 0))
    vec = _const_spec((1, d))
    return pl.pallas_call(
        _mix_ffn_kernel,
        grid=(s // tm,),
        in_specs=[row(FOX_WIDTH), row(RET_WIDTH), row(d),
                  _const_spec((FOX_WIDTH, d)), _const_spec((RET_WIDTH, d)), vec, vec, vec,
                  vec, vec, vec,
                  _const_spec((d, 2 * D_FF)), _const_spec((CONV_WIDTH, 2 * D_FF)),
                  _const_spec((1, 2 * D_FF)), _const_spec((D_FF, d)), vec, vec],
        out_specs=row(d),
        out_shape=jax.ShapeDtypeStruct((s, d), F32),
        scratch_shapes=[pltpu.VMEM((2 * SUBLANES, 2 * D_FF), F32),
                        pltpu.VMEM((d // LANES, tm, LANES), F32),
                        pltpu.VMEM((tm, D_FF), BF16)],
        compiler_params=pltpu.CompilerParams(dimension_semantics=("arbitrary",),
                                             vmem_limit_bytes=VMEM_LIMIT),
        name="mix_ffn",
    )(fox, ret, x, w_fox, w_ret, g1, ln1_g, ln1_b, sc2, sh2, g2, w_up, conv_w, conv_b, w_down,
      ln2_g, ln2_b)


def _placement_tables():
    e = np.zeros((LANES, 2 * FOX_WIDTH), np.float32)
    ones_q = np.zeros((1, FOX_WIDTH), np.float32)
    ones_k = np.zeros((1, FOX_WIDTH), np.float32)
    for h in range(FOX_HEADS):
        base = (h // 2) * LANES + (FOX_HEAD_DIM if h % 2 == 0 else 0)
        for j in range(N_PIECES):
            e[j * FOX_HEADS + h, base + BIAS_CUM + j] = 1.0
            ones_q[0, base + BIAS_ONE + j] = 1.0
            ones_k[0, base + BIAS_CUM + j] = 1.0
            e[j * FOX_HEADS + h, FOX_WIDTH + base + BIAS_ONE + j] = -1.0
    return jnp.asarray(e, BF16), jnp.asarray(ones_q), jnp.asarray(ones_k)


def _rotation_tables(s):
    dk = RET_HEAD_DIM
    pos = jnp.arange(s, dtype=F32)
    inv_freq = ROPE_BASE ** (-jnp.arange(0, dk, 2, dtype=F32) / dk)
    ang = pos[:, None] * inv_freq[None, :]
    cos, sin = jnp.cos(ang), jnp.sin(ang)
    return cos, sin, cos.T, sin.T


def _decay_tables():
    c = RET_CHUNK
    log_gamma = jnp.log1p(-jnp.exp2(-5.0 - jnp.arange(RET_HEADS, dtype=F32)))
    idx = jnp.arange(c, dtype=F32)
    diff = idx[:, None] - idx[None, :]
    inner = jnp.where(diff[None] >= 0,
                      jnp.exp(jnp.maximum(diff, 0.0)[None] * log_gamma[:, None, None]), 0.0)
    xi = jnp.exp((idx[None, :] + 1.0) * log_gamma[:, None])
    zeta = jnp.exp((c - 1.0 - idx[None, :]) * log_gamma[:, None])
    g_chunk = jnp.exp(c * log_gamma)
    bshape = (RET_HEADS, c, RET_HEAD_DIM)
    return (inner,
            jnp.broadcast_to(xi[:, :, None], bshape),
            jnp.broadcast_to(zeta[:, None, :], bshape),
            jnp.broadcast_to(g_chunk[:, None, None], (RET_HEADS, 1, RET_HEAD_DIM)))


def kernel(x, c, w_ada, b_ada, w_in, b_f, w_out, ln1_g, ln1_b, w_up, conv_w, conv_b, w_down, ln2_g, ln2_b):
    b, s, d = x.shape
    assert (b, s, d) == (1, SEQ, D_MODEL) and w_ada.shape[0] == DEPTH
    xs = x[0]

    wi = w_in[0]
    o = 0
    parts = {}
    for name, width in (("fq", FOX_WIDTH), ("fk", FOX_WIDTH), ("fv", FOX_WIDTH), ("ff", FOX_HEADS),
                        ("rq", RET_WIDTH), ("rk", RET_WIDTH), ("rv", RET_WIDTH), ("rg", RET_WIDTH)):
        parts[name] = wi[:, o:o + width]
        o += width
    w_all = jnp.concatenate(
        [parts["fq"], parts["fk"], jnp.pad(parts["ff"], ((0, 0), (0, LANES - FOX_HEADS))),
         parts["rq"], parts["rv"], parts["rg"]], axis=1).astype(BF16)
    w_t = jnp.concatenate([parts["fv"], parts["rk"]], axis=1).T.astype(BF16)
    bf_row = jnp.pad(b_f[0][None, :], ((0, 0), (0, LANES - FOX_HEADS)))
    wo = w_out[0].astype(BF16)

    e_mat, ones_q, ones_k = _placement_tables()
    cos, sin, cos_t, sin_t = _rotation_tables(s)
    inner, xi_b, zeta_b, gc_b = _decay_tables()

    mod = _adaln_mod(c.reshape(d, 1), w_ada[0], b_ada[0][None, :])
    sh1, sc1, g1, sh2, sc2, g2 = [mod[:, i * d:(i + 1) * d] for i in range(6)]

    q, qb, k, kb, vt, rq, rkt, rv, rg = _in_proj(
        xs, sc1, sh1, w_all, w_t, bf_row, cos, sin, cos_t, sin_t, e_mat, ones_q, ones_k)
    fox = _fox_attention(q, qb, k, kb, vt)
    ret = _retention(rq, rkt, rv, rg, inner, xi_b, zeta_b, gc_b)
    out = _mix_ffn(fox, ret, xs, wo[:FOX_WIDTH], wo[FOX_WIDTH:], g1, ln1_g[0][None, :], ln1_b[0][None, :],
                   sc2, sh2, g2, w_up[0].astype(BF16), conv_w[0], conv_b[0][None, :],
                   w_down[0].astype(BF16), ln2_g[0][None, :], ln2_b[0][None, :])
    return out[None]
```

```python
import math

import jax
import jax.numpy as jnp
import numpy as np
from jax import lax
from jax.experimental import pallas as pl
from jax.experimental.pallas import tpu as pltpu

F32 = jnp.float32
BF16 = jnp.bfloat16

D_MODEL = 1024
SEQ = 16384
FOX_HEADS = 8
FOX_HEAD_DIM = 64
FOX_WIDTH = FOX_HEADS * FOX_HEAD_DIM
RET_HEADS = 4
RET_HEAD_DIM = 128
RET_WIDTH = RET_HEADS * RET_HEAD_DIM
D_FF = 2816
CONV_WIDTH = 3
RET_CHUNK = 128
ROPE_BASE = 10000.0
LN_EPS = 1e-5
GN_EPS = 1e-6
DEPTH = 1
ALPHA = (2 * DEPTH) ** 0.25

LANES = 128
SUBLANES = 8
BF16_ROWS = 16
VMEM_LIMIT = 56 * 1024 * 1024

FOX_PAIRS = FOX_HEADS // 2
BIAS_CUM = 0
BIAS_ONE = 3
N_PIECES = 3
PV_ROWS = FOX_HEAD_DIM + BF16_ROWS
NEG_BIG = -1e30
LOG2_E = math.log2(math.e)
RSQRT2 = 2.0 ** -0.5

COL_Q = 0
COL_K = COL_Q + FOX_WIDTH
COL_F = COL_K + FOX_WIDTH
COL_RQ = COL_F + LANES
COL_RV = COL_RQ + RET_WIDTH
COL_RG = COL_RV + RET_WIDTH
N_COLS = COL_RG + RET_WIDTH
ROW_VT = 0
ROW_RKT = ROW_VT + FOX_WIDTH
N_ROWS_T = ROW_RKT + RET_WIDTH

TM_PROJ = 512
TQ = 512
TK = 1024
TR = 512
TM_FFN = 512
FF_CHUNK = 256


def _const_spec(shape):
    return pl.BlockSpec(shape, lambda *_: (0,) * len(shape), pipeline_mode=pl.Buffered(1))


def _split3(x):
    p0 = x.astype(BF16).astype(F32)
    r1 = x - p0
    p1 = r1.astype(BF16).astype(F32)
    p2 = (r1 - p1).astype(BF16).astype(F32)
    return p0, p1, p2


def _layer_norm(y, g, b):
    mu = jnp.mean(y, axis=-1, keepdims=True)
    d = y - mu
    var = jnp.mean(d * d, axis=-1, keepdims=True)
    return d * lax.rsqrt(var + LN_EPS) * g + b


def _mod_kernel(c_ref, w_ref, b_ref, o_ref):
    c = c_ref[...]
    sc = c * jax.nn.sigmoid(c)
    o_ref[...] = jnp.sum(w_ref[...] * sc, axis=0, keepdims=True) + b_ref[...]


def _adaln_mod(c_col, w_ada, b_ada):
    d, n = w_ada.shape
    tn = 512
    return pl.pallas_call(
        _mod_kernel,
        grid=(n // tn,),
        in_specs=[pl.BlockSpec((d, 1), lambda j: (0, 0)),
                  pl.BlockSpec((d, tn), lambda j: (0, j)),
                  pl.BlockSpec((1, tn), lambda j: (0, j))],
        out_specs=pl.BlockSpec((1, tn), lambda j: (0, j)),
        out_shape=jax.ShapeDtypeStruct((1, n), F32),
        compiler_params=pltpu.CompilerParams(dimension_semantics=("arbitrary",)),
        name="adaln_mod",
    )(c_col, w_ada, b_ada)


def _in_proj_kernel(x_ref, sc_ref, sh_ref, w_ref, wt_ref, bf_ref, ra_ref, rb_ref, rat_ref, rbt_ref,
                    e_ref, oq_ref, ok_ref,
                    q_out, qb_out, k_out, kb_out, vt_out, rq_out, rkt_out, rv_out, rg_out,
                    carry_ref):
    tm = x_ref.shape[0]

    @pl.when(pl.program_id(0) == 0)
    def _():
        carry_ref[...] = jnp.zeros_like(carry_ref)

    h = (x_ref[...] * (1.0 + sc_ref[...]) + sh_ref[...]).astype(BF16)

    def proj(c0, width):
        return jnp.dot(h, w_ref[:, c0:c0 + width], preferred_element_type=F32)

    lane = lax.broadcasted_iota(jnp.int32, (tm, LANES), 1)
    head_lane = lane < FOX_HEADS
    ff = proj(COL_F, LANES) + bf_ref[...]
    logf = jnp.minimum(ff, 0.0) - jnp.log1p(jnp.exp(-jnp.abs(ff)))
    logf = jnp.where(head_lane, logf, 0.0)
    p0, p1, p2 = _split3(logf)
    pieces = p0 + pltpu.roll(p1, FOX_HEADS, 1) + pltpu.roll(p2, 2 * FOX_HEADS, 1)
    row = lax.broadcasted_iota(jnp.int32, (tm, tm), 0)
    col = lax.broadcasted_iota(jnp.int32, (tm, tm), 1)
    tril = jnp.where(row >= col, 1.0, 0.0).astype(BF16)
    c3 = jnp.dot(tril, pieces.astype(BF16), preferred_element_type=F32)
    cum = c3 + pltpu.roll(c3, LANES - FOX_HEADS, 1) + pltpu.roll(c3, LANES - 2 * FOX_HEADS, 1)
    cum = jnp.where(head_lane, cum, 0.0) + carry_ref[0:1, :]
    carry_ref[0:1, :] = cum[tm - 1:tm, :]
    c0_, c1_, c2_ = _split3(cum * LOG2_E)
    cum_pieces = (c0_ + pltpu.roll(c1_, FOX_HEADS, 1) + pltpu.roll(c2_, 2 * FOX_HEADS, 1)).astype(BF16)
    bias = jnp.dot(cum_pieces, e_ref[...], preferred_element_type=F32)
    qb_out[...] = (bias[:, :FOX_WIDTH] + oq_ref[...]).astype(BF16)
    kb_out[...] = (bias[:, FOX_WIDTH:] + ok_ref[...]).astype(BF16)

    q_out[...] = (proj(COL_Q, FOX_WIDTH) * (FOX_HEAD_DIM ** -0.5 * LOG2_E)).astype(BF16)
    k_out[...] = proj(COL_K, FOX_WIDTH).astype(BF16)
    tp = lax.dot_general(wt_ref[...], h, (((1,), (1,)), ((), ())), preferred_element_type=F32)
    vt_out[...] = tp[ROW_VT:ROW_VT + FOX_WIDTH].astype(BF16)

    half = RET_HEAD_DIM // 2
    ca, sa = ra_ref[0, 0:1, :], ra_ref[0, 1:2, :]
    cos = ca * rb_ref[0] - sa * rb_ref[1]
    sin = sa * rb_ref[2] + ca * rb_ref[3]
    cat, sat = rat_ref[0, :, 0:1], rat_ref[0, :, 1:2]
    cost = cat * rbt_ref[0] - sat * rbt_ref[1]
    sint = sat * rbt_ref[2] + cat * rbt_ref[3]
    kscale = RET_HEAD_DIM ** -0.5
    two = 2 * RET_HEAD_DIM
    for pair in range(RET_HEADS // 2):
        a2 = proj(COL_RQ + pair * two, two)
        for e in range(2):
            lo = pair * two + e * RET_HEAD_DIM
            a = a2[:, e * RET_HEAD_DIM:(e + 1) * RET_HEAD_DIM]
            rq_out[:, lo:lo + RET_HEAD_DIM] = (a * cos + pltpu.roll(a, half, 1) * sin).astype(BF16)
    for hh in range(RET_HEADS):
        lo = hh * RET_HEAD_DIM
        b = tp[ROW_RKT + lo:ROW_RKT + lo + RET_HEAD_DIM]
        b_rot = jnp.concatenate([b[half:], b[:half]], axis=0)
        rkt_out[lo:lo + RET_HEAD_DIM, :] = ((b * cost + b_rot * sint) * kscale).astype(BF16)
    rv_out[...] = proj(COL_RV, RET_WIDTH).astype(BF16)
    rg_out[...] = proj(COL_RG, RET_WIDTH).astype(BF16)


def _in_proj(x, sc1, sh1, w_all, w_t, bf_row, rot_a, rot_b, rot_at, rot_bt, e_mat, ones_q, ones_k):
    s, d = x.shape
    tm = TM_PROJ
    row = lambda w: pl.BlockSpec((tm, w), lambda i: (i, 0))
    col = lambda h: pl.BlockSpec((h, tm), lambda i: (0, i))
    out_shapes = ([jax.ShapeDtypeStruct((s, FOX_WIDTH), BF16)] * 4
                  + [jax.ShapeDtypeStruct((FOX_WIDTH, s), BF16),
                     jax.ShapeDtypeStruct((s, RET_WIDTH), BF16),
                     jax.ShapeDtypeStruct((RET_WIDTH, s), BF16)]
                  + [jax.ShapeDtypeStruct((s, RET_WIDTH), BF16)] * 2)
    return pl.pallas_call(
        _in_proj_kernel,
        grid=(s // tm,),
        in_specs=[row(d), _const_spec((1, d)), _const_spec((1, d)),
                  _const_spec((d, N_COLS)), _const_spec((N_ROWS_T, d)), _const_spec((1, LANES)),
                  pl.BlockSpec((1, 2, RET_HEAD_DIM), lambda i: (i, 0, 0)),
                  _const_spec((4, tm, RET_HEAD_DIM)),
                  pl.BlockSpec((1, RET_HEAD_DIM, 2), lambda i: (i, 0, 0)),
                  _const_spec((4, RET_HEAD_DIM, tm)),
                  _const_spec((LANES, 2 * FOX_WIDTH)),
                  _const_spec((1, FOX_WIDTH)), _const_spec((1, FOX_WIDTH))],
        out_specs=([row(FOX_WIDTH)] * 4
                   + [col(FOX_WIDTH), row(RET_WIDTH), col(RET_WIDTH), row(RET_WIDTH), row(RET_WIDTH)]),
        out_shape=out_shapes,
        scratch_shapes=[pltpu.VMEM((SUBLANES, LANES), F32)],
        compiler_params=pltpu.CompilerParams(dimension_semantics=("arbitrary",),
                                             vmem_limit_bytes=VMEM_LIMIT),
        name="in_proj",
    )(x, sc1, sh1, w_all, w_t, bf_row, rot_a, rot_b, rot_at, rot_bt, e_mat, ones_q, ones_k)


def _fox_kernel(q_ref, qb_ref, k_ref, kb_ref, vt_ref, o_ref, kaug_ref, m_ref, acc_ref,
                sa_ref, ca_ref, sb_ref, cb_ref):
    i = pl.program_id(1)
    lane = lax.broadcasted_iota(jnp.int32, (1, LANES), 1)
    own = (lane < FOX_HEAD_DIM, lane >= FOX_HEAD_DIM)

    @pl.when(i == 0)
    def _():
        for e in range(2):
            kaug_ref[e] = jnp.where(own[e], k_ref[...], kb_ref[...])

    q = [jnp.where(own[e], q_ref[...], qb_ref[...]) for e in range(2)]
    m_ref[...] = jnp.full_like(m_ref, NEG_BIG)
    acc_ref[...] = jnp.zeros_like(acc_ref)

    def scores(j, e, s_ref, c_ref):
        start = pl.multiple_of(j * TK, TK)
        k = kaug_ref[e, pl.ds(start, TK), :]
        st = lax.dot_general(k, q[e], (((1,), (1,)), ((), ())), preferred_element_type=F32)
        s_ref[e] = st
        c_ref[e] = jnp.max(st, axis=0, keepdims=True)

    def update(e, st, cmax, vt):
        m_prev = m_ref[e]
        m_new = jnp.maximum(m_prev, cmax)
        alpha = jnp.exp2(m_prev - m_new)
        pt = jnp.exp2(st - m_new).astype(BF16)
        acc_ref[e] = alpha * acc_ref[e] + jnp.dot(vt, pt, preferred_element_type=F32)
        m_ref[e] = m_new

    def values(e, start, width):
        return jnp.concatenate(
            [vt_ref[e * FOX_HEAD_DIM:(e + 1) * FOX_HEAD_DIM, pl.ds(start, width)],
             jnp.ones((BF16_ROWS, width), BF16)], axis=0)

    def accumulate(j, e, s_ref, c_ref):
        update(e, s_ref[e], c_ref[e], values(e, pl.multiple_of(j * TK, TK), TK))

    def accumulate_last(j, s_ref):
        n_pieces = TK // TQ
        tri = (lax.broadcasted_iota(jnp.int32, (TQ, TQ), 0)
               <= lax.broadcasted_iota(jnp.int32, (TQ, TQ), 1))
        for diag in range(n_pieces):
            @pl.when(i % n_pieces == diag)
            def _():
                for e in range(2):
                    for piece in range(diag + 1):
                        st = s_ref[e, piece * TQ:(piece + 1) * TQ, :]
                        if piece == diag:
                            st = jnp.where(tri, st, NEG_BIG)
                        start = pl.multiple_of(j * TK + piece * TQ, TQ)
                        update(e, st, jnp.max(st, axis=0, keepdims=True), values(e, start, TQ))

    buf_a = (sa_ref, ca_ref)
    buf_b = (sb_ref, cb_ref)

    nfull = (i * TQ) // TK
    for e in range(2):
        scores(0, e, *buf_a)

    def pair(jj, carry):
        j = 2 * jj
        for e in range(2):
            scores(j + 1, e, *buf_b)
            accumulate(j, e, *buf_a)
        for e in range(2):
            scores(j + 2, e, *buf_a)
            accumulate(j + 1, e, *buf_b)
        return carry

    lax.fori_loop(0, nfull // 2, pair, 0)

    @pl.when(nfull % 2 == 1)
    def _():
        for e in range(2):
            scores(nfull, e, *buf_b)
            accumulate(nfull - 1, e, *buf_a)
        accumulate_last(nfull, sb_ref)

    @pl.when(nfull % 2 == 0)
    def _():
        accumulate_last(nfull, sa_ref)

    outs = []
    for e in range(2):
        acc = acc_ref[e]
        outs.append(acc[:FOX_HEAD_DIM] / acc[FOX_HEAD_DIM:FOX_HEAD_DIM + 1])
    o_ref[...] = jnp.concatenate(outs, axis=0).T.astype(o_ref.dtype)


def _fox_attention(q, qb, k, kb, vt):
    s = q.shape[0]
    assert TK % TQ == 0 and s % TK == 0
    blk = pl.BlockSpec((TQ, LANES), lambda p, i: (i, p))
    res = pl.BlockSpec((s, LANES), lambda p, i: (0, p), pipeline_mode=pl.Buffered(1))
    return pl.pallas_call(
        _fox_kernel,
        grid=(FOX_PAIRS, s // TQ),
        in_specs=[blk, blk, res, res,
                  pl.BlockSpec((LANES, s), lambda p, i: (p, 0), pipeline_mode=pl.Buffered(1))],
        out_specs=blk,
        out_shape=jax.ShapeDtypeStruct((s, FOX_WIDTH), BF16),
        scratch_shapes=[pltpu.VMEM((2, s, LANES), BF16),
                        pltpu.VMEM((2, 1, TQ), F32), pltpu.VMEM((2, PV_ROWS, TQ), F32),
                        pltpu.VMEM((2, TK, TQ), F32), pltpu.VMEM((2, 1, TQ), F32),
                        pltpu.VMEM((2, TK, TQ), F32), pltpu.VMEM((2, 1, TQ), F32)],
        compiler_params=pltpu.CompilerParams(dimension_semantics=("arbitrary", "arbitrary"),
                                             vmem_limit_bytes=VMEM_LIMIT),
        name="fox_attention",
    )(q, qb, k, kb, vt)


def _ret_kernel(q_ref, kt_ref, v_ref, g_ref, inner_ref, xi_ref, zeta_ref, gc_ref, o_ref, r_ref):
    @pl.when(pl.program_id(0) == 0)
    def _():
        r_ref[...] = jnp.zeros_like(r_ref)

    c = RET_CHUNK
    n_chunks = q_ref.shape[0] // c
    tiles = [(hh, ci) for hh in range(RET_HEADS) for ci in range(n_chunks)]

    def head_cols(hh):
        return slice(hh * RET_HEAD_DIM, (hh + 1) * RET_HEAD_DIM)

    def chunk_rows(ci):
        return slice(ci * c, (ci + 1) * c)

    intra, kv = {}, {}
    for hh, ci in tiles:
        q = q_ref[chunk_rows(ci), head_cols(hh)]
        kt = kt_ref[head_cols(hh), chunk_rows(ci)]
        v = v_ref[chunk_rows(ci), head_cols(hh)]
        intra[hh, ci] = (jnp.dot(q, kt, preferred_element_type=F32) * inner_ref[hh]).astype(BF16)
        kzt = (kt.astype(F32) * zeta_ref[hh]).astype(BF16)
        kv[hh, ci] = jnp.dot(kzt, v, preferred_element_type=F32)

    state = {}
    for hh in range(RET_HEADS):
        r_state = r_ref[hh]
        for ci in range(n_chunks):
            state[hh, ci] = r_state.astype(BF16)
            r_state = r_state * gc_ref[hh] + kv[hh, ci]
        r_ref[hh] = r_state

    for hh, ci in tiles:
        q = q_ref[chunk_rows(ci), head_cols(hh)]
        v = v_ref[chunk_rows(ci), head_cols(hh)]
        o = (jnp.dot(intra[hh, ci], v, preferred_element_type=F32)
             + jnp.dot(q, state[hh, ci], preferred_element_type=F32) * xi_ref[hh])
        mu = jnp.mean(o, axis=-1, keepdims=True)
        d = o - mu
        var = jnp.mean(d * d, axis=-1, keepdims=True)
        on = d * lax.rsqrt(var + GN_EPS)
        g = g_ref[chunk_rows(ci), head_cols(hh)].astype(F32)
        o_ref[chunk_rows(ci), head_cols(hh)] = (g * jax.nn.sigmoid(g) * on).astype(o_ref.dtype)


def _retention(rq, rkt, rv, rg, inner, xi_b, zeta_b, gc_b):
    s = rq.shape[0]
    blk = pl.BlockSpec((TR, RET_WIDTH), lambda i: (i, 0))
    blk_t = pl.BlockSpec((RET_WIDTH, TR), lambda i: (0, i))
    tab = _const_spec((RET_HEADS, RET_CHUNK, RET_CHUNK))
    return pl.pallas_call(
        _ret_kernel,
        grid=(s // TR,),
        in_specs=[blk, blk_t, blk, blk, tab, tab, tab, _const_spec((RET_HEADS, 1, RET_HEAD_DIM))],
        out_specs=blk,
        out_shape=jax.ShapeDtypeStruct((s, RET_WIDTH), BF16),
        scratch_shapes=[pltpu.VMEM((RET_HEADS, RET_HEAD_DIM, RET_HEAD_DIM), F32)],
        compiler_params=pltpu.CompilerParams(dimension_semantics=("arbitrary",)),
        name="retention",
    )(rq, rkt, rv, rg, inner, xi_b, zeta_b, gc_b)


def _mix_ffn_kernel(fox_ref, ret_ref, x_ref, wf_ref, wr_ref, g1_ref, l1g_ref, l1b_ref,
                    sc_ref, sh_ref, g2_ref, wu_ref, cw_ref, cb_ref, wd_ref, l2g_ref, l2b_ref,
                    o_ref, carry_ref, slab_ref, y_ref):
    tm = x_ref.shape[0]
    grp = tm // SUBLANES
    n_slab = D_MODEL // LANES

    @pl.when(pl.program_id(0) == 0)
    def _():
        carry_ref[...] = jnp.zeros_like(carry_ref)

    mix = (jnp.dot(fox_ref[...], wf_ref[...], preferred_element_type=F32)
           + jnp.dot(ret_ref[...], wr_ref[...], preferred_element_type=F32))
    x1_nat = _layer_norm(ALPHA * x_ref[...] + g1_ref[...] * mix, l1g_ref[...], l1b_ref[...])

    for c in range(n_slab):
        slab_ref[c] = x1_nat[:, c * LANES:(c + 1) * LANES]
    x1 = jnp.concatenate(
        [jnp.concatenate([slab_ref[c, pl.ds(v, SUBLANES, stride=grp), :] for v in range(grp)], axis=0)
         for c in range(n_slab)], axis=1)

    h = (x1 * (1.0 + sc_ref[...]) + sh_ref[...]).astype(BF16)
    first_sublane = lax.broadcasted_iota(jnp.int32, (SUBLANES, FF_CHUNK), 0) == 0

    def conv_up(c0):
        cols = slice(c0, c0 + FF_CHUNK)
        up = jnp.dot(h, wu_ref[:, cols], preferred_element_type=F32)
        prev = carry_ref[:, cols]
        tail = []
        for g in range(2):
            rows = slice(tm - (2 - g) * SUBLANES, tm - (1 - g) * SUBLANES)
            tail.append(jnp.where(first_sublane,
                                  pltpu.roll(prev[g * SUBLANES:(g + 1) * SUBLANES], 1, 0),
                                  pltpu.roll(up[rows], 1, 0)))
        carry_ref[:, cols] = up[tm - 2 * SUBLANES:tm]
        back1 = jnp.concatenate([tail[1], up[:tm - SUBLANES]], axis=0)
        back2 = jnp.concatenate([tail[0], tail[1], up[:tm - 2 * SUBLANES]], axis=0)
        cw = cw_ref[:, cols] * RSQRT2
        return cb_ref[:, cols] * RSQRT2 + back2 * cw[0:1] + back1 * cw[1:2] + up * cw[2:3]

    for ci in range(D_FF // FF_CHUNK):
        a = conv_up(ci * FF_CHUNK)
        b = conv_up(D_FF + ci * FF_CHUNK)
        y_ref[:, ci * FF_CHUNK:(ci + 1) * FF_CHUNK] = (a * (1.0 + lax.erf(a)) * b).astype(BF16)

    ffn = jnp.dot(y_ref[...], wd_ref[...], preferred_element_type=F32)
    out = _layer_norm(ALPHA * x1 + g2_ref[...] * ffn, l2g_ref[...], l2b_ref[...])

    for c in range(n_slab):
        for v in range(grp):
            slab_ref[c, pl.ds(v, SUBLANES, stride=grp), :] = out[v * SUBLANES:(v + 1) * SUBLANES,
                                                                 c * LANES:(c + 1) * LANES]
    for c in range(n_slab):
        o_ref[:, c * LANES:(c + 1) * LANES] = slab_ref[c]


def _mix_ffn(fox, ret, x, w_fox, w_ret, g1, ln1_g, ln1_b, sc2, sh2, g2, w_up, conv_w, conv_b, w_down,
             ln2_g, ln2_b):
    s, d = x.shape
    tm = TM_FFN
    row = lambda w: pl.BlockSpec((tm, w), lambda i: (i, 0))
    vec = _const_spec((1, d))
    return pl.pallas_call(
        _mix_ffn_kernel,
        grid=(s // tm,),
        in_specs=[row(FOX_WIDTH), row(RET_WIDTH), row(d),
                  _const_spec((FOX_WIDTH, d)), _const_spec((RET_WIDTH, d)), vec, vec, vec,
                  vec, vec, vec,
                  _const_spec((d, 2 * D_FF)), _const_spec((CONV_WIDTH, 2 * D_FF)),
                  _const_spec((1, 2 * D_FF)), _const_spec((D_FF, d)), vec, vec],
        out_specs=row(d),
        out_shape=jax.ShapeDtypeStruct((s, d), F32),
        scratch_shapes=[pltpu.VMEM((2 * SUBLANES, 2 * D_FF), F32),
                        pltpu.VMEM((d // LANES, tm, LANES), F32),
                        pltpu.VMEM((tm, D_FF), BF16)],
        compiler_params=pltpu.CompilerParams(dimension_semantics=("arbitrary",),
                                             vmem_limit_bytes=VMEM_LIMIT),
        name="mix_ffn",
    )(fox, ret, x, w_fox, w_ret, g1, ln1_g, ln1_b, sc2, sh2, g2, w_up, conv_w, conv_b, w_down,
      ln2_g, ln2_b)


def _placement_tables():
    e = np.zeros((LANES, 2 * FOX_WIDTH), np.float32)
    ones_q = np.zeros((1, FOX_WIDTH), np.float32)
    ones_k = np.zeros((1, FOX_WIDTH), np.float32)
    for h in range(FOX_HEADS):
        base = (h // 2) * LANES + (FOX_HEAD_DIM if h % 2 == 0 else 0)
        for j in range(N_PIECES):
            e[j * FOX_HEADS + h, base + BIAS_CUM + j] = 1.0
            ones_q[0, base + BIAS_ONE + j] = 1.0
            ones_k[0, base + BIAS_CUM + j] = 1.0
            e[j * FOX_HEADS + h, FOX_WIDTH + base + BIAS_ONE + j] = -1.0
    return jnp.asarray(e, BF16), jnp.asarray(ones_q), jnp.asarray(ones_k)


def _rotation_tables(s, tm):
    dk = RET_HEAD_DIM
    inv_freq = ROPE_BASE ** (-np.arange(0, dk, 2, dtype=np.float64) / dk)
    inv_freq = np.concatenate([inv_freq, inv_freq])
    sign = np.concatenate([-np.ones(dk // 2), np.ones(dk // 2)])
    ang_a = (np.arange(s // tm, dtype=np.float64) * tm)[:, None] * inv_freq[None, :]
    ang_b = np.arange(tm, dtype=np.float64)[:, None] * inv_freq[None, :]
    rot_a = np.stack([np.cos(ang_a), np.sin(ang_a)], axis=1)
    rot_b = np.stack([np.cos(ang_b), np.sin(ang_b), sign * np.cos(ang_b), sign * np.sin(ang_b)])
    f32 = lambda a: jnp.asarray(a, F32)
    return f32(rot_a), f32(rot_b), f32(rot_a.transpose(0, 2, 1)), f32(rot_b.transpose(0, 2, 1))


def _decay_tables():
    c = RET_CHUNK
    log_gamma = np.log1p(-np.exp2(-5.0 - np.arange(RET_HEADS, dtype=np.float64)))
    idx = np.arange(c, dtype=np.float64)
    diff = idx[:, None] - idx[None, :]
    inner = np.where(diff[None] >= 0,
                     np.exp(np.maximum(diff, 0.0)[None] * log_gamma[:, None, None]), 0.0)
    xi = np.exp((idx[None, :] + 1.0) * log_gamma[:, None])
    zeta = np.exp((c - 1.0 - idx[None, :]) * log_gamma[:, None])
    g_chunk = np.exp(c * log_gamma)
    bshape = (RET_HEADS, c, RET_HEAD_DIM)
    f32 = lambda a: jnp.asarray(a, F32)
    return (f32(inner),
            f32(np.broadcast_to(xi[:, :, None], bshape)),
            f32(np.broadcast_to(zeta[:, None, :], bshape)),
            f32(np.broadcast_to(g_chunk[:, None, None], (RET_HEADS, 1, RET_HEAD_DIM))))


def kernel(x, c, w_ada, b_ada, w_in, b_f, w_out, ln1_g, ln1_b, w_up, conv_w, conv_b, w_down, ln2_g, ln2_b):
    b, s, d = x.shape
    assert (b, s, d) == (1, SEQ, D_MODEL) and w_ada.shape[0] == DEPTH
    xs = x[0]

    wi = w_in[0]
    o = 0
    parts = {}
    for name, width in (("fq", FOX_WIDTH), ("fk", FOX_WIDTH), ("fv", FOX_WIDTH), ("ff", FOX_HEADS),
                        ("rq", RET_WIDTH), ("rk", RET_WIDTH), ("rv", RET_WIDTH), ("rg", RET_WIDTH)):
        parts[name] = wi[:, o:o + width]
        o += width
    w_all = jnp.concatenate(
        [parts["fq"], parts["fk"], jnp.pad(parts["ff"], ((0, 0), (0, LANES - FOX_HEADS))),
         parts["rq"], parts["rv"], parts["rg"]], axis=1).astype(BF16)
    w_t = jnp.concatenate([parts["fv"], parts["rk"]], axis=1).T.astype(BF16)
    bf_row = jnp.pad(b_f[0][None, :], ((0, 0), (0, LANES - FOX_HEADS)))
    wo = w_out[0].astype(BF16)

    e_mat, ones_q, ones_k = _placement_tables()
    rot_a, rot_b, rot_at, rot_bt = _rotation_tables(s, TM_PROJ)
    inner, xi_b, zeta_b, gc_b = _decay_tables()

    mod = _adaln_mod(c.reshape(d, 1), w_ada[0], b_ada[0][None, :])
    sh1, sc1, g1, sh2, sc2, g2 = [mod[:, i * d:(i + 1) * d] for i in range(6)]

    q, qb, k, kb, vt, rq, rkt, rv, rg = _in_proj(
        xs, sc1, sh1, w_all, w_t, bf_row, rot_a, rot_b, rot_at, rot_bt, e_mat, ones_q, ones_k)
    fox = _fox_attention(q, qb, k, kb, vt)
    ret = _retention(rq, rkt, rv, rg, inner, xi_b, zeta_b, gc_b)
    out = _mix_ffn(fox, ret, xs, wo[:FOX_WIDTH], wo[FOX_WIDTH:], g1, ln1_g[0][None, :], ln1_b[0][None, :],
                   sc2, sh2, g2, w_up[0].astype(BF16), conv_w[0], conv_b[0][None, :],
                   w_down[0].astype(BF16), ln2_g[0][None, :], ln2_b[0][None, :])
    return out[None]
```

```python
import math

import jax
import jax.numpy as jnp
import numpy as np
from jax import lax
from jax.experimental import pallas as pl
from jax.experimental.pallas import tpu as pltpu

F32 = jnp.float32
BF16 = jnp.bfloat16

D_MODEL = 1024
SEQ = 16384
FOX_HEADS = 8
FOX_HEAD_DIM = 64
FOX_WIDTH = FOX_HEADS * FOX_HEAD_DIM
RET_HEADS = 4
RET_HEAD_DIM = 128
RET_WIDTH = RET_HEADS * RET_HEAD_DIM
D_FF = 2816
CONV_WIDTH = 3
RET_CHUNK = 128
ROPE_BASE = 10000.0
LN_EPS = 1e-5
GN_EPS = 1e-6
DEPTH = 1
ALPHA = (2 * DEPTH) ** 0.25

LANES = 128
SUBLANES = 8
BF16_ROWS = 16
VMEM_LIMIT = 56 * 1024 * 1024

FOX_PAIRS = FOX_HEADS // 2
BIAS_CUM = 0
BIAS_ONE = 3
N_PIECES = 3
PV_ROWS = FOX_HEAD_DIM + BF16_ROWS
NEG_BIG = -1e30
LOG2_E = math.log2(math.e)
RSQRT2 = 2.0 ** -0.5

COL_Q = 0
COL_K = COL_Q + FOX_WIDTH
COL_F = COL_K + FOX_WIDTH
COL_RQ = COL_F + LANES
COL_RV = COL_RQ + RET_WIDTH
COL_RG = COL_RV + RET_WIDTH
N_COLS = COL_RG + RET_WIDTH
ROW_VT = 0
ROW_RKT = ROW_VT + FOX_WIDTH
N_ROWS_T = ROW_RKT + RET_WIDTH

TM_PROJ = 512
TQ = 512
TK = 1024
TR = 512
TM_FFN = 512
FF_CHUNK = 256


def _const_spec(shape):
    return pl.BlockSpec(shape, lambda *_: (0,) * len(shape), pipeline_mode=pl.Buffered(1))


def _split3(x):
    p0 = x.astype(BF16).astype(F32)
    r1 = x - p0
    p1 = r1.astype(BF16).astype(F32)
    p2 = (r1 - p1).astype(BF16).astype(F32)
    return p0, p1, p2


def _layer_norm(y, g, b):
    mu = jnp.mean(y, axis=-1, keepdims=True)
    d = y - mu
    var = jnp.mean(d * d, axis=-1, keepdims=True)
    return d * lax.rsqrt(var + LN_EPS) * g + b


def _mod_kernel(c_ref, w_ref, b_ref, o_ref):
    c = c_ref[...]
    sc = c * jax.nn.sigmoid(c)
    o_ref[...] = jnp.sum(w_ref[...] * sc, axis=0, keepdims=True) + b_ref[...]


def _adaln_mod(c_col, w_ada, b_ada):
    d, n = w_ada.shape
    tn = 1024
    return pl.pallas_call(
        _mod_kernel,
        grid=(n // tn,),
        in_specs=[pl.BlockSpec((d, 1), lambda j: (0, 0)),
                  pl.BlockSpec((d, tn), lambda j: (0, j)),
                  pl.BlockSpec((1, tn), lambda j: (0, j))],
        out_specs=pl.BlockSpec((1, tn), lambda j: (0, j)),
        out_shape=jax.ShapeDtypeStruct((1, n), F32),
        compiler_params=pltpu.CompilerParams(dimension_semantics=("arbitrary",)),
        name="adaln_mod",
    )(c_col, w_ada, b_ada)


def _in_proj_kernel(x_ref, sc_ref, sh_ref, w_ref, wt_ref, bf_ref, ra_ref, rb_ref, rat_ref, rbt_ref,
                    e_ref, oq_ref, ok_ref,
                    q_out, qb_out, k_out, kb_out, vt_out, rq_out, rkt_out, rv_out, rg_out,
                    carry_ref):
    tm = x_ref.shape[0]

    @pl.when(pl.program_id(0) == 0)
    def _():
        carry_ref[...] = jnp.zeros_like(carry_ref)

    h = (x_ref[...] * (1.0 + sc_ref[...]) + sh_ref[...]).astype(BF16)

    def proj(c0, width):
        return jnp.dot(h, w_ref[:, c0:c0 + width], preferred_element_type=F32)

    lane = lax.broadcasted_iota(jnp.int32, (tm, LANES), 1)
    head_lane = lane < FOX_HEADS
    ff = proj(COL_F, LANES) + bf_ref[...]
    logf = jnp.minimum(ff, 0.0) - jnp.log1p(jnp.exp(-jnp.abs(ff)))
    logf = jnp.where(head_lane, logf, 0.0)
    p0, p1, p2 = _split3(logf)
    pieces = p0 + pltpu.roll(p1, FOX_HEADS, 1) + pltpu.roll(p2, 2 * FOX_HEADS, 1)
    row = lax.broadcasted_iota(jnp.int32, (tm, tm), 0)
    col = lax.broadcasted_iota(jnp.int32, (tm, tm), 1)
    tril = jnp.where(row >= col, 1.0, 0.0).astype(BF16)
    c3 = jnp.dot(tril, pieces.astype(BF16), preferred_element_type=F32)
    cum = c3 + pltpu.roll(c3, LANES - FOX_HEADS, 1) + pltpu.roll(c3, LANES - 2 * FOX_HEADS, 1)
    cum = jnp.where(head_lane, cum, 0.0) + carry_ref[0:1, :]
    carry_ref[0:1, :] = cum[tm - 1:tm, :]
    c0_, c1_, c2_ = _split3(cum * LOG2_E)
    cum_pieces = (c0_ + pltpu.roll(c1_, FOX_HEADS, 1) + pltpu.roll(c2_, 2 * FOX_HEADS, 1)).astype(BF16)
    bias = jnp.dot(cum_pieces, e_ref[...], preferred_element_type=F32)
    qb_out[...] = (bias[:, :FOX_WIDTH] + oq_ref[...]).astype(BF16)
    kb_out[...] = (bias[:, FOX_WIDTH:] + ok_ref[...]).astype(BF16)

    q_out[...] = (proj(COL_Q, FOX_WIDTH) * (FOX_HEAD_DIM ** -0.5 * LOG2_E)).astype(BF16)
    k_out[...] = proj(COL_K, FOX_WIDTH).astype(BF16)
    tp = lax.dot_general(wt_ref[...], h, (((1,), (1,)), ((), ())), preferred_element_type=F32)
    vt_out[...] = tp[ROW_VT:ROW_VT + FOX_WIDTH].astype(BF16)

    half = RET_HEAD_DIM // 2
    ca, sa = ra_ref[0, 0:1, :], ra_ref[0, 1:2, :]
    cos = ca * rb_ref[0] - sa * rb_ref[1]
    sin = sa * rb_ref[2] + ca * rb_ref[3]
    cat, sat = rat_ref[0, :, 0:1], rat_ref[0, :, 1:2]
    cost = cat * rbt_ref[0] - sat * rbt_ref[1]
    sint = sat * rbt_ref[2] + cat * rbt_ref[3]
    kscale = RET_HEAD_DIM ** -0.5
    two = 2 * RET_HEAD_DIM
    for pair in range(RET_HEADS // 2):
        a2 = proj(COL_RQ + pair * two, two)
        for e in range(2):
            lo = pair * two + e * RET_HEAD_DIM
            a = a2[:, e * RET_HEAD_DIM:(e + 1) * RET_HEAD_DIM]
            rq_out[:, lo:lo + RET_HEAD_DIM] = (a * cos + pltpu.roll(a, half, 1) * sin).astype(BF16)
    for hh in range(RET_HEADS):
        lo = hh * RET_HEAD_DIM
        b = tp[ROW_RKT + lo:ROW_RKT + lo + RET_HEAD_DIM]
        b_rot = jnp.concatenate([b[half:], b[:half]], axis=0)
        rkt_out[lo:lo + RET_HEAD_DIM, :] = ((b * cost + b_rot * sint) * kscale).astype(BF16)
    rv_out[...] = proj(COL_RV, RET_WIDTH).astype(BF16)
    rg_out[...] = proj(COL_RG, RET_WIDTH).astype(BF16)


def _in_proj(x, sc1, sh1, w_all, w_t, bf_row, rot_a, rot_b, rot_at, rot_bt, e_mat, ones_q, ones_k):
    s, d = x.shape
    tm = TM_PROJ
    row = lambda w: pl.BlockSpec((tm, w), lambda i: (i, 0))
    col = lambda h: pl.BlockSpec((h, tm), lambda i: (0, i))
    out_shapes = ([jax.ShapeDtypeStruct((s, FOX_WIDTH), BF16)] * 4
                  + [jax.ShapeDtypeStruct((FOX_WIDTH, s), BF16),
                     jax.ShapeDtypeStruct((s, RET_WIDTH), BF16),
                     jax.ShapeDtypeStruct((RET_WIDTH, s), BF16)]
                  + [jax.ShapeDtypeStruct((s, RET_WIDTH), BF16)] * 2)
    return pl.pallas_call(
        _in_proj_kernel,
        grid=(s // tm,),
        in_specs=[row(d), _const_spec((1, d)), _const_spec((1, d)),
                  _const_spec((d, N_COLS)), _const_spec((N_ROWS_T, d)), _const_spec((1, LANES)),
                  pl.BlockSpec((1, 2, RET_HEAD_DIM), lambda i: (i, 0, 0)),
                  _const_spec((4, tm, RET_HEAD_DIM)),
                  pl.BlockSpec((1, RET_HEAD_DIM, 2), lambda i: (i, 0, 0)),
                  _const_spec((4, RET_HEAD_DIM, tm)),
                  _const_spec((LANES, 2 * FOX_WIDTH)),
                  _const_spec((1, FOX_WIDTH)), _const_spec((1, FOX_WIDTH))],
        out_specs=([row(FOX_WIDTH)] * 4
                   + [col(FOX_WIDTH), row(RET_WIDTH), col(RET_WIDTH), row(RET_WIDTH), row(RET_WIDTH)]),
        out_shape=out_shapes,
        scratch_shapes=[pltpu.VMEM((SUBLANES, LANES), F32)],
        compiler_params=pltpu.CompilerParams(dimension_semantics=("arbitrary",),
                                             vmem_limit_bytes=VMEM_LIMIT),
        name="in_proj",
    )(x, sc1, sh1, w_all, w_t, bf_row, rot_a, rot_b, rot_at, rot_bt, e_mat, ones_q, ones_k)


def _fox_kernel(q_ref, qb_ref, k_ref, kb_ref, vt_ref, o_ref, kaug_ref, m_ref, acc_ref,
                sa_ref, ca_ref, sb_ref, cb_ref):
    i = pl.program_id(1)
    lane = lax.broadcasted_iota(jnp.int32, (1, LANES), 1)
    own = (lane < FOX_HEAD_DIM, lane >= FOX_HEAD_DIM)

    @pl.when(i == 0)
    def _():
        for e in range(2):
            kaug_ref[e] = jnp.where(own[e], k_ref[...], kb_ref[...])

    q = [jnp.where(own[e], q_ref[...], qb_ref[...]) for e in range(2)]
    m_ref[...] = jnp.full_like(m_ref, NEG_BIG)
    acc_ref[...] = jnp.zeros_like(acc_ref)

    def scores(j, e, s_ref, c_ref):
        start = pl.multiple_of(j * TK, TK)
        k = kaug_ref[e, pl.ds(start, TK), :]
        st = lax.dot_general(k, q[e], (((1,), (1,)), ((), ())), preferred_element_type=F32)
        s_ref[e] = st
        c_ref[e] = jnp.max(st, axis=0, keepdims=True)

    def update(e, st, cmax, vt):
        m_prev = m_ref[e]
        m_new = jnp.maximum(m_prev, cmax)
        alpha = jnp.exp2(m_prev - m_new)
        pt = jnp.exp2(st - m_new).astype(BF16)
        acc_ref[e] = alpha * acc_ref[e] + jnp.dot(vt, pt, preferred_element_type=F32)
        m_ref[e] = m_new

    def values(e, start, width):
        return jnp.concatenate(
            [vt_ref[e * FOX_HEAD_DIM:(e + 1) * FOX_HEAD_DIM, pl.ds(start, width)],
             jnp.ones((BF16_ROWS, width), BF16)], axis=0)

    def accumulate(j, e, s_ref, c_ref):
        update(e, s_ref[e], c_ref[e], values(e, pl.multiple_of(j * TK, TK), TK))

    def accumulate_last(j, s_ref):
        n_pieces = TK // TQ
        tri = (lax.broadcasted_iota(jnp.int32, (TQ, TQ), 0)
               <= lax.broadcasted_iota(jnp.int32, (TQ, TQ), 1))
        for diag in range(n_pieces):
            @pl.when(i % n_pieces == diag)
            def _():
                for e in range(2):
                    for piece in range(diag + 1):
                        st = s_ref[e, piece * TQ:(piece + 1) * TQ, :]
                        if piece == diag:
                            st = jnp.where(tri, st, NEG_BIG)
                        start = pl.multiple_of(j * TK + piece * TQ, TQ)
                        update(e, st, jnp.max(st, axis=0, keepdims=True), values(e, start, TQ))

    buf_a = (sa_ref, ca_ref)
    buf_b = (sb_ref, cb_ref)

    nfull = (i * TQ) // TK
    for e in range(2):
        scores(0, e, *buf_a)

    def pair(jj, carry):
        j = 2 * jj
        for e in range(2):
            scores(j + 1, e, *buf_b)
            accumulate(j, e, *buf_a)
        for e in range(2):
            scores(j + 2, e, *buf_a)
            accumulate(j + 1, e, *buf_b)
        return carry

    def quad(jq, carry):
        return pair(2 * jq + 1, pair(2 * jq, carry))

    lax.fori_loop(0, nfull // 4, quad, 0)
    lax.fori_loop(2 * (nfull // 4), nfull // 2, pair, 0)

    @pl.when(nfull % 2 == 1)
    def _():
        for e in range(2):
            scores(nfull, e, *buf_b)
            accumulate(nfull - 1, e, *buf_a)
        accumulate_last(nfull, sb_ref)

    @pl.when(nfull % 2 == 0)
    def _():
        accumulate_last(nfull, sa_ref)

    outs = []
    for e in range(2):
        acc = acc_ref[e]
        outs.append(acc[:FOX_HEAD_DIM] / acc[FOX_HEAD_DIM:FOX_HEAD_DIM + 1])
    o_ref[...] = jnp.concatenate(outs, axis=0).T.astype(o_ref.dtype)


def _fox_attention(q, qb, k, kb, vt):
    s = q.shape[0]
    assert TK % TQ == 0 and s % TK == 0
    blk = pl.BlockSpec((TQ, LANES), lambda p, i: (i, p))
    res = pl.BlockSpec((s, LANES), lambda p, i: (0, p), pipeline_mode=pl.Buffered(1))
    return pl.pallas_call(
        _fox_kernel,
        grid=(FOX_PAIRS, s // TQ),
        in_specs=[blk, blk, res, res,
                  pl.BlockSpec((LANES, s), lambda p, i: (p, 0), pipeline_mode=pl.Buffered(1))],
        out_specs=blk,
        out_shape=jax.ShapeDtypeStruct((s, FOX_WIDTH), BF16),
        scratch_shapes=[pltpu.VMEM((2, s, LANES), BF16),
                        pltpu.VMEM((2, 1, TQ), F32), pltpu.VMEM((2, PV_ROWS, TQ), F32),
                        pltpu.VMEM((2, TK, TQ), F32), pltpu.VMEM((2, 1, TQ), F32),
                        pltpu.VMEM((2, TK, TQ), F32), pltpu.VMEM((2, 1, TQ), F32)],
        compiler_params=pltpu.CompilerParams(dimension_semantics=("arbitrary", "arbitrary"),
                                             vmem_limit_bytes=VMEM_LIMIT),
        name="fox_attention",
    )(q, qb, k, kb, vt)


def _ret_kernel(q_ref, kt_ref, v_ref, g_ref, inner_ref, xi_ref, zeta_ref, gc_ref, o_ref, r_ref):
    @pl.when(pl.program_id(0) == 0)
    def _():
        r_ref[...] = jnp.zeros_like(r_ref)

    c = RET_CHUNK
    n_chunks = q_ref.shape[0] // c
    tiles = [(hh, ci) for hh in range(RET_HEADS) for ci in range(n_chunks)]

    def head_cols(hh):
        return slice(hh * RET_HEAD_DIM, (hh + 1) * RET_HEAD_DIM)

    def chunk_rows(ci):
        return slice(ci * c, (ci + 1) * c)

    intra, kv = {}, {}
    for hh, ci in tiles:
        q = q_ref[chunk_rows(ci), head_cols(hh)]
        kt = kt_ref[head_cols(hh), chunk_rows(ci)]
        v = v_ref[chunk_rows(ci), head_cols(hh)]
        intra[hh, ci] = (jnp.dot(q, kt, preferred_element_type=F32) * inner_ref[hh]).astype(BF16)
        kzt = (kt.astype(F32) * zeta_ref[hh]).astype(BF16)
        kv[hh, ci] = jnp.dot(kzt, v, preferred_element_type=F32)

    state = {}
    for hh in range(RET_HEADS):
        r_state = r_ref[hh]
        for ci in range(n_chunks):
            state[hh, ci] = r_state.astype(BF16)
            r_state = r_state * gc_ref[hh] + kv[hh, ci]
        r_ref[hh] = r_state

    for hh, ci in tiles:
        q = q_ref[chunk_rows(ci), head_cols(hh)]
        v = v_ref[chunk_rows(ci), head_cols(hh)]
        o = (jnp.dot(intra[hh, ci], v, preferred_element_type=F32)
             + jnp.dot(q, state[hh, ci], preferred_element_type=F32) * xi_ref[hh])
        mu = jnp.mean(o, axis=-1, keepdims=True)
        d = o - mu
        var = jnp.mean(d * d, axis=-1, keepdims=True)
        on = d * lax.rsqrt(var + GN_EPS)
        g = g_ref[chunk_rows(ci), head_cols(hh)].astype(F32)
        o_ref[chunk_rows(ci), head_cols(hh)] = (g * jax.nn.sigmoid(g) * on).astype(o_ref.dtype)


def _retention(rq, rkt, rv, rg, inner, xi_b, zeta_b, gc_b):
    s = rq.shape[0]
    blk = pl.BlockSpec((TR, RET_WIDTH), lambda i: (i, 0))
    blk_t = pl.BlockSpec((RET_WIDTH, TR), lambda i: (0, i))
    tab = _const_spec((RET_HEADS, RET_CHUNK, RET_CHUNK))
    return pl.pallas_call(
        _ret_kernel,
        grid=(s // TR,),
        in_specs=[blk, blk_t, blk, blk, tab, tab, tab, _const_spec((RET_HEADS, 1, RET_HEAD_DIM))],
        out_specs=blk,
        out_shape=jax.ShapeDtypeStruct((s, RET_WIDTH), BF16),
        scratch_shapes=[pltpu.VMEM((RET_HEADS, RET_HEAD_DIM, RET_HEAD_DIM), F32)],
        compiler_params=pltpu.CompilerParams(dimension_semantics=("arbitrary",)),
        name="retention",
    )(rq, rkt, rv, rg, inner, xi_b, zeta_b, gc_b)


def _mix_ffn_kernel(fox_ref, ret_ref, x_ref, fox0_ref, ret0_ref, x0_ref,
                    wf_ref, wr_ref, g1_ref, l1g_ref, l1b_ref,
                    sc_ref, sh_ref, g2_ref, wu_ref, cw_ref, cb_ref, wd_ref, l2g_ref, l2b_ref,
                    o_ref, carry_ref, slab_in_ref, slab_out_ref, y_ref, x1_ref):
    tm = x_ref.shape[0]
    grp = tm // SUBLANES
    n_slab = D_MODEL // LANES

    def front(fox, ret, x):
        mix = (jnp.dot(fox, wf_ref[...], preferred_element_type=F32)
               + jnp.dot(ret, wr_ref[...], preferred_element_type=F32))
        x1_nat = _layer_norm(ALPHA * x + g1_ref[...] * mix, l1g_ref[...], l1b_ref[...])
        for c in range(n_slab):
            slab_in_ref[c] = x1_nat[:, c * LANES:(c + 1) * LANES]
        return jnp.concatenate(
            [jnp.concatenate([slab_in_ref[c, pl.ds(v, SUBLANES, stride=grp), :] for v in range(grp)],
                             axis=0)
             for c in range(n_slab)], axis=1)

    @pl.when(pl.program_id(0) == 0)
    def _():
        carry_ref[...] = jnp.zeros_like(carry_ref)
        x1_ref[...] = front(fox0_ref[...], ret0_ref[...], x0_ref[...])

    h = (x1_ref[...] * (1.0 + sc_ref[...]) + sh_ref[...]).astype(BF16)
    first_sublane = lax.broadcasted_iota(jnp.int32, (SUBLANES, FF_CHUNK), 0) == 0

    def conv_up(c0):
        cols = slice(c0, c0 + FF_CHUNK)
        up = jnp.dot(h, wu_ref[:, cols], preferred_element_type=F32)
        prev = carry_ref[:, cols]
        tail = []
        for g in range(2):
            rows = slice(tm - (2 - g) * SUBLANES, tm - (1 - g) * SUBLANES)
            tail.append(jnp.where(first_sublane,
                                  pltpu.roll(prev[g * SUBLANES:(g + 1) * SUBLANES], 1, 0),
                                  pltpu.roll(up[rows], 1, 0)))
        carry_ref[:, cols] = up[tm - 2 * SUBLANES:tm]
        back1 = jnp.concatenate([tail[1], up[:tm - SUBLANES]], axis=0)
        back2 = jnp.concatenate([tail[0], tail[1], up[:tm - 2 * SUBLANES]], axis=0)
        cw = cw_ref[:, cols] * RSQRT2
        return cb_ref[:, cols] * RSQRT2 + back2 * cw[0:1] + back1 * cw[1:2] + up * cw[2:3]

    for ci in range(D_FF // FF_CHUNK):
        a = conv_up(ci * FF_CHUNK)
        b = conv_up(D_FF + ci * FF_CHUNK)
        y_ref[:, ci * FF_CHUNK:(ci + 1) * FF_CHUNK] = (a * (1.0 + lax.erf(a)) * b).astype(BF16)

    x1_next = front(fox_ref[...], ret_ref[...], x_ref[...])

    ffn = jnp.dot(y_ref[...], wd_ref[...], preferred_element_type=F32)
    out = _layer_norm(ALPHA * x1_ref[...] + g2_ref[...] * ffn, l2g_ref[...], l2b_ref[...])
    x1_ref[...] = x1_next

    for c in range(n_slab):
        for v in range(grp):
            slab_out_ref[c, pl.ds(v, SUBLANES, stride=grp), :] = out[v * SUBLANES:(v + 1) * SUBLANES,
                                                                     c * LANES:(c + 1) * LANES]
    for c in range(n_slab):
        o_ref[:, c * LANES:(c + 1) * LANES] = slab_out_ref[c]


def _mix_ffn(fox, ret, x, w_fox, w_ret, g1, ln1_g, ln1_b, sc2, sh2, g2, w_up, conv_w, conv_b, w_down,
             ln2_g, ln2_b):
    s, d = x.shape
    tm = TM_FFN
    n_tiles = s // tm
    nxt = lambda w: pl.BlockSpec((tm, w), lambda i: (jnp.minimum(i + 1, n_tiles - 1), 0))
    vec = _const_spec((1, d))
    slab = pltpu.VMEM((d // LANES, tm, LANES), F32)
    return pl.pallas_call(
        _mix_ffn_kernel,
        grid=(n_tiles,),
        in_specs=[nxt(FOX_WIDTH), nxt(RET_WIDTH), nxt(d),
                  _const_spec((tm, FOX_WIDTH)), _const_spec((tm, RET_WIDTH)), _const_spec((tm, d)),
                  _const_spec((FOX_WIDTH, d)), _const_spec((RET_WIDTH, d)), vec, vec, vec,
                  vec, vec, vec,
                  _const_spec((d, 2 * D_FF)), _const_spec((CONV_WIDTH, 2 * D_FF)),
                  _const_spec((1, 2 * D_FF)), _const_spec((D_FF, d)), vec, vec],
        out_specs=pl.BlockSpec((tm, d), lambda i: (i, 0)),
        out_shape=jax.ShapeDtypeStruct((s, d), F32),
        scratch_shapes=[pltpu.VMEM((2 * SUBLANES, 2 * D_FF), F32), slab, slab,
                        pltpu.VMEM((tm, D_FF), BF16), pltpu.VMEM((tm, d), F32)],
        compiler_params=pltpu.CompilerParams(dimension_semantics=("arbitrary",),
                                             vmem_limit_bytes=VMEM_LIMIT),
        name="mix_ffn",
    )(fox, ret, x, fox, ret, x, w_fox, w_ret, g1, ln1_g, ln1_b, sc2, sh2, g2, w_up, conv_w, conv_b,
      w_down, ln2_g, ln2_b)


def _placement_tables():
    e = np.zeros((LANES, 2 * FOX_WIDTH), np.float32)
    ones_q = np.zeros((1, FOX_WIDTH), np.float32)
    ones_k = np.zeros((1, FOX_WIDTH), np.float32)
    for h in range(FOX_HEADS):
        base = (h // 2) * LANES + (FOX_HEAD_DIM if h % 2 == 0 else 0)
        for j in range(N_PIECES):
            e[j * FOX_HEADS + h, base + BIAS_CUM + j] = 1.0
            ones_q[0, base + BIAS_ONE + j] = 1.0
            ones_k[0, base + BIAS_CUM + j] = 1.0
            e[j * FOX_HEADS + h, FOX_WIDTH + base + BIAS_ONE + j] = -1.0
    return jnp.asarray(e, BF16), jnp.asarray(ones_q), jnp.asarray(ones_k)


def _rotation_tables(s, tm):
    dk = RET_HEAD_DIM
    inv_freq = ROPE_BASE ** (-np.arange(0, dk, 2, dtype=np.float64) / dk)
    inv_freq = np.concatenate([inv_freq, inv_freq])
    sign = np.concatenate([-np.ones(dk // 2), np.ones(dk // 2)])
    ang_a = (np.arange(s // tm, dtype=np.float64) * tm)[:, None] * inv_freq[None, :]
    ang_b = np.arange(tm, dtype=np.float64)[:, None] * inv_freq[None, :]
    rot_a = np.stack([np.cos(ang_a), np.sin(ang_a)], axis=1)
    rot_b = np.stack([np.cos(ang_b), np.sin(ang_b), sign * np.cos(ang_b), sign * np.sin(ang_b)])
    f32 = lambda a: jnp.asarray(a, F32)
    return f32(rot_a), f32(rot_b), f32(rot_a.transpose(0, 2, 1)), f32(rot_b.transpose(0, 2, 1))


def _decay_tables():
    c = RET_CHUNK
    log_gamma = np.log1p(-np.exp2(-5.0 - np.arange(RET_HEADS, dtype=np.float64)))
    idx = np.arange(c, dtype=np.float64)
    diff = idx[:, None] - idx[None, :]
    inner = np.where(diff[None] >= 0,
                     np.exp(np.maximum(diff, 0.0)[None] * log_gamma[:, None, None]), 0.0)
    xi = np.exp((idx[None, :] + 1.0) * log_gamma[:, None])
    zeta = np.exp((c - 1.0 - idx[None, :]) * log_gamma[:, None])
    g_chunk = np.exp(c * log_gamma)
    bshape = (RET_HEADS, c, RET_HEAD_DIM)
    f32 = lambda a: jnp.asarray(a, F32)
    return (f32(inner),
            f32(np.broadcast_to(xi[:, :, None], bshape)),
            f32(np.broadcast_to(zeta[:, None, :], bshape)),
            f32(np.broadcast_to(g_chunk[:, None, None], (RET_HEADS, 1, RET_HEAD_DIM))))


def kernel(x, c, w_ada, b_ada, w_in, b_f, w_out, ln1_g, ln1_b, w_up, conv_w, conv_b, w_down, ln2_g, ln2_b):
    b, s, d = x.shape
    assert (b, s, d) == (1, SEQ, D_MODEL) and w_ada.shape[0] == DEPTH
    xs = x[0]

    wi = w_in[0]
    o = 0
    parts = {}
    for name, width in (("fq", FOX_WIDTH), ("fk", FOX_WIDTH), ("fv", FOX_WIDTH), ("ff", FOX_HEADS),
                        ("rq", RET_WIDTH), ("rk", RET_WIDTH), ("rv", RET_WIDTH), ("rg", RET_WIDTH)):
        parts[name] = wi[:, o:o + width]
        o += width
    w_all = jnp.concatenate(
        [parts["fq"], parts["fk"], jnp.pad(parts["ff"], ((0, 0), (0, LANES - FOX_HEADS))),
         parts["rq"], parts["rv"], parts["rg"]], axis=1).astype(BF16)
    w_t = jnp.concatenate([parts["fv"], parts["rk"]], axis=1).T.astype(BF16)
    bf_row = jnp.pad(b_f[0][None, :], ((0, 0), (0, LANES - FOX_HEADS)))
    wo = w_out[0].astype(BF16)

    e_mat, ones_q, ones_k = _placement_tables()
    rot_a, rot_b, rot_at, rot_bt = _rotation_tables(s, TM_PROJ)
    inner, xi_b, zeta_b, gc_b = _decay_tables()

    mod = _adaln_mod(c.reshape(d, 1), w_ada[0], b_ada[0][None, :])
    sh1, sc1, g1, sh2, sc2, g2 = [mod[:, i * d:(i + 1) * d] for i in range(6)]

    q, qb, k, kb, vt, rq, rkt, rv, rg = _in_proj(
        xs, sc1, sh1, w_all, w_t, bf_row, rot_a, rot_b, rot_at, rot_bt, e_mat, ones_q, ones_k)
    fox = _fox_attention(q, qb, k, kb, vt)
    ret = _retention(rq, rkt, rv, rg, inner, xi_b, zeta_b, gc_b)
    out = _mix_ffn(fox, ret, xs, wo[:FOX_WIDTH], wo[FOX_WIDTH:], g1, ln1_g[0][None, :], ln1_b[0][None, :],
                   sc2, sh2, g2, w_up[0].astype(BF16), conv_w[0], conv_b[0][None, :],
                   w_down[0].astype(BF16), ln2_g[0][None, :], ln2_b[0][None, :])
    return out[None]
```

```python
import math

import jax
import jax.numpy as jnp
import numpy as np
from jax import lax
from jax.experimental import pallas as pl
from jax.experimental.pallas import tpu as pltpu

F32 = jnp.float32
BF16 = jnp.bfloat16

D_MODEL = 1024
SEQ = 16384
FOX_HEADS = 8
FOX_HEAD_DIM = 64
FOX_WIDTH = FOX_HEADS * FOX_HEAD_DIM
RET_HEADS = 4
RET_HEAD_DIM = 128
RET_WIDTH = RET_HEADS * RET_HEAD_DIM
D_FF = 2816
CONV_WIDTH = 3
RET_CHUNK = 128
ROPE_BASE = 10000.0
LN_EPS = 1e-5
GN_EPS = 1e-6
DEPTH = 1
ALPHA = (2 * DEPTH) ** 0.25

LANES = 128
SUBLANES = 8
BF16_ROWS = 16
VMEM_LIMIT = 56 * 1024 * 1024

FOX_PAIRS = FOX_HEADS // 2
BIAS_CUM = 0
BIAS_ONE = 3
N_PIECES = 3
ONE_LANE = N_PIECES * FOX_HEADS
PV_ROWS = FOX_HEAD_DIM + BF16_ROWS
NEG_BIG = -1e30
LOG2_E = math.log2(math.e)
RSQRT2 = 2.0 ** -0.5

COL_K = 0
COL_F = COL_K + FOX_WIDTH
COL_RQ = COL_F + LANES
COL_RV = COL_RQ + RET_WIDTH
COL_RG = COL_RV + RET_WIDTH
N_COLS = COL_RG + RET_WIDTH
ROW_QT = 0
ROW_VT = ROW_QT + FOX_WIDTH
ROW_RKT = ROW_VT + FOX_WIDTH
N_ROWS_T = ROW_RKT + RET_WIDTH

TM_PROJ = 512
TQ = 512
TK = 1024
TK_SUB = 256
TR = 512
TM_FFN = 512
FF_CHUNK = 256


def _const_spec(shape):
    return pl.BlockSpec(shape, lambda *_: (0,) * len(shape), pipeline_mode=pl.Buffered(1))


def _split3(x):
    p0 = x.astype(BF16).astype(F32)
    r1 = x - p0
    p1 = r1.astype(BF16).astype(F32)
    p2 = (r1 - p1).astype(BF16).astype(F32)
    return p0, p1, p2


def _layer_norm(y, g, b):
    mu = jnp.mean(y, axis=-1, keepdims=True)
    d = y - mu
    var = jnp.mean(d * d, axis=-1, keepdims=True)
    return d * lax.rsqrt(var + LN_EPS) * g + b


def _mod_kernel(c_ref, w_ref, b_ref, o_ref):
    c = c_ref[...]
    sc = c * jax.nn.sigmoid(c)
    o_ref[...] = jnp.sum(w_ref[...] * sc, axis=0, keepdims=True) + b_ref[...]


def _adaln_mod(c_col, w_ada, b_ada):
    d, n = w_ada.shape
    tn = 1024
    return pl.pallas_call(
        _mod_kernel,
        grid=(n // tn,),
        in_specs=[pl.BlockSpec((d, 1), lambda j: (0, 0)),
                  pl.BlockSpec((d, tn), lambda j: (0, j)),
                  pl.BlockSpec((1, tn), lambda j: (0, j))],
        out_specs=pl.BlockSpec((1, tn), lambda j: (0, j)),
        out_shape=jax.ShapeDtypeStruct((1, n), F32),
        compiler_params=pltpu.CompilerParams(dimension_semantics=("arbitrary",)),
        name="adaln_mod",
    )(c_col, w_ada, b_ada)


def _in_proj_kernel(x_ref, sc_ref, sh_ref, w_ref, wt_ref, bf_ref, ra_ref, rb_ref, rat_ref, rbt_ref,
                    eqt_ref, ek_ref,
                    qt_out, qbt_out, k_out, kb_out, vt_out, rq_out, rkt_out, rv_out, rg_out,
                    carry_ref):
    tm = x_ref.shape[0]

    @pl.when(pl.program_id(0) == 0)
    def _():
        carry_ref[...] = jnp.zeros_like(carry_ref)

    h = (x_ref[...] * (1.0 + sc_ref[...]) + sh_ref[...]).astype(BF16)

    def proj(c0, width):
        return jnp.dot(h, w_ref[:, c0:c0 + width], preferred_element_type=F32)

    lane = lax.broadcasted_iota(jnp.int32, (tm, LANES), 1)
    head_lane = lane < FOX_HEADS
    ff = proj(COL_F, LANES) + bf_ref[...]
    logf = jnp.minimum(ff, 0.0) - jnp.log1p(jnp.exp(-jnp.abs(ff)))
    logf = jnp.where(head_lane, logf, 0.0)
    p0, p1, p2 = _split3(logf)
    pieces = p0 + pltpu.roll(p1, FOX_HEADS, 1) + pltpu.roll(p2, 2 * FOX_HEADS, 1)
    row = lax.broadcasted_iota(jnp.int32, (tm, tm), 0)
    col = lax.broadcasted_iota(jnp.int32, (tm, tm), 1)
    tril = jnp.where(row >= col, 1.0, 0.0).astype(BF16)
    c3 = jnp.dot(tril, pieces.astype(BF16), preferred_element_type=F32)
    cum = c3 + pltpu.roll(c3, LANES - FOX_HEADS, 1) + pltpu.roll(c3, LANES - 2 * FOX_HEADS, 1)
    cum = jnp.where(head_lane, cum, 0.0) + carry_ref[0:1, :]
    carry_ref[0:1, :] = cum[tm - 1:tm, :]
    c0_, c1_, c2_ = _split3(cum * LOG2_E)
    cum_pieces = c0_ + pltpu.roll(c1_, FOX_HEADS, 1) + pltpu.roll(c2_, 2 * FOX_HEADS, 1)
    cum_pieces = jnp.where(lane == ONE_LANE, 1.0, cum_pieces)
    kb_out[...] = jnp.dot(cum_pieces.astype(BF16), ek_ref[...], preferred_element_type=F32).astype(BF16)
    qbt_out[...] = jnp.dot(eqt_ref[...], cum_pieces.T.astype(BF16),
                           preferred_element_type=F32).astype(BF16)

    k_out[...] = proj(COL_K, FOX_WIDTH).astype(BF16)
    tp = lax.dot_general(wt_ref[...], h, (((1,), (1,)), ((), ())), preferred_element_type=F32)
    qt_out[...] = (tp[ROW_QT:ROW_QT + FOX_WIDTH] * (FOX_HEAD_DIM ** -0.5 * LOG2_E)).astype(BF16)
    vt_out[...] = tp[ROW_VT:ROW_VT + FOX_WIDTH].astype(BF16)

    half = RET_HEAD_DIM // 2
    ca, sa = ra_ref[0, 0:1, :], ra_ref[0, 1:2, :]
    cos = ca * rb_ref[0] - sa * rb_ref[1]
    sin = sa * rb_ref[2] + ca * rb_ref[3]
    cat, sat = rat_ref[0, :, 0:1], rat_ref[0, :, 1:2]
    cost = cat * rbt_ref[0] - sat * rbt_ref[1]
    sint = sat * rbt_ref[2] + cat * rbt_ref[3]
    kscale = RET_HEAD_DIM ** -0.5
    two = 2 * RET_HEAD_DIM
    for pair in range(RET_HEADS // 2):
        a2 = proj(COL_RQ + pair * two, two)
        for e in range(2):
            lo = pair * two + e * RET_HEAD_DIM
            a = a2[:, e * RET_HEAD_DIM:(e + 1) * RET_HEAD_DIM]
            rq_out[:, lo:lo + RET_HEAD_DIM] = (a * cos + pltpu.roll(a, half, 1) * sin).astype(BF16)
    for hh in range(RET_HEADS):
        lo = hh * RET_HEAD_DIM
        b = tp[ROW_RKT + lo:ROW_RKT + lo + RET_HEAD_DIM]
        b_rot = jnp.concatenate([b[half:], b[:half]], axis=0)
        rkt_out[lo:lo + RET_HEAD_DIM, :] = ((b * cost + b_rot * sint) * kscale).astype(BF16)
    rv_out[...] = proj(COL_RV, RET_WIDTH).astype(BF16)
    rg_out[...] = proj(COL_RG, RET_WIDTH).astype(BF16)


def _in_proj(x, sc1, sh1, w_all, w_t, bf_row, rot_a, rot_b, rot_at, rot_bt, eq_t, e_k):
    s, d = x.shape
    tm = TM_PROJ
    row = lambda w: pl.BlockSpec((tm, w), lambda i: (i, 0))
    col = lambda h: pl.BlockSpec((h, tm), lambda i: (0, i))
    out_shapes = ([jax.ShapeDtypeStruct((FOX_WIDTH, s), BF16)] * 2
                  + [jax.ShapeDtypeStruct((s, FOX_WIDTH), BF16)] * 2
                  + [jax.ShapeDtypeStruct((FOX_WIDTH, s), BF16),
                     jax.ShapeDtypeStruct((s, RET_WIDTH), BF16),
                     jax.ShapeDtypeStruct((RET_WIDTH, s), BF16)]
                  + [jax.ShapeDtypeStruct((s, RET_WIDTH), BF16)] * 2)
    return pl.pallas_call(
        _in_proj_kernel,
        grid=(s // tm,),
        in_specs=[row(d), _const_spec((1, d)), _const_spec((1, d)),
                  _const_spec((d, N_COLS)), _const_spec((N_ROWS_T, d)), _const_spec((1, LANES)),
                  pl.BlockSpec((1, 2, RET_HEAD_DIM), lambda i: (i, 0, 0)),
                  _const_spec((4, tm, RET_HEAD_DIM)),
                  pl.BlockSpec((1, RET_HEAD_DIM, 2), lambda i: (i, 0, 0)),
                  _const_spec((4, RET_HEAD_DIM, tm)),
                  _const_spec((FOX_WIDTH, LANES)), _const_spec((LANES, FOX_WIDTH))],
        out_specs=([col(FOX_WIDTH)] * 2 + [row(FOX_WIDTH)] * 2
                   + [col(FOX_WIDTH), row(RET_WIDTH), col(RET_WIDTH), row(RET_WIDTH), row(RET_WIDTH)]),
        out_shape=out_shapes,
        scratch_shapes=[pltpu.VMEM((SUBLANES, LANES), F32)],
        compiler_params=pltpu.CompilerParams(dimension_semantics=("arbitrary",),
                                             vmem_limit_bytes=VMEM_LIMIT),
        name="in_proj",
    )(x, sc1, sh1, w_all, w_t, bf_row, rot_a, rot_b, rot_at, rot_bt, eq_t, e_k)


def _fox_kernel(qt_ref, qbt_ref, k_ref, kb_ref, vt_ref, o_ref, kaug_ref, m_ref, acc_ref,
                sa_ref, ca_ref, sb_ref, cb_ref):
    i = pl.program_id(1)
    lane = lax.broadcasted_iota(jnp.int32, (1, LANES), 1)
    dim = lax.broadcasted_iota(jnp.int32, (LANES, 1), 0)
    own_lanes = (lane < FOX_HEAD_DIM, lane >= FOX_HEAD_DIM)
    own_dims = (dim < FOX_HEAD_DIM, dim >= FOX_HEAD_DIM)

    @pl.when(i == 0)
    def _():
        for e in range(2):
            kaug_ref[e] = jnp.where(own_lanes[e], k_ref[...], kb_ref[...])

    qt = [jnp.where(own_dims[e], qt_ref[...], qbt_ref[...]) for e in range(2)]
    m_ref[...] = jnp.full_like(m_ref, NEG_BIG)
    acc_ref[...] = jnp.zeros_like(acc_ref)
    n_sub = TK // TK_SUB

    def qk(e, start):
        return jnp.dot(kaug_ref[e, pl.ds(start, TK_SUB), :], qt[e], preferred_element_type=F32)

    def values(e, start, width):
        return jnp.concatenate(
            [vt_ref[e * FOX_HEAD_DIM:(e + 1) * FOX_HEAD_DIM, pl.ds(start, width)],
             jnp.ones((BF16_ROWS, width), BF16)], axis=0)

    def pv(e, st, m_new, start, width):
        pt = jnp.exp2(st - m_new).astype(BF16)
        return jnp.dot(values(e, start, width), pt, preferred_element_type=F32)

    def scores(j, e, s_ref, c_ref):
        cmax = None
        for sub in range(n_sub):
            st = qk(e, pl.multiple_of(j * TK + sub * TK_SUB, TK_SUB))
            s_ref[e, sub * TK_SUB:(sub + 1) * TK_SUB, :] = st
            cm = jnp.max(st, axis=0, keepdims=True)
            cmax = cm if cmax is None else jnp.maximum(cmax, cm)
        c_ref[e] = cmax

    def stage(e, j_next, buf_next, j_cur, buf_cur):
        (s_next, c_next), (s_cur, c_cur) = buf_next, buf_cur
        m_prev = m_ref[e]
        m_new = jnp.maximum(m_prev, c_cur[e])
        acc = jnp.exp2(m_prev - m_new) * acc_ref[e]
        cmax = None
        for sub in range(n_sub):
            rows = slice(sub * TK_SUB, (sub + 1) * TK_SUB)
            st = qk(e, pl.multiple_of(j_next * TK + sub * TK_SUB, TK_SUB))
            s_next[e, rows, :] = st
            cm = jnp.max(st, axis=0, keepdims=True)
            cmax = cm if cmax is None else jnp.maximum(cmax, cm)
            acc = acc + pv(e, s_cur[e, rows, :], m_new,
                           pl.multiple_of(j_cur * TK + sub * TK_SUB, TK_SUB), TK_SUB)
        c_next[e] = cmax
        acc_ref[e] = acc
        m_ref[e] = m_new

    def finish(j, s_ref):
        n_pieces = TK // TQ
        tri = (lax.broadcasted_iota(jnp.int32, (TQ, TQ), 0)
               <= lax.broadcasted_iota(jnp.int32, (TQ, TQ), 1))
        for diag in range(n_pieces):
            @pl.when(i % n_pieces == diag)
            def _():
                for e in range(2):
                    for piece in range(diag + 1):
                        st = s_ref[e, piece * TQ:(piece + 1) * TQ, :]
                        if piece == diag:
                            st = jnp.where(tri, st, NEG_BIG)
                        m_prev = m_ref[e]
                        m_new = jnp.maximum(m_prev, jnp.max(st, axis=0, keepdims=True))
                        start = pl.multiple_of(j * TK + piece * TQ, TQ)
                        acc_ref[e] = (jnp.exp2(m_prev - m_new) * acc_ref[e]
                                      + pv(e, st, m_new, start, TQ))
                        m_ref[e] = m_new

    buf_a = (sa_ref, ca_ref)
    buf_b = (sb_ref, cb_ref)

    nfull = (i * TQ) // TK
    for e in range(2):
        scores(0, e, *buf_a)

    def pair(jj, carry):
        j = 2 * jj
        for e in range(2):
            stage(e, j + 1, buf_b, j, buf_a)
        for e in range(2):
            stage(e, j + 2, buf_a, j + 1, buf_b)
        return carry

    def quad(jq, carry):
        return pair(2 * jq + 1, pair(2 * jq, carry))

    lax.fori_loop(0, nfull // 4, quad, 0)
    lax.fori_loop(2 * (nfull // 4), nfull // 2, pair, 0)

    @pl.when(nfull % 2 == 1)
    def _():
        for e in range(2):
            stage(e, nfull, buf_b, nfull - 1, buf_a)
        finish(nfull, sb_ref)

    @pl.when(nfull % 2 == 0)
    def _():
        finish(nfull, sa_ref)

    outs = []
    for e in range(2):
        acc = acc_ref[e]
        outs.append(acc[:FOX_HEAD_DIM] / acc[FOX_HEAD_DIM:FOX_HEAD_DIM + 1])
    o_ref[...] = jnp.concatenate(outs, axis=0).T.astype(o_ref.dtype)


def _fox_attention(qt, qbt, k, kb, vt):
    s = k.shape[0]
    assert TK % TQ == 0 and TK % TK_SUB == 0 and s % TK == 0
    blk_t = pl.BlockSpec((LANES, TQ), lambda p, i: (p, i))
    res = pl.BlockSpec((s, LANES), lambda p, i: (0, p), pipeline_mode=pl.Buffered(1))
    return pl.pallas_call(
        _fox_kernel,
        grid=(FOX_PAIRS, s // TQ),
        in_specs=[blk_t, blk_t, res, res,
                  pl.BlockSpec((LANES, s), lambda p, i: (p, 0), pipeline_mode=pl.Buffered(1))],
        out_specs=pl.BlockSpec((TQ, LANES), lambda p, i: (i, p)),
        out_shape=jax.ShapeDtypeStruct((s, FOX_WIDTH), BF16),
        scratch_shapes=[pltpu.VMEM((2, s, LANES), BF16),
                        pltpu.VMEM((2, 1, TQ), F32), pltpu.VMEM((2, PV_ROWS, TQ), F32),
                        pltpu.VMEM((2, TK, TQ), F32), pltpu.VMEM((2, 1, TQ), F32),
                        pltpu.VMEM((2, TK, TQ), F32), pltpu.VMEM((2, 1, TQ), F32)],
        compiler_params=pltpu.CompilerParams(dimension_semantics=("arbitrary", "arbitrary"),
                                             vmem_limit_bytes=VMEM_LIMIT),
        name="fox_attention",
    )(qt, qbt, k, kb, vt)


def _ret_kernel(q_ref, kt_ref, v_ref, g_ref, inner_ref, xi_ref, zeta_ref, gc_ref, o_ref, r_ref):
    @pl.when(pl.program_id(0) == 0)
    def _():
        r_ref[...] = jnp.zeros_like(r_ref)

    c = RET_CHUNK
    n_chunks = q_ref.shape[0] // c
    tiles = [(hh, ci) for hh in range(RET_HEADS) for ci in range(n_chunks)]

    def head_cols(hh):
        return slice(hh * RET_HEAD_DIM, (hh + 1) * RET_HEAD_DIM)

    def chunk_rows(ci):
        return slice(ci * c, (ci + 1) * c)

    intra, kv = {}, {}
    for hh, ci in tiles:
        q = q_ref[chunk_rows(ci), head_cols(hh)]
        kt = kt_ref[head_cols(hh), chunk_rows(ci)]
        v = v_ref[chunk_rows(ci), head_cols(hh)]
        intra[hh, ci] = (jnp.dot(q, kt, preferred_element_type=F32) * inner_ref[hh]).astype(BF16)
        kzt = (kt.astype(F32) * zeta_ref[hh]).astype(BF16)
        kv[hh, ci] = jnp.dot(kzt, v, preferred_element_type=F32)

    state = {}
    for hh in range(RET_HEADS):
        r_state = r_ref[hh]
        for ci in range(n_chunks):
            state[hh, ci] = r_state.astype(BF16)
            r_state = r_state * gc_ref[hh] + kv[hh, ci]
        r_ref[hh] = r_state

    for hh, ci in tiles:
        q = q_ref[chunk_rows(ci), head_cols(hh)]
        v = v_ref[chunk_rows(ci), head_cols(hh)]
        o = (jnp.dot(intra[hh, ci], v, preferred_element_type=F32)
             + jnp.dot(q, state[hh, ci], preferred_element_type=F32) * xi_ref[hh])
        mu = jnp.mean(o, axis=-1, keepdims=True)
        d = o - mu
        var = jnp.mean(d * d, axis=-1, keepdims=True)
        on = d * lax.rsqrt(var + GN_EPS)
        g = g_ref[chunk_rows(ci), head_cols(hh)].astype(F32)
        o_ref[chunk_rows(ci), head_cols(hh)] = (g * jax.nn.sigmoid(g) * on).astype(o_ref.dtype)


def _retention(rq, rkt, rv, rg, inner, xi_b, zeta_b, gc_b):
    s = rq.shape[0]
    blk = pl.BlockSpec((TR, RET_WIDTH), lambda i: (i, 0))
    blk_t = pl.BlockSpec((RET_WIDTH, TR), lambda i: (0, i))
    tab = _const_spec((RET_HEADS, RET_CHUNK, RET_CHUNK))
    return pl.pallas_call(
        _ret_kernel,
        grid=(s // TR,),
        in_specs=[blk, blk_t, blk, blk, tab, tab, tab, _const_spec((RET_HEADS, 1, RET_HEAD_DIM))],
        out_specs=blk,
        out_shape=jax.ShapeDtypeStruct((s, RET_WIDTH), BF16),
        scratch_shapes=[pltpu.VMEM((RET_HEADS, RET_HEAD_DIM, RET_HEAD_DIM), F32)],
        compiler_params=pltpu.CompilerParams(dimension_semantics=("arbitrary",)),
        name="retention",
    )(rq, rkt, rv, rg, inner, xi_b, zeta_b, gc_b)


def _mix_ffn_kernel(fox_ref, ret_ref, x_ref, fox0_ref, ret0_ref, x0_ref,
                    wf_ref, wr_ref, g1_ref, l1g_ref, l1b_ref,
                    sc_ref, sh_ref, g2_ref, wu_ref, cw_ref, cb_ref, wd_ref, l2g_ref, l2b_ref,
                    o_ref, carry_ref, slab_in_ref, slab_out_ref, y_ref, x1_ref):
    tm = x_ref.shape[0]
    grp = tm // SUBLANES
    n_slab = D_MODEL // LANES

    def front(fox, ret, x):
        mix = (jnp.dot(fox, wf_ref[...], preferred_element_type=F32)
               + jnp.dot(ret, wr_ref[...], preferred_element_type=F32))
        x1_nat = _layer_norm(ALPHA * x + g1_ref[...] * mix, l1g_ref[...], l1b_ref[...])
        for c in range(n_slab):
            slab_in_ref[c] = x1_nat[:, c * LANES:(c + 1) * LANES]
        return jnp.concatenate(
            [jnp.concatenate([slab_in_ref[c, pl.ds(v, SUBLANES, stride=grp), :] for v in range(grp)],
                             axis=0)
             for c in range(n_slab)], axis=1)

    @pl.when(pl.program_id(0) == 0)
    def _():
        carry_ref[...] = jnp.zeros_like(carry_ref)
        x1_ref[...] = front(fox0_ref[...], ret0_ref[...], x0_ref[...])

    h = (x1_ref[...] * (1.0 + sc_ref[...]) + sh_ref[...]).astype(BF16)
    first_sublane = lax.broadcasted_iota(jnp.int32, (SUBLANES, FF_CHUNK), 0) == 0

    def conv_up(c0):
        cols = slice(c0, c0 + FF_CHUNK)
        up = jnp.dot(h, wu_ref[:, cols], preferred_element_type=F32)
        prev = carry_ref[:, cols]
        tail = []
        for g in range(2):
            rows = slice(tm - (2 - g) * SUBLANES, tm - (1 - g) * SUBLANES)
            tail.append(jnp.where(first_sublane,
                                  pltpu.roll(prev[g * SUBLANES:(g + 1) * SUBLANES], 1, 0),
                                  pltpu.roll(up[rows], 1, 0)))
        carry_ref[:, cols] = up[tm - 2 * SUBLANES:tm]
        back1 = jnp.concatenate([tail[1], up[:tm - SUBLANES]], axis=0)
        back2 = jnp.concatenate([tail[0], tail[1], up[:tm - 2 * SUBLANES]], axis=0)
        cw = cw_ref[:, cols] * RSQRT2
        return cb_ref[:, cols] * RSQRT2 + back2 * cw[0:1] + back1 * cw[1:2] + up * cw[2:3]

    for ci in range(D_FF // FF_CHUNK):
        a = conv_up(ci * FF_CHUNK)
        b = conv_up(D_FF + ci * FF_CHUNK)
        y_ref[:, ci * FF_CHUNK:(ci + 1) * FF_CHUNK] = (a * (1.0 + lax.erf(a)) * b).astype(BF16)

    x1_next = front(fox_ref[...], ret_ref[...], x_ref[...])

    ffn = jnp.dot(y_ref[...], wd_ref[...], preferred_element_type=F32)
    out = _layer_norm(ALPHA * x1_ref[...] + g2_ref[...] * ffn, l2g_ref[...], l2b_ref[...])
    x1_ref[...] = x1_next

    for c in range(n_slab):
        for v in range(grp):
            slab_out_ref[c, pl.ds(v, SUBLANES, stride=grp), :] = out[v * SUBLANES:(v + 1) * SUBLANES,
                                                                     c * LANES:(c + 1) * LANES]
    for c in range(n_slab):
        o_ref[:, c * LANES:(c + 1) * LANES] = slab_out_ref[c]


def _mix_ffn(fox, ret, x, w_fox, w_ret, g1, ln1_g, ln1_b, sc2, sh2, g2, w_up, conv_w, conv_b, w_down,
             ln2_g, ln2_b):
    s, d = x.shape
    tm = TM_FFN
    n_tiles = s // tm
    nxt = lambda w: pl.BlockSpec((tm, w), lambda i: (jnp.minimum(i + 1, n_tiles - 1), 0))
    vec = _const_spec((1, d))
    slab = pltpu.VMEM((d // LANES, tm, LANES), F32)
    return pl.pallas_call(
        _mix_ffn_kernel,
        grid=(n_tiles,),
        in_specs=[nxt(FOX_WIDTH), nxt(RET_WIDTH), nxt(d),
                  _const_spec((tm, FOX_WIDTH)), _const_spec((tm, RET_WIDTH)), _const_spec((tm, d)),
                  _const_spec((FOX_WIDTH, d)), _const_spec((RET_WIDTH, d)), vec, vec, vec,
                  vec, vec, vec,
                  _const_spec((d, 2 * D_FF)), _const_spec((CONV_WIDTH, 2 * D_FF)),
                  _const_spec((1, 2 * D_FF)), _const_spec((D_FF, d)), vec, vec],
        out_specs=pl.BlockSpec((tm, d), lambda i: (i, 0)),
        out_shape=jax.ShapeDtypeStruct((s, d), F32),
        scratch_shapes=[pltpu.VMEM((2 * SUBLANES, 2 * D_FF), F32), slab, slab,
                        pltpu.VMEM((tm, D_FF), BF16), pltpu.VMEM((tm, d), F32)],
        compiler_params=pltpu.CompilerParams(dimension_semantics=("arbitrary",),
                                             vmem_limit_bytes=VMEM_LIMIT),
        name="mix_ffn",
    )(fox, ret, x, fox, ret, x, w_fox, w_ret, g1, ln1_g, ln1_b, sc2, sh2, g2, w_up, conv_w, conv_b,
      w_down, ln2_g, ln2_b)


def _placement_tables():
    e_q = np.zeros((LANES, FOX_WIDTH), np.float32)
    e_k = np.zeros((LANES, FOX_WIDTH), np.float32)
    for h in range(FOX_HEADS):
        base = (h // 2) * LANES + (FOX_HEAD_DIM if h % 2 == 0 else 0)
        for j in range(N_PIECES):
            e_q[j * FOX_HEADS + h, base + BIAS_CUM + j] = 1.0
            e_q[ONE_LANE, base + BIAS_ONE + j] = 1.0
            e_k[ONE_LANE, base + BIAS_CUM + j] = 1.0
            e_k[j * FOX_HEADS + h, base + BIAS_ONE + j] = -1.0
    return jnp.asarray(e_q.T, BF16), jnp.asarray(e_k, BF16)


def _rotation_tables(s, tm):
    dk = RET_HEAD_DIM
    inv_freq = ROPE_BASE ** (-np.arange(0, dk, 2, dtype=np.float64) / dk)
    inv_freq = np.concatenate([inv_freq, inv_freq])
    sign = np.concatenate([-np.ones(dk // 2), np.ones(dk // 2)])
    ang_a = (np.arange(s // tm, dtype=np.float64) * tm)[:, None] * inv_freq[None, :]
    ang_b = np.arange(tm, dtype=np.float64)[:, None] * inv_freq[None, :]
    rot_a = np.stack([np.cos(ang_a), np.sin(ang_a)], axis=1)
    rot_b = np.stack([np.cos(ang_b), np.sin(ang_b), sign * np.cos(ang_b), sign * np.sin(ang_b)])
    f32 = lambda a: jnp.asarray(a, F32)
    return f32(rot_a), f32(rot_b), f32(rot_a.transpose(0, 2, 1)), f32(rot_b.transpose(0, 2, 1))


def _decay_tables():
    c = RET_CHUNK
    log_gamma = np.log1p(-np.exp2(-5.0 - np.arange(RET_HEADS, dtype=np.float64)))
    idx = np.arange(c, dtype=np.float64)
    diff = idx[:, None] - idx[None, :]
    inner = np.where(diff[None] >= 0,
                     np.exp(np.maximum(diff, 0.0)[None] * log_gamma[:, None, None]), 0.0)
    xi = np.exp((idx[None, :] + 1.0) * log_gamma[:, None])
    zeta = np.exp((c - 1.0 - idx[None, :]) * log_gamma[:, None])
    g_chunk = np.exp(c * log_gamma)
    bshape = (RET_HEADS, c, RET_HEAD_DIM)
    f32 = lambda a: jnp.asarray(a, F32)
    return (f32(inner),
            f32(np.broadcast_to(xi[:, :, None], bshape)),
            f32(np.broadcast_to(zeta[:, None, :], bshape)),
            f32(np.broadcast_to(g_chunk[:, None, None], (RET_HEADS, 1, RET_HEAD_DIM))))


def kernel(x, c, w_ada, b_ada, w_in, b_f, w_out, ln1_g, ln1_b, w_up, conv_w, conv_b, w_down, ln2_g, ln2_b):
    b, s, d = x.shape
    assert (b, s, d) == (1, SEQ, D_MODEL) and w_ada.shape[0] == DEPTH
    xs = x[0]

    wi = w_in[0]
    o = 0
    parts = {}
    for name, width in (("fq", FOX_WIDTH), ("fk", FOX_WIDTH), ("fv", FOX_WIDTH), ("ff", FOX_HEADS),
                        ("rq", RET_WIDTH), ("rk", RET_WIDTH), ("rv", RET_WIDTH), ("rg", RET_WIDTH)):
        parts[name] = wi[:, o:o + width]
        o += width
    w_all = jnp.concatenate(
        [parts["fk"], jnp.pad(parts["ff"], ((0, 0), (0, LANES - FOX_HEADS))),
         parts["rq"], parts["rv"], parts["rg"]], axis=1).astype(BF16)
    w_t = jnp.concatenate([parts["fq"], parts["fv"], parts["rk"]], axis=1).T.astype(BF16)
    bf_row = jnp.pad(b_f[0][None, :], ((0, 0), (0, LANES - FOX_HEADS)))
    wo = w_out[0].astype(BF16)

    eq_t, e_k = _placement_tables()
    rot_a, rot_b, rot_at, rot_bt = _rotation_tables(s, TM_PROJ)
    inner, xi_b, zeta_b, gc_b = _decay_tables()

    mod = _adaln_mod(c.reshape(d, 1), w_ada[0], b_ada[0][None, :])
    sh1, sc1, g1, sh2, sc2, g2 = [mod[:, i * d:(i + 1) * d] for i in range(6)]

    qt, qbt, k, kb, vt, rq, rkt, rv, rg = _in_proj(
        xs, sc1, sh1, w_all, w_t, bf_row, rot_a, rot_b, rot_at, rot_bt, eq_t, e_k)
    fox = _fox_attention(qt, qbt, k, kb, vt)
    ret = _retention(rq, rkt, rv, rg, inner, xi_b, zeta_b, gc_b)
    out = _mix_ffn(fox, ret, xs, wo[:FOX_WIDTH], wo[FOX_WIDTH:], g1, ln1_g[0][None, :], ln1_b[0][None, :],
                   sc2, sh2, g2, w_up[0].astype(BF16), conv_w[0], conv_b[0][None, :],
                   w_down[0].astype(BF16), ln2_g[0][None, :], ln2_b[0][None, :])
    return out[None]
```

```python
import math

import jax
import jax.numpy as jnp
import numpy as np
from jax import lax
from jax.experimental import pallas as pl
from jax.experimental.pallas import tpu as pltpu

F32 = jnp.float32
BF16 = jnp.bfloat16

D_MODEL = 1024
SEQ = 16384
FOX_HEADS = 8
FOX_HEAD_DIM = 64
FOX_WIDTH = FOX_HEADS * FOX_HEAD_DIM
RET_HEADS = 4
RET_HEAD_DIM = 128
RET_WIDTH = RET_HEADS * RET_HEAD_DIM
D_FF = 2816
CONV_WIDTH = 3
RET_CHUNK = 128
ROPE_BASE = 10000.0
LN_EPS = 1e-5
GN_EPS = 1e-6
DEPTH = 1
ALPHA = (2 * DEPTH) ** 0.25

LANES = 128
SUBLANES = 8
BF16_ROWS = 16
VMEM_LIMIT = 56 * 1024 * 1024

FOX_PAIRS = FOX_HEADS // 2
BIAS_CUM = 0
BIAS_ONE = 3
N_PIECES = 3
ONE_LANE = N_PIECES * FOX_HEADS
PV_ROWS = FOX_HEAD_DIM + BF16_ROWS
NEG_BIG = -1e30
LOG2_E = math.log2(math.e)
RSQRT2 = 2.0 ** -0.5

COL_K = 0
COL_F = COL_K + FOX_WIDTH
COL_RQ = COL_F + LANES
COL_RV = COL_RQ + RET_WIDTH
COL_RG = COL_RV + RET_WIDTH
N_COLS = COL_RG + RET_WIDTH
ROW_QT = 0
ROW_VT = ROW_QT + FOX_WIDTH
ROW_RKT = ROW_VT + FOX_WIDTH
N_ROWS_T = ROW_RKT + RET_WIDTH

TM_PROJ = 512
TQ = 512
TK = 1024
TK_SUB = 256
PAIRS_PER_TRIP = 2
TR = 512
TM_FFN = 512
FF_CHUNK = 256


def _const_spec(shape):
    return pl.BlockSpec(shape, lambda *_: (0,) * len(shape), pipeline_mode=pl.Buffered(1))


def _split3(x):
    p0 = x.astype(BF16).astype(F32)
    r1 = x - p0
    p1 = r1.astype(BF16).astype(F32)
    p2 = (r1 - p1).astype(BF16).astype(F32)
    return p0, p1, p2


def _layer_norm(y, g, b):
    mu = jnp.mean(y, axis=-1, keepdims=True)
    d = y - mu
    var = jnp.mean(d * d, axis=-1, keepdims=True)
    return d * lax.rsqrt(var + LN_EPS) * g + b


def _mod_kernel(c_ref, w_ref, b_ref, o_ref):
    c = c_ref[...]
    sc = c * jax.nn.sigmoid(c)
    o_ref[...] = jnp.sum(w_ref[...] * sc, axis=0, keepdims=True) + b_ref[...]


def _adaln_mod(c_col, w_ada, b_ada):
    d, n = w_ada.shape
    tn = 1024
    return pl.pallas_call(
        _mod_kernel,
        grid=(n // tn,),
        in_specs=[pl.BlockSpec((d, 1), lambda j: (0, 0)),
                  pl.BlockSpec((d, tn), lambda j: (0, j)),
                  pl.BlockSpec((1, tn), lambda j: (0, j))],
        out_specs=pl.BlockSpec((1, tn), lambda j: (0, j)),
        out_shape=jax.ShapeDtypeStruct((1, n), F32),
        compiler_params=pltpu.CompilerParams(dimension_semantics=("arbitrary",)),
        name="adaln_mod",
    )(c_col, w_ada, b_ada)


def _in_proj_kernel(x_ref, sc_ref, sh_ref, w_ref, wt_ref, bf_ref, ra_ref, rb_ref, rat_ref, rbt_ref,
                    eqt_ref, ek_ref,
                    qt_out, qbt_out, k_out, kb_out, vt_out, rq_out, rkt_out, rv_out, rg_out,
                    carry_ref):
    tm = x_ref.shape[0]

    @pl.when(pl.program_id(0) == 0)
    def _():
        carry_ref[...] = jnp.zeros_like(carry_ref)

    h = (x_ref[...] * (1.0 + sc_ref[...]) + sh_ref[...]).astype(BF16)

    def proj(c0, width):
        return jnp.dot(h, w_ref[:, c0:c0 + width], preferred_element_type=F32)

    lane = lax.broadcasted_iota(jnp.int32, (tm, LANES), 1)
    head_lane = lane < FOX_HEADS
    ff = proj(COL_F, LANES) + bf_ref[...]
    k_out[...] = proj(COL_K, FOX_WIDTH).astype(BF16)
    logf = jnp.minimum(ff, 0.0) - jnp.log1p(jnp.exp(-jnp.abs(ff)))
    logf = jnp.where(head_lane, logf, 0.0)
    p0, p1, p2 = _split3(logf)
    pieces = p0 + pltpu.roll(p1, FOX_HEADS, 1) + pltpu.roll(p2, 2 * FOX_HEADS, 1)
    row = lax.broadcasted_iota(jnp.int32, (tm, tm), 0)
    col = lax.broadcasted_iota(jnp.int32, (tm, tm), 1)
    tril = jnp.where(row >= col, 1.0, 0.0).astype(BF16)
    c3 = jnp.dot(tril, pieces.astype(BF16), preferred_element_type=F32)
    tp = lax.dot_general(wt_ref[...], h, (((1,), (1,)), ((), ())), preferred_element_type=F32)
    cum = c3 + pltpu.roll(c3, LANES - FOX_HEADS, 1) + pltpu.roll(c3, LANES - 2 * FOX_HEADS, 1)
    cum = jnp.where(head_lane, cum, 0.0) + carry_ref[0:1, :]
    carry_ref[0:1, :] = cum[tm - 1:tm, :]
    c0_, c1_, c2_ = _split3(cum * LOG2_E)
    cum_pieces = c0_ + pltpu.roll(c1_, FOX_HEADS, 1) + pltpu.roll(c2_, 2 * FOX_HEADS, 1)
    cum_pieces = jnp.where(lane == ONE_LANE, 1.0, cum_pieces)
    kb_out[...] = jnp.dot(cum_pieces.astype(BF16), ek_ref[...], preferred_element_type=F32).astype(BF16)
    qbt_out[...] = jnp.dot(eqt_ref[...], cum_pieces.T.astype(BF16),
                           preferred_element_type=F32).astype(BF16)

    qt_out[...] = (tp[ROW_QT:ROW_QT + FOX_WIDTH] * (FOX_HEAD_DIM ** -0.5 * LOG2_E)).astype(BF16)
    vt_out[...] = tp[ROW_VT:ROW_VT + FOX_WIDTH].astype(BF16)

    half = RET_HEAD_DIM // 2
    ca, sa = ra_ref[0, 0:1, :], ra_ref[0, 1:2, :]
    cos = ca * rb_ref[0] - sa * rb_ref[1]
    sin = sa * rb_ref[2] + ca * rb_ref[3]
    cat, sat = rat_ref[0, :, 0:1], rat_ref[0, :, 1:2]
    cost = cat * rbt_ref[0] - sat * rbt_ref[1]
    sint = sat * rbt_ref[2] + cat * rbt_ref[3]
    kscale = RET_HEAD_DIM ** -0.5
    two = 2 * RET_HEAD_DIM
    for pair in range(RET_HEADS // 2):
        a2 = proj(COL_RQ + pair * two, two)
        for e in range(2):
            lo = pair * two + e * RET_HEAD_DIM
            a = a2[:, e * RET_HEAD_DIM:(e + 1) * RET_HEAD_DIM]
            rq_out[:, lo:lo + RET_HEAD_DIM] = (a * cos + pltpu.roll(a, half, 1) * sin).astype(BF16)
    for hh in range(RET_HEADS):
        lo = hh * RET_HEAD_DIM
        b = tp[ROW_RKT + lo:ROW_RKT + lo + RET_HEAD_DIM]
        b_rot = jnp.concatenate([b[half:], b[:half]], axis=0)
        rkt_out[lo:lo + RET_HEAD_DIM, :] = ((b * cost + b_rot * sint) * kscale).astype(BF16)
    rv_out[...] = proj(COL_RV, RET_WIDTH).astype(BF16)
    rg_out[...] = proj(COL_RG, RET_WIDTH).astype(BF16)


def _in_proj(x, sc1, sh1, w_all, w_t, bf_row, rot_a, rot_b, rot_at, rot_bt, eq_t, e_k):
    s, d = x.shape
    tm = TM_PROJ
    row = lambda w: pl.BlockSpec((tm, w), lambda i: (i, 0))
    col = lambda h: pl.BlockSpec((h, tm), lambda i: (0, i))
    out_shapes = ([jax.ShapeDtypeStruct((FOX_WIDTH, s), BF16)] * 2
                  + [jax.ShapeDtypeStruct((s, FOX_WIDTH), BF16)] * 2
                  + [jax.ShapeDtypeStruct((FOX_WIDTH, s), BF16),
                     jax.ShapeDtypeStruct((s, RET_WIDTH), BF16),
                     jax.ShapeDtypeStruct((RET_WIDTH, s), BF16)]
                  + [jax.ShapeDtypeStruct((s, RET_WIDTH), BF16)] * 2)
    return pl.pallas_call(
        _in_proj_kernel,
        grid=(s // tm,),
        in_specs=[row(d), _const_spec((1, d)), _const_spec((1, d)),
                  _const_spec((d, N_COLS)), _const_spec((N_ROWS_T, d)), _const_spec((1, LANES)),
                  pl.BlockSpec((1, 2, RET_HEAD_DIM), lambda i: (i, 0, 0)),
                  _const_spec((4, tm, RET_HEAD_DIM)),
                  pl.BlockSpec((1, RET_HEAD_DIM, 2), lambda i: (i, 0, 0)),
                  _const_spec((4, RET_HEAD_DIM, tm)),
                  _const_spec((FOX_WIDTH, LANES)), _const_spec((LANES, FOX_WIDTH))],
        out_specs=([col(FOX_WIDTH)] * 2 + [row(FOX_WIDTH)] * 2
                   + [col(FOX_WIDTH), row(RET_WIDTH), col(RET_WIDTH), row(RET_WIDTH), row(RET_WIDTH)]),
        out_shape=out_shapes,
        scratch_shapes=[pltpu.VMEM((SUBLANES, LANES), F32)],
        compiler_params=pltpu.CompilerParams(dimension_semantics=("arbitrary",),
                                             vmem_limit_bytes=VMEM_LIMIT),
        name="in_proj",
    )(x, sc1, sh1, w_all, w_t, bf_row, rot_a, rot_b, rot_at, rot_bt, eq_t, e_k)


def _fox_kernel(qt_ref, qbt_ref, k_ref, kb_ref, vt_ref, o_ref, kaug_ref, m_ref, acc_ref,
                sa_ref, ca_ref, sb_ref, cb_ref):
    i = pl.program_id(1)
    lane = lax.broadcasted_iota(jnp.int32, (1, LANES), 1)
    dim = lax.broadcasted_iota(jnp.int32, (LANES, 1), 0)
    own_lanes = (lane < FOX_HEAD_DIM, lane >= FOX_HEAD_DIM)
    own_dims = (dim < FOX_HEAD_DIM, dim >= FOX_HEAD_DIM)

    @pl.when(i == 0)
    def _():
        for e in range(2):
            kaug_ref[e] = jnp.where(own_lanes[e], k_ref[...], kb_ref[...])

    qt = [jnp.where(own_dims[e], qt_ref[...], qbt_ref[...]) for e in range(2)]
    m_ref[...] = jnp.full_like(m_ref, NEG_BIG)
    acc_ref[...] = jnp.zeros_like(acc_ref)
    n_sub = TK // TK_SUB

    def qk(e, start):
        return jnp.dot(kaug_ref[e, pl.ds(start, TK_SUB), :], qt[e], preferred_element_type=F32)

    def values(e, start, width):
        return jnp.concatenate(
            [vt_ref[e * FOX_HEAD_DIM:(e + 1) * FOX_HEAD_DIM, pl.ds(start, width)],
             jnp.ones((BF16_ROWS, width), BF16)], axis=0)

    def pv(e, st, m_new, start, width):
        pt = jnp.exp2(st - m_new).astype(BF16)
        return jnp.dot(values(e, start, width), pt, preferred_element_type=F32)

    def scores(j, e, s_ref, c_ref):
        cmax = None
        for sub in range(n_sub):
            st = qk(e, pl.multiple_of(j * TK + sub * TK_SUB, TK_SUB))
            s_ref[e, sub * TK_SUB:(sub + 1) * TK_SUB, :] = st
            cm = jnp.max(st, axis=0, keepdims=True)
            cmax = cm if cmax is None else jnp.maximum(cmax, cm)
        c_ref[e] = cmax

    def stage(e, j_next, buf_next, j_cur, buf_cur):
        (s_next, c_next), (s_cur, c_cur) = buf_next, buf_cur
        m_prev = m_ref[e]
        m_new = jnp.maximum(m_prev, c_cur[e])
        acc = jnp.exp2(m_prev - m_new) * acc_ref[e]
        cmax = None
        for sub in range(n_sub):
            rows = slice(sub * TK_SUB, (sub + 1) * TK_SUB)
            st = qk(e, pl.multiple_of(j_next * TK + sub * TK_SUB, TK_SUB))
            s_next[e, rows, :] = st
            cm = jnp.max(st, axis=0, keepdims=True)
            cmax = cm if cmax is None else jnp.maximum(cmax, cm)
            acc = acc + pv(e, s_cur[e, rows, :], m_new,
                           pl.multiple_of(j_cur * TK + sub * TK_SUB, TK_SUB), TK_SUB)
        c_next[e] = cmax
        acc_ref[e] = acc
        m_ref[e] = m_new

    def finish(j, s_ref):
        n_pieces = TK // TQ
        tri = (lax.broadcasted_iota(jnp.int32, (TQ, TQ), 0)
               <= lax.broadcasted_iota(jnp.int32, (TQ, TQ), 1))
        for diag in range(n_pieces):
            @pl.when(i % n_pieces == diag)
            def _():
                for e in range(2):
                    for piece in range(diag + 1):
                        st = s_ref[e, piece * TQ:(piece + 1) * TQ, :]
                        if piece == diag:
                            st = jnp.where(tri, st, NEG_BIG)
                        m_prev = m_ref[e]
                        m_new = jnp.maximum(m_prev, jnp.max(st, axis=0, keepdims=True))
                        start = pl.multiple_of(j * TK + piece * TQ, TQ)
                        acc_ref[e] = (jnp.exp2(m_prev - m_new) * acc_ref[e]
                                      + pv(e, st, m_new, start, TQ))
                        m_ref[e] = m_new

    buf_a = (sa_ref, ca_ref)
    buf_b = (sb_ref, cb_ref)

    nfull = (i * TQ) // TK
    for e in range(2):
        scores(0, e, *buf_a)

    def pair(jj, carry):
        j = 2 * jj
        for e in range(2):
            stage(e, j + 1, buf_b, j, buf_a)
        for e in range(2):
            stage(e, j + 2, buf_a, j + 1, buf_b)
        return carry

    def group(jg, carry):
        for u in range(PAIRS_PER_TRIP):
            carry = pair(PAIRS_PER_TRIP * jg + u, carry)
        return carry

    n_pairs = nfull // 2
    n_groups = n_pairs // PAIRS_PER_TRIP
    lax.fori_loop(0, n_groups, group, 0)
    lax.fori_loop(PAIRS_PER_TRIP * n_groups, n_pairs, pair, 0)

    @pl.when(nfull % 2 == 1)
    def _():
        for e in range(2):
            stage(e, nfull, buf_b, nfull - 1, buf_a)
        finish(nfull, sb_ref)

    @pl.when(nfull % 2 == 0)
    def _():
        finish(nfull, sa_ref)

    outs = []
    for e in range(2):
        acc = acc_ref[e]
        outs.append(acc[:FOX_HEAD_DIM] / acc[FOX_HEAD_DIM:FOX_HEAD_DIM + 1])
    o_ref[...] = jnp.concatenate(outs, axis=0).T.astype(o_ref.dtype)


def _fox_attention(qt, qbt, k, kb, vt):
    s = k.shape[0]
    assert TK % TQ == 0 and TK % TK_SUB == 0 and s % TK == 0
    blk_t = pl.BlockSpec((LANES, TQ), lambda p, i: (p, i))
    res = pl.BlockSpec((s, LANES), lambda p, i: (0, p))
    return pl.pallas_call(
        _fox_kernel,
        grid=(FOX_PAIRS, s // TQ),
        in_specs=[blk_t, blk_t, res, res, pl.BlockSpec((LANES, s), lambda p, i: (p, 0))],
        out_specs=pl.BlockSpec((TQ, LANES), lambda p, i: (i, p)),
        out_shape=jax.ShapeDtypeStruct((s, FOX_WIDTH), BF16),
        scratch_shapes=[pltpu.VMEM((2, s, LANES), BF16),
                        pltpu.VMEM((2, 1, TQ), F32), pltpu.VMEM((2, PV_ROWS, TQ), F32),
                        pltpu.VMEM((2, TK, TQ), F32), pltpu.VMEM((2, 1, TQ), F32),
                        pltpu.VMEM((2, TK, TQ), F32), pltpu.VMEM((2, 1, TQ), F32)],
        compiler_params=pltpu.CompilerParams(dimension_semantics=("arbitrary", "arbitrary"),
                                             vmem_limit_bytes=VMEM_LIMIT),
        name="fox_attention",
    )(qt, qbt, k, kb, vt)


def _ret_kernel(q_ref, kt_ref, v_ref, g_ref, inner_ref, xi_ref, zeta_ref, gc_ref, o_ref, r_ref):
    @pl.when(pl.program_id(0) == 0)
    def _():
        r_ref[...] = jnp.zeros_like(r_ref)

    c = RET_CHUNK
    n_chunks = q_ref.shape[0] // c
    tiles = [(hh, ci) for hh in range(RET_HEADS) for ci in range(n_chunks)]

    def head_cols(hh):
        return slice(hh * RET_HEAD_DIM, (hh + 1) * RET_HEAD_DIM)

    def chunk_rows(ci):
        return slice(ci * c, (ci + 1) * c)

    intra, kv = {}, {}
    for hh, ci in tiles:
        q = q_ref[chunk_rows(ci), head_cols(hh)]
        kt = kt_ref[head_cols(hh), chunk_rows(ci)]
        v = v_ref[chunk_rows(ci), head_cols(hh)]
        intra[hh, ci] = (jnp.dot(q, kt, preferred_element_type=F32) * inner_ref[hh]).astype(BF16)
        kzt = (kt.astype(F32) * zeta_ref[hh]).astype(BF16)
        kv[hh, ci] = jnp.dot(kzt, v, preferred_element_type=F32)

    state = {}
    for hh in range(RET_HEADS):
        r_state = r_ref[hh]
        for ci in range(n_chunks):
            state[hh, ci] = r_state.astype(BF16)
            r_state = r_state * gc_ref[hh] + kv[hh, ci]
        r_ref[hh] = r_state

    for hh, ci in tiles:
        q = q_ref[chunk_rows(ci), head_cols(hh)]
        v = v_ref[chunk_rows(ci), head_cols(hh)]
        o = (jnp.dot(intra[hh, ci], v, preferred_element_type=F32)
             + jnp.dot(q, state[hh, ci], preferred_element_type=F32) * xi_ref[hh])
        mu = jnp.mean(o, axis=-1, keepdims=True)
        d = o - mu
        var = jnp.mean(d * d, axis=-1, keepdims=True)
        on = d * lax.rsqrt(var + GN_EPS)
        g = g_ref[chunk_rows(ci), head_cols(hh)].astype(F32)
        o_ref[chunk_rows(ci), head_cols(hh)] = (g * jax.nn.sigmoid(g) * on).astype(o_ref.dtype)


def _retention(rq, rkt, rv, rg, inner, xi_b, zeta_b, gc_b):
    s = rq.shape[0]
    blk = pl.BlockSpec((TR, RET_WIDTH), lambda i: (i, 0))
    blk_t = pl.BlockSpec((RET_WIDTH, TR), lambda i: (0, i))
    tab = _const_spec((RET_HEADS, RET_CHUNK, RET_CHUNK))
    return pl.pallas_call(
        _ret_kernel,
        grid=(s // TR,),
        in_specs=[blk, blk_t, blk, blk, tab, tab, tab, _const_spec((RET_HEADS, 1, RET_HEAD_DIM))],
        out_specs=blk,
        out_shape=jax.ShapeDtypeStruct((s, RET_WIDTH), BF16),
        scratch_shapes=[pltpu.VMEM((RET_HEADS, RET_HEAD_DIM, RET_HEAD_DIM), F32)],
        compiler_params=pltpu.CompilerParams(dimension_semantics=("arbitrary",)),
        name="retention",
    )(rq, rkt, rv, rg, inner, xi_b, zeta_b, gc_b)


def _mix_ffn_kernel(fox_ref, ret_ref, x_ref, fox0_ref, ret0_ref, x0_ref,
                    wf_ref, wr_ref, g1_ref, l1g_ref, l1b_ref,
                    sc_ref, sh_ref, g2_ref, wu_ref, cw_ref, cb_ref, wd_ref, l2g_ref, l2b_ref,
                    o_ref, carry_ref, slab_in_ref, slab_out_ref, y_ref, x1_ref):
    tm = x_ref.shape[0]
    grp = tm // SUBLANES
    n_slab = D_MODEL // LANES

    def front(fox, ret, x):
        mix = (jnp.dot(fox, wf_ref[...], preferred_element_type=F32)
               + jnp.dot(ret, wr_ref[...], preferred_element_type=F32))
        x1_nat = _layer_norm(ALPHA * x + g1_ref[...] * mix, l1g_ref[...], l1b_ref[...])
        for c in range(n_slab):
            slab_in_ref[c] = x1_nat[:, c * LANES:(c + 1) * LANES]
        return jnp.concatenate(
            [jnp.concatenate([slab_in_ref[c, pl.ds(v, SUBLANES, stride=grp), :] for v in range(grp)],
                             axis=0)
             for c in range(n_slab)], axis=1)

    @pl.when(pl.program_id(0) == 0)
    def _():
        carry_ref[...] = jnp.zeros_like(carry_ref)
        x1_ref[...] = front(fox0_ref[...], ret0_ref[...], x0_ref[...])

    h = (x1_ref[...] * (1.0 + sc_ref[...]) + sh_ref[...]).astype(BF16)
    first_sublane = lax.broadcasted_iota(jnp.int32, (SUBLANES, FF_CHUNK), 0) == 0

    def conv_up(c0):
        cols = slice(c0, c0 + FF_CHUNK)
        up = jnp.dot(h, wu_ref[:, cols], preferred_element_type=F32)
        prev = carry_ref[:, cols]
        tail = []
        for g in range(2):
            rows = slice(tm - (2 - g) * SUBLANES, tm - (1 - g) * SUBLANES)
            tail.append(jnp.where(first_sublane,
                                  pltpu.roll(prev[g * SUBLANES:(g + 1) * SUBLANES], 1, 0),
                                  pltpu.roll(up[rows], 1, 0)))
        carry_ref[:, cols] = up[tm - 2 * SUBLANES:tm]
        back1 = jnp.concatenate([tail[1], up[:tm - SUBLANES]], axis=0)
        back2 = jnp.concatenate([tail[0], tail[1], up[:tm - 2 * SUBLANES]], axis=0)
        cw = cw_ref[:, cols] * RSQRT2
        return cb_ref[:, cols] * RSQRT2 + back2 * cw[0:1] + back1 * cw[1:2] + up * cw[2:3]

    for ci in range(D_FF // FF_CHUNK):
        a = conv_up(ci * FF_CHUNK)
        b = conv_up(D_FF + ci * FF_CHUNK)
        y_ref[:, ci * FF_CHUNK:(ci + 1) * FF_CHUNK] = (a * (1.0 + lax.erf(a)) * b).astype(BF16)

    x1_next = front(fox_ref[...], ret_ref[...], x_ref[...])

    ffn = jnp.dot(y_ref[...], wd_ref[...], preferred_element_type=F32)
    out = _layer_norm(ALPHA * x1_ref[...] + g2_ref[...] * ffn, l2g_ref[...], l2b_ref[...])
    x1_ref[...] = x1_next

    for c in range(n_slab):
        for v in range(grp):
            slab_out_ref[c, pl.ds(v, SUBLANES, stride=grp), :] = out[v * SUBLANES:(v + 1) * SUBLANES,
                                                                     c * LANES:(c + 1) * LANES]
    for c in range(n_slab):
        o_ref[:, c * LANES:(c + 1) * LANES] = slab_out_ref[c]


def _mix_ffn(fox, ret, x, w_fox, w_ret, g1, ln1_g, ln1_b, sc2, sh2, g2, w_up, conv_w, conv_b, w_down,
             ln2_g, ln2_b):
    s, d = x.shape
    tm = TM_FFN
    n_tiles = s // tm
    nxt = lambda w: pl.BlockSpec((tm, w), lambda i: (jnp.minimum(i + 1, n_tiles - 1), 0))
    vec = _const_spec((1, d))
    slab = pltpu.VMEM((d // LANES, tm, LANES), F32)
    return pl.pallas_call(
        _mix_ffn_kernel,
        grid=(n_tiles,),
        in_specs=[nxt(FOX_WIDTH), nxt(RET_WIDTH), nxt(d),
                  _const_spec((tm, FOX_WIDTH)), _const_spec((tm, RET_WIDTH)), _const_spec((tm, d)),
                  _const_spec((FOX_WIDTH, d)), _const_spec((RET_WIDTH, d)), vec, vec, vec,
                  vec, vec, vec,
                  _const_spec((d, 2 * D_FF)), _const_spec((CONV_WIDTH, 2 * D_FF)),
                  _const_spec((1, 2 * D_FF)), _const_spec((D_FF, d)), vec, vec],
        out_specs=pl.BlockSpec((tm, d), lambda i: (i, 0)),
        out_shape=jax.ShapeDtypeStruct((s, d), F32),
        scratch_shapes=[pltpu.VMEM((2 * SUBLANES, 2 * D_FF), F32), slab, slab,
                        pltpu.VMEM((tm, D_FF), BF16), pltpu.VMEM((tm, d), F32)],
        compiler_params=pltpu.CompilerParams(dimension_semantics=("arbitrary",),
                                             vmem_limit_bytes=VMEM_LIMIT),
        name="mix_ffn",
    )(fox, ret, x, fox, ret, x, w_fox, w_ret, g1, ln1_g, ln1_b, sc2, sh2, g2, w_up, conv_w, conv_b,
      w_down, ln2_g, ln2_b)


def _placement_tables():
    e_q = np.zeros((LANES, FOX_WIDTH), np.float32)
    e_k = np.zeros((LANES, FOX_WIDTH), np.float32)
    for h in range(FOX_HEADS):
        base = (h // 2) * LANES + (FOX_HEAD_DIM if h % 2 == 0 else 0)
        for j in range(N_PIECES):
            e_q[j * FOX_HEADS + h, base + BIAS_CUM + j] = 1.0
            e_q[ONE_LANE, base + BIAS_ONE + j] = 1.0
            e_k[ONE_LANE, base + BIAS_CUM + j] = 1.0
            e_k[j * FOX_HEADS + h, base + BIAS_ONE + j] = -1.0
    return jnp.asarray(e_q.T, BF16), jnp.asarray(e_k, BF16)


def _rotation_tables(s, tm):
    dk = RET_HEAD_DIM
    inv_freq = ROPE_BASE ** (-np.arange(0, dk, 2, dtype=np.float64) / dk)
    inv_freq = np.concatenate([inv_freq, inv_freq])
    sign = np.concatenate([-np.ones(dk // 2), np.ones(dk // 2)])
    ang_a = (np.arange(s // tm, dtype=np.float64) * tm)[:, None] * inv_freq[None, :]
    ang_b = np.arange(tm, dtype=np.float64)[:, None] * inv_freq[None, :]
    rot_a = np.stack([np.cos(ang_a), np.sin(ang_a)], axis=1)
    rot_b = np.stack([np.cos(ang_b), np.sin(ang_b), sign * np.cos(ang_b), sign * np.sin(ang_b)])
    f32 = lambda a: jnp.asarray(a, F32)
    return f32(rot_a), f32(rot_b), f32(rot_a.transpose(0, 2, 1)), f32(rot_b.transpose(0, 2, 1))


def _decay_tables():
    c = RET_CHUNK
    log_gamma = np.log1p(-np.exp2(-5.0 - np.arange(RET_HEADS, dtype=np.float64)))
    idx = np.arange(c, dtype=np.float64)
    diff = idx[:, None] - idx[None, :]
    inner = np.where(diff[None] >= 0,
                     np.exp(np.maximum(diff, 0.0)[None] * log_gamma[:, None, None]), 0.0)
    xi = np.exp((idx[None, :] + 1.0) * log_gamma[:, None])
    zeta = np.exp((c - 1.0 - idx[None, :]) * log_gamma[:, None])
    g_chunk = np.exp(c * log_gamma)
    bshape = (RET_HEADS, c, RET_HEAD_DIM)
    f32 = lambda a: jnp.asarray(a, F32)
    return (f32(inner),
            f32(np.broadcast_to(xi[:, :, None], bshape)),
            f32(np.broadcast_to(zeta[:, None, :], bshape)),
            f32(np.broadcast_to(g_chunk[:, None, None], (RET_HEADS, 1, RET_HEAD_DIM))))


def kernel(x, c, w_ada, b_ada, w_in, b_f, w_out, ln1_g, ln1_b, w_up, conv_w, conv_b, w_down, ln2_g, ln2_b):
    b, s, d = x.shape
    assert (b, s, d) == (1, SEQ, D_MODEL) and w_ada.shape[0] == DEPTH
    xs = x[0]

    wi = w_in[0]
    o = 0
    parts = {}
    for name, width in (("fq", FOX_WIDTH), ("fk", FOX_WIDTH), ("fv", FOX_WIDTH), ("ff", FOX_HEADS),
                        ("rq", RET_WIDTH), ("rk", RET_WIDTH), ("rv", RET_WIDTH), ("rg", RET_WIDTH)):
        parts[name] = wi[:, o:o + width]
        o += width
    w_all = jnp.concatenate(
        [parts["fk"], jnp.pad(parts["ff"], ((0, 0), (0, LANES - FOX_HEADS))),
         parts["rq"], parts["rv"], parts["rg"]], axis=1).astype(BF16)
    w_t = jnp.concatenate([parts["fq"], parts["fv"], parts["rk"]], axis=1).T.astype(BF16)
    bf_row = jnp.pad(b_f[0][None, :], ((0, 0), (0, LANES - FOX_HEADS)))
    wo = w_out[0].astype(BF16)

    eq_t, e_k = _placement_tables()
    rot_a, rot_b, rot_at, rot_bt = _rotation_tables(s, TM_PROJ)
    inner, xi_b, zeta_b, gc_b = _decay_tables()

    mod = _adaln_mod(c.reshape(d, 1), w_ada[0], b_ada[0][None, :])
    sh1, sc1, g1, sh2, sc2, g2 = [mod[:, i * d:(i + 1) * d] for i in range(6)]

    qt, qbt, k, kb, vt, rq, rkt, rv, rg = _in_proj(
        xs, sc1, sh1, w_all, w_t, bf_row, rot_a, rot_b, rot_at, rot_bt, eq_t, e_k)
    fox = _fox_attention(qt, qbt, k, kb, vt)
    ret = _retention(rq, rkt, rv, rg, inner, xi_b, zeta_b, gc_b)
    out = _mix_ffn(fox, ret, xs, wo[:FOX_WIDTH], wo[FOX_WIDTH:], g1, ln1_g[0][None, :], ln1_b[0][None, :],
                   sc2, sh2, g2, w_up[0].astype(BF16), conv_w[0], conv_b[0][None, :],
                   w_down[0].astype(BF16), ln2_g[0][None, :], ln2_b[0][None, :])
    return out[None]
```

```python
import math

import jax
import jax.numpy as jnp
import numpy as np
from jax import lax
from jax.experimental import pallas as pl
from jax.experimental.pallas import tpu as pltpu

F32 = jnp.float32
BF16 = jnp.bfloat16

D_MODEL = 1024
SEQ = 16384
FOX_HEADS = 8
FOX_HEAD_DIM = 64
FOX_WIDTH = FOX_HEADS * FOX_HEAD_DIM
RET_HEADS = 4
RET_HEAD_DIM = 128
RET_WIDTH = RET_HEADS * RET_HEAD_DIM
D_FF = 2816
CONV_WIDTH = 3
RET_CHUNK = 128
ROPE_BASE = 10000.0
LN_EPS = 1e-5
GN_EPS = 1e-6
DEPTH = 1
ALPHA = (2 * DEPTH) ** 0.25

LANES = 128
SUBLANES = 8
BF16_ROWS = 16
VMEM_LIMIT = 56 * 1024 * 1024

FOX_PAIRS = FOX_HEADS // 2
BIAS_CUM = 0
BIAS_ONE = 3
N_PIECES = 3
ONE_LANE = N_PIECES * FOX_HEADS
PV_ROWS = FOX_HEAD_DIM + BF16_ROWS
NEG_BIG = -1e30
LOG2_E = math.log2(math.e)
RSQRT2 = 2.0 ** -0.5

COL_K = 0
COL_F = COL_K + FOX_WIDTH
COL_RQ = COL_F + LANES
COL_RV = COL_RQ + RET_WIDTH
COL_RG = COL_RV + RET_WIDTH
N_COLS = COL_RG + RET_WIDTH
ROW_QT = 0
ROW_VT = ROW_QT + FOX_WIDTH
ROW_RKT = ROW_VT + FOX_WIDTH
N_ROWS_T = ROW_RKT + RET_WIDTH

TM_PROJ = 512
TQ = 512
TK = 1024
TK_SUB = 256
PAIRS_PER_TRIP = 2
TR = 512
TM_FFN = 512
FF_CHUNK = 256
SLAB_PAD_ROWS = 4


def _const_spec(shape):
    return pl.BlockSpec(shape, lambda *_: (0,) * len(shape), pipeline_mode=pl.Buffered(1))


def _split3(x):
    p0 = x.astype(BF16).astype(F32)
    r1 = x - p0
    p1 = r1.astype(BF16).astype(F32)
    p2 = (r1 - p1).astype(BF16).astype(F32)
    return p0, p1, p2


def _layer_norm(y, g, b):
    mu = jnp.mean(y, axis=-1, keepdims=True)
    d = y - mu
    var = jnp.mean(d * d, axis=-1, keepdims=True)
    return d * lax.rsqrt(var + LN_EPS) * g + b


def _mod_kernel(c_ref, w_ref, b_ref, o_ref):
    c = c_ref[...]
    sc = c * jax.nn.sigmoid(c)
    o_ref[...] = jnp.sum(w_ref[...] * sc, axis=0, keepdims=True) + b_ref[...]


def _adaln_mod(c_col, w_ada, b_ada):
    d, n = w_ada.shape
    tn = 1024
    return pl.pallas_call(
        _mod_kernel,
        grid=(n // tn,),
        in_specs=[pl.BlockSpec((d, 1), lambda j: (0, 0)),
                  pl.BlockSpec((d, tn), lambda j: (0, j)),
                  pl.BlockSpec((1, tn), lambda j: (0, j))],
        out_specs=pl.BlockSpec((1, tn), lambda j: (0, j)),
        out_shape=jax.ShapeDtypeStruct((1, n), F32),
        compiler_params=pltpu.CompilerParams(dimension_semantics=("arbitrary",)),
        name="adaln_mod",
    )(c_col, w_ada, b_ada)


def _in_proj_kernel(x_ref, sc_ref, sh_ref, w_ref, wt_ref, bf_ref, ra_ref, rb_ref, rat_ref, rbt_ref,
                    eqt_ref, ek_ref,
                    qt0_out, qt1_out, k0_out, k1_out, vt_out, rq_out, rkt_out, rv_out, rg_out,
                    carry_ref):
    tm = x_ref.shape[0]

    @pl.when(pl.program_id(0) == 0)
    def _():
        carry_ref[...] = jnp.zeros_like(carry_ref)

    h = (x_ref[...] * (1.0 + sc_ref[...]) + sh_ref[...]).astype(BF16)

    def proj(c0, width):
        return jnp.dot(h, w_ref[:, c0:c0 + width], preferred_element_type=F32)

    lane = lax.broadcasted_iota(jnp.int32, (tm, LANES), 1)
    head_lane = lane < FOX_HEADS
    ff = proj(COL_F, LANES) + bf_ref[...]
    k = proj(COL_K, FOX_WIDTH)
    logf = jnp.minimum(ff, 0.0) - jnp.log1p(jnp.exp(-jnp.abs(ff)))
    logf = jnp.where(head_lane, logf, 0.0)
    p0, p1, p2 = _split3(logf)
    pieces = p0 + pltpu.roll(p1, FOX_HEADS, 1) + pltpu.roll(p2, 2 * FOX_HEADS, 1)
    row = lax.broadcasted_iota(jnp.int32, (tm, tm), 0)
    col = lax.broadcasted_iota(jnp.int32, (tm, tm), 1)
    tril = jnp.where(row >= col, 1.0, 0.0).astype(BF16)
    c3 = jnp.dot(tril, pieces.astype(BF16), preferred_element_type=F32)
    tp = lax.dot_general(wt_ref[...], h, (((1,), (1,)), ((), ())), preferred_element_type=F32)
    cum = c3 + pltpu.roll(c3, LANES - FOX_HEADS, 1) + pltpu.roll(c3, LANES - 2 * FOX_HEADS, 1)
    cum = jnp.where(head_lane, cum, 0.0) + carry_ref[0:1, :]
    carry_ref[0:1, :] = cum[tm - 1:tm, :]
    c0_, c1_, c2_ = _split3(cum * LOG2_E)
    cum_pieces = c0_ + pltpu.roll(c1_, FOX_HEADS, 1) + pltpu.roll(c2_, 2 * FOX_HEADS, 1)
    cum_pieces = jnp.where(lane == ONE_LANE, 1.0, cum_pieces)
    kb = jnp.dot(cum_pieces.astype(BF16), ek_ref[...], preferred_element_type=F32)
    qbt = jnp.dot(eqt_ref[...], cum_pieces.T.astype(BF16), preferred_element_type=F32)
    qt = tp[ROW_QT:ROW_QT + FOX_WIDTH] * (FOX_HEAD_DIM ** -0.5 * LOG2_E)
    even_own = (lax.broadcasted_iota(jnp.int32, (1, FOX_WIDTH), 1) % LANES) < FOX_HEAD_DIM
    even_own_t = (lax.broadcasted_iota(jnp.int32, (FOX_WIDTH, 1), 0) % LANES) < FOX_HEAD_DIM
    k0_out[...] = jnp.where(even_own, k, kb).astype(BF16)
    k1_out[...] = jnp.where(even_own, kb, k).astype(BF16)
    qt0_out[...] = jnp.where(even_own_t, qt, qbt).astype(BF16)
    qt1_out[...] = jnp.where(even_own_t, qbt, qt).astype(BF16)
    vt_out[...] = tp[ROW_VT:ROW_VT + FOX_WIDTH].astype(BF16)

    half = RET_HEAD_DIM // 2
    ca, sa = ra_ref[0, 0:1, :], ra_ref[0, 1:2, :]
    cos = ca * rb_ref[0] - sa * rb_ref[1]
    sin = sa * rb_ref[2] + ca * rb_ref[3]
    cat, sat = rat_ref[0, :, 0:1], rat_ref[0, :, 1:2]
    cost = cat * rbt_ref[0] - sat * rbt_ref[1]
    sint = sat * rbt_ref[2] + cat * rbt_ref[3]
    kscale = RET_HEAD_DIM ** -0.5
    two = 2 * RET_HEAD_DIM
    for pair in range(RET_HEADS // 2):
        a2 = proj(COL_RQ + pair * two, two)
        for e in range(2):
            lo = pair * two + e * RET_HEAD_DIM
            a = a2[:, e * RET_HEAD_DIM:(e + 1) * RET_HEAD_DIM]
            rq_out[:, lo:lo + RET_HEAD_DIM] = (a * cos + pltpu.roll(a, half, 1) * sin).astype(BF16)
    for hh in range(RET_HEADS):
        lo = hh * RET_HEAD_DIM
        b = tp[ROW_RKT + lo:ROW_RKT + lo + RET_HEAD_DIM]
        b_rot = jnp.concatenate([b[half:], b[:half]], axis=0)
        rkt_out[lo:lo + RET_HEAD_DIM, :] = ((b * cost + b_rot * sint) * kscale).astype(BF16)
    rv_out[...] = proj(COL_RV, RET_WIDTH).astype(BF16)
    rg_out[...] = proj(COL_RG, RET_WIDTH).astype(BF16)


def _in_proj(x, sc1, sh1, w_all, w_t, bf_row, rot_a, rot_b, rot_at, rot_bt, eq_t, e_k):
    s, d = x.shape
    tm = TM_PROJ
    row = lambda w: pl.BlockSpec((tm, w), lambda i: (i, 0))
    col = lambda h: pl.BlockSpec((h, tm), lambda i: (0, i))
    out_shapes = ([jax.ShapeDtypeStruct((FOX_WIDTH, s), BF16)] * 2
                  + [jax.ShapeDtypeStruct((s, FOX_WIDTH), BF16)] * 2
                  + [jax.ShapeDtypeStruct((FOX_WIDTH, s), BF16),
                     jax.ShapeDtypeStruct((s, RET_WIDTH), BF16),
                     jax.ShapeDtypeStruct((RET_WIDTH, s), BF16)]
                  + [jax.ShapeDtypeStruct((s, RET_WIDTH), BF16)] * 2)
    return pl.pallas_call(
        _in_proj_kernel,
        grid=(s // tm,),
        in_specs=[row(d), _const_spec((1, d)), _const_spec((1, d)),
                  _const_spec((d, N_COLS)), _const_spec((N_ROWS_T, d)), _const_spec((1, LANES)),
                  pl.BlockSpec((1, 2, RET_HEAD_DIM), lambda i: (i, 0, 0)),
                  _const_spec((4, tm, RET_HEAD_DIM)),
                  pl.BlockSpec((1, RET_HEAD_DIM, 2), lambda i: (i, 0, 0)),
                  _const_spec((4, RET_HEAD_DIM, tm)),
                  _const_spec((FOX_WIDTH, LANES)), _const_spec((LANES, FOX_WIDTH))],
        out_specs=([col(FOX_WIDTH)] * 2 + [row(FOX_WIDTH)] * 2
                   + [col(FOX_WIDTH), row(RET_WIDTH), col(RET_WIDTH), row(RET_WIDTH), row(RET_WIDTH)]),
        out_shape=out_shapes,
        scratch_shapes=[pltpu.VMEM((SUBLANES, LANES), F32)],
        compiler_params=pltpu.CompilerParams(dimension_semantics=("arbitrary",),
                                             vmem_limit_bytes=VMEM_LIMIT),
        name="in_proj",
    )(x, sc1, sh1, w_all, w_t, bf_row, rot_a, rot_b, rot_at, rot_bt, eq_t, e_k)


def _fox_kernel(qt0_ref, qt1_ref, qtn0_ref, qtn1_ref, k0_ref, k1_ref, vt_ref, o_ref, m_ref, acc_ref,
                sa_ref, ca_ref, sb_ref, cb_ref):
    i = pl.program_id(1)
    k_refs = (k0_ref, k1_ref)
    qt = [qt0_ref[...], qt1_ref[...]]
    qt_next = [qtn0_ref[...], qtn1_ref[...]]
    m_ref[...] = jnp.full_like(m_ref, NEG_BIG)
    acc_ref[...] = jnp.zeros_like(acc_ref)
    n_sub = TK // TK_SUB

    def qk(e, q_aug, start):
        return jnp.dot(k_refs[e][pl.ds(start, TK_SUB), :], q_aug, preferred_element_type=F32)

    def values(e, start, width):
        return jnp.concatenate(
            [vt_ref[e * FOX_HEAD_DIM:(e + 1) * FOX_HEAD_DIM, pl.ds(start, width)],
             jnp.ones((BF16_ROWS, width), BF16)], axis=0)

    def pv(e, st, m_new, start, width):
        pt = jnp.exp2(st - m_new).astype(BF16)
        return jnp.dot(values(e, start, width), pt, preferred_element_type=F32)

    def score_sub(e, q_aug, j, sub, s_ref, cmax):
        st = qk(e, q_aug, pl.multiple_of(j * TK + sub * TK_SUB, TK_SUB))
        s_ref[e, sub * TK_SUB:(sub + 1) * TK_SUB, :] = st
        cm = jnp.max(st, axis=0, keepdims=True)
        return cm if cmax is None else jnp.maximum(cmax, cm)

    def scores(j, e, s_ref, c_ref):
        cmax = None
        for sub in range(n_sub):
            cmax = score_sub(e, qt[e], j, sub, s_ref, cmax)
        c_ref[e] = cmax

    def stage(e, j_next, buf_next, j_cur, buf_cur):
        (s_next, c_next), (s_cur, c_cur) = buf_next, buf_cur
        m_prev = m_ref[e]
        m_new = jnp.maximum(m_prev, c_cur[e])
        acc = jnp.exp2(m_prev - m_new) * acc_ref[e]
        cmax = None
        for sub in range(n_sub):
            rows = slice(sub * TK_SUB, (sub + 1) * TK_SUB)
            cmax = score_sub(e, qt[e], j_next, sub, s_next, cmax)
            acc = acc + pv(e, s_cur[e, rows, :], m_new,
                           pl.multiple_of(j_cur * TK + sub * TK_SUB, TK_SUB), TK_SUB)
        c_next[e] = cmax
        acc_ref[e] = acc
        m_ref[e] = m_new

    def finish(j, s_ref, fill_next):
        n_pieces = TK // TQ
        tri = (lax.broadcasted_iota(jnp.int32, (TQ, TQ), 0)
               <= lax.broadcasted_iota(jnp.int32, (TQ, TQ), 1))
        for diag in range(n_pieces):
            @pl.when(i % n_pieces == diag)
            def _():
                for e in range(2):
                    cmax = None
                    for piece in range(n_pieces):
                        if fill_next:
                            for sub in range(piece * n_sub // n_pieces, (piece + 1) * n_sub // n_pieces):
                                cmax = score_sub(e, qt_next[e], 0, sub, sa_ref, cmax)
                        if piece > diag:
                            continue
                        st = s_ref[e, piece * TQ:(piece + 1) * TQ, :]
                        if piece == diag:
                            st = jnp.where(tri, st, NEG_BIG)
                        m_prev = m_ref[e]
                        m_new = jnp.maximum(m_prev, jnp.max(st, axis=0, keepdims=True))
                        start = pl.multiple_of(j * TK + piece * TQ, TQ)
                        acc_ref[e] = (jnp.exp2(m_prev - m_new) * acc_ref[e]
                                      + pv(e, st, m_new, start, TQ))
                        m_ref[e] = m_new
                    if fill_next:
                        ca_ref[e] = cmax

    buf_a = (sa_ref, ca_ref)
    buf_b = (sb_ref, cb_ref)

    nfull = (i * TQ) // TK
    prefilled = jnp.logical_and(i > 0, (((i - 1) * TQ) // TK) % 2 == 1)

    @pl.when(jnp.logical_not(prefilled))
    def _():
        for e in range(2):
            scores(0, e, *buf_a)

    def pair(jj, carry):
        j = 2 * jj
        for e in range(2):
            stage(e, j + 1, buf_b, j, buf_a)
        for e in range(2):
            stage(e, j + 2, buf_a, j + 1, buf_b)
        return carry

    def group(jg, carry):
        for u in range(PAIRS_PER_TRIP):
            carry = pair(PAIRS_PER_TRIP * jg + u, carry)
        return carry

    n_pairs = nfull // 2
    n_groups = n_pairs // PAIRS_PER_TRIP
    lax.fori_loop(0, n_groups, group, 0)
    lax.fori_loop(PAIRS_PER_TRIP * n_groups, n_pairs, pair, 0)

    @pl.when(nfull % 2 == 1)
    def _():
        for e in range(2):
            stage(e, nfull, buf_b, nfull - 1, buf_a)
        finish(nfull, sb_ref, True)

    @pl.when(nfull % 2 == 0)
    def _():
        finish(nfull, sa_ref, False)

    outs = []
    for e in range(2):
        acc = acc_ref[e]
        outs.append(acc[:FOX_HEAD_DIM] / acc[FOX_HEAD_DIM:FOX_HEAD_DIM + 1])
    o_ref[...] = jnp.concatenate(outs, axis=0).T.astype(o_ref.dtype)


def _fox_attention(qt0, qt1, k0, k1, vt):
    s = k0.shape[0]
    assert TK % TQ == 0 and TK % TK_SUB == 0 and s % TK == 0
    n_q = s // TQ
    blk_t = pl.BlockSpec((LANES, TQ), lambda p, i: (p, i))
    nxt_t = pl.BlockSpec((LANES, TQ), lambda p, i: (p, jnp.minimum(i + 1, n_q - 1)))
    res = pl.BlockSpec((s, LANES), lambda p, i: (0, p))
    return pl.pallas_call(
        _fox_kernel,
        grid=(FOX_PAIRS, n_q),
        in_specs=[blk_t, blk_t, nxt_t, nxt_t, res, res, pl.BlockSpec((LANES, s), lambda p, i: (p, 0))],
        out_specs=pl.BlockSpec((TQ, LANES), lambda p, i: (i, p)),
        out_shape=jax.ShapeDtypeStruct((s, FOX_WIDTH), BF16),
        scratch_shapes=[pltpu.VMEM((2, 1, TQ), F32), pltpu.VMEM((2, PV_ROWS, TQ), F32),
                        pltpu.VMEM((2, TK, TQ), F32), pltpu.VMEM((2, 1, TQ), F32),
                        pltpu.VMEM((2, TK, TQ), F32), pltpu.VMEM((2, 1, TQ), F32)],
        compiler_params=pltpu.CompilerParams(dimension_semantics=("arbitrary", "arbitrary"),
                                             vmem_limit_bytes=VMEM_LIMIT),
        name="fox_attention",
    )(qt0, qt1, qt0, qt1, k0, k1, vt)


def _ret_kernel(q_ref, kt_ref, v_ref, g_ref, inner_ref, xi_ref, zeta_ref, gc_ref, o_ref, r_ref):
    @pl.when(pl.program_id(0) == 0)
    def _():
        r_ref[...] = jnp.zeros_like(r_ref)

    c = RET_CHUNK
    n_chunks = q_ref.shape[0] // c
    tiles = [(hh, ci) for hh in range(RET_HEADS) for ci in range(n_chunks)]

    def head_cols(hh):
        return slice(hh * RET_HEAD_DIM, (hh + 1) * RET_HEAD_DIM)

    def chunk_rows(ci):
        return slice(ci * c, (ci + 1) * c)

    intra, kv = {}, {}
    for hh, ci in tiles:
        q = q_ref[chunk_rows(ci), head_cols(hh)]
        kt = kt_ref[head_cols(hh), chunk_rows(ci)]
        v = v_ref[chunk_rows(ci), head_cols(hh)]
        intra[hh, ci] = (jnp.dot(q, kt, preferred_element_type=F32) * inner_ref[hh]).astype(BF16)
        kzt = (kt.astype(F32) * zeta_ref[hh]).astype(BF16)
        kv[hh, ci] = jnp.dot(kzt, v, preferred_element_type=F32)

    state = {}
    for hh in range(RET_HEADS):
        r_state = r_ref[hh]
        for ci in range(n_chunks):
            state[hh, ci] = r_state.astype(BF16)
            r_state = r_state * gc_ref[hh] + kv[hh, ci]
        r_ref[hh] = r_state

    for hh, ci in tiles:
        q = q_ref[chunk_rows(ci), head_cols(hh)]
        v = v_ref[chunk_rows(ci), head_cols(hh)]
        o = (jnp.dot(intra[hh, ci], v, preferred_element_type=F32)
             + jnp.dot(q, state[hh, ci], preferred_element_type=F32) * xi_ref[hh])
        mu = jnp.mean(o, axis=-1, keepdims=True)
        d = o - mu
        var = jnp.mean(d * d, axis=-1, keepdims=True)
        on = d * lax.rsqrt(var + GN_EPS)
        g = g_ref[chunk_rows(ci), head_cols(hh)].astype(F32)
        o_ref[chunk_rows(ci), head_cols(hh)] = (g * jax.nn.sigmoid(g) * on).astype(o_ref.dtype)


def _retention(rq, rkt, rv, rg, inner, xi_b, zeta_b, gc_b):
    s = rq.shape[0]
    blk = pl.BlockSpec((TR, RET_WIDTH), lambda i: (i, 0))
    blk_t = pl.BlockSpec((RET_WIDTH, TR), lambda i: (0, i))
    tab = _const_spec((RET_HEADS, RET_CHUNK, RET_CHUNK))
    return pl.pallas_call(
        _ret_kernel,
        grid=(s // TR,),
        in_specs=[blk, blk_t, blk, blk, tab, tab, tab, _const_spec((RET_HEADS, 1, RET_HEAD_DIM))],
        out_specs=blk,
        out_shape=jax.ShapeDtypeStruct((s, RET_WIDTH), BF16),
        scratch_shapes=[pltpu.VMEM((RET_HEADS, RET_HEAD_DIM, RET_HEAD_DIM), F32)],
        compiler_params=pltpu.CompilerParams(dimension_semantics=("arbitrary",)),
        name="retention",
    )(rq, rkt, rv, rg, inner, xi_b, zeta_b, gc_b)


def _mix_ffn_kernel(fox_ref, ret_ref, x_ref, fox0_ref, ret0_ref, x0_ref,
                    wf_ref, wr_ref, g1_ref, l1g_ref, l1b_ref,
                    sc_ref, sh_ref, g2_ref, wu_ref, cw_ref, cb_ref, wd_ref, l2g_ref, l2b_ref,
                    o_ref, carry_ref, slab_in_ref, slab_out_ref, y_ref, x1_ref):
    tm = x_ref.shape[0]
    grp = tm // SUBLANES
    pitch = grp + SLAB_PAD_ROWS
    n_slab = D_MODEL // LANES

    def front(fox, ret, x):
        mix = (jnp.dot(fox, wf_ref[...], preferred_element_type=F32)
               + jnp.dot(ret, wr_ref[...], preferred_element_type=F32))
        x1_nat = _layer_norm(ALPHA * x + g1_ref[...] * mix, l1g_ref[...], l1b_ref[...])
        for c in range(n_slab):
            for s in range(SUBLANES):
                slab_in_ref[c, s * pitch:s * pitch + grp, :] = x1_nat[s * grp:(s + 1) * grp,
                                                                      c * LANES:(c + 1) * LANES]
        return jnp.concatenate(
            [jnp.concatenate([slab_in_ref[c, pl.ds(v, SUBLANES, stride=pitch), :] for v in range(grp)],
                             axis=0)
             for c in range(n_slab)], axis=1)

    @pl.when(pl.program_id(0) == 0)
    def _():
        carry_ref[...] = jnp.zeros_like(carry_ref)
        x1_ref[...] = front(fox0_ref[...], ret0_ref[...], x0_ref[...])

    h = (x1_ref[...] * (1.0 + sc_ref[...]) + sh_ref[...]).astype(BF16)
    first_sublane = lax.broadcasted_iota(jnp.int32, (SUBLANES, FF_CHUNK), 0) == 0

    def conv_up(c0):
        cols = slice(c0, c0 + FF_CHUNK)
        up = jnp.dot(h, wu_ref[:, cols], preferred_element_type=F32)
        prev = carry_ref[:, cols]
        tail = []
        for g in range(2):
            rows = slice(tm - (2 - g) * SUBLANES, tm - (1 - g) * SUBLANES)
            tail.append(jnp.where(first_sublane,
                                  pltpu.roll(prev[g * SUBLANES:(g + 1) * SUBLANES], 1, 0),
                                  pltpu.roll(up[rows], 1, 0)))
        carry_ref[:, cols] = up[tm - 2 * SUBLANES:tm]
        back1 = jnp.concatenate([tail[1], up[:tm - SUBLANES]], axis=0)
        back2 = jnp.concatenate([tail[0], tail[1], up[:tm - 2 * SUBLANES]], axis=0)
        cw = cw_ref[:, cols] * RSQRT2
        return cb_ref[:, cols] * RSQRT2 + back2 * cw[0:1] + back1 * cw[1:2] + up * cw[2:3]

    for ci in range(D_FF // FF_CHUNK):
        a = conv_up(ci * FF_CHUNK)
        b = conv_up(D_FF + ci * FF_CHUNK)
        y_ref[:, ci * FF_CHUNK:(ci + 1) * FF_CHUNK] = (a * (1.0 + lax.erf(a)) * b).astype(BF16)

    x1_next = front(fox_ref[...], ret_ref[...], x_ref[...])

    ffn = jnp.dot(y_ref[...], wd_ref[...], preferred_element_type=F32)
    out = _layer_norm(ALPHA * x1_ref[...] + g2_ref[...] * ffn, l2g_ref[...], l2b_ref[...])
    x1_ref[...] = x1_next

    for c in range(n_slab):
        for v in range(grp):
            slab_out_ref[c, pl.ds(v, SUBLANES, stride=pitch), :] = out[v * SUBLANES:(v + 1) * SUBLANES,
                                                                       c * LANES:(c + 1) * LANES]
    for c in range(n_slab):
        for s in range(SUBLANES):
            o_ref[s * grp:(s + 1) * grp, c * LANES:(c + 1) * LANES] = slab_out_ref[c, s * pitch:s * pitch + grp, :]


def _mix_ffn(fox, ret, x, w_fox, w_ret, g1, ln1_g, ln1_b, sc2, sh2, g2, w_up, conv_w, conv_b, w_down,
             ln2_g, ln2_b):
    s, d = x.shape
    tm = TM_FFN
    n_tiles = s // tm
    nxt = lambda w: pl.BlockSpec((tm, w), lambda i: (jnp.minimum(i + 1, n_tiles - 1), 0))
    vec = _const_spec((1, d))
    slab = pltpu.VMEM((d // LANES, tm + SUBLANES * SLAB_PAD_ROWS, LANES), F32)
    return pl.pallas_call(
        _mix_ffn_kernel,
        grid=(n_tiles,),
        in_specs=[nxt(FOX_WIDTH), nxt(RET_WIDTH), nxt(d),
                  _const_spec((tm, FOX_WIDTH)), _const_spec((tm, RET_WIDTH)), _const_spec((tm, d)),
                  _const_spec((FOX_WIDTH, d)), _const_spec((RET_WIDTH, d)), vec, vec, vec,
                  vec, vec, vec,
                  _const_spec((d, 2 * D_FF)), _const_spec((CONV_WIDTH, 2 * D_FF)),
                  _const_spec((1, 2 * D_FF)), _const_spec((D_FF, d)), vec, vec],
        out_specs=pl.BlockSpec((tm, d), lambda i: (i, 0)),
        out_shape=jax.ShapeDtypeStruct((s, d), F32),
        scratch_shapes=[pltpu.VMEM((2 * SUBLANES, 2 * D_FF), F32), slab, slab,
                        pltpu.VMEM((tm, D_FF), BF16), pltpu.VMEM((tm, d), F32)],
        compiler_params=pltpu.CompilerParams(dimension_semantics=("arbitrary",),
                                             vmem_limit_bytes=VMEM_LIMIT),
        name="mix_ffn",
    )(fox, ret, x, fox, ret, x, w_fox, w_ret, g1, ln1_g, ln1_b, sc2, sh2, g2, w_up, conv_w, conv_b,
      w_down, ln2_g, ln2_b)


def _placement_tables():
    e_q = np.zeros((LANES, FOX_WIDTH), np.float32)
    e_k = np.zeros((LANES, FOX_WIDTH), np.float32)
    for h in range(FOX_HEADS):
        base = (h // 2) * LANES + (FOX_HEAD_DIM if h % 2 == 0 else 0)
        for j in range(N_PIECES):
            e_q[j * FOX_HEADS + h, base + BIAS_CUM + j] = 1.0
            e_q[ONE_LANE, base + BIAS_ONE + j] = 1.0
            e_k[ONE_LANE, base + BIAS_CUM + j] = 1.0
            e_k[j * FOX_HEADS + h, base + BIAS_ONE + j] = -1.0
    return jnp.asarray(e_q.T, BF16), jnp.asarray(e_k, BF16)


def _rotation_tables(s, tm):
    dk = RET_HEAD_DIM
    inv_freq = ROPE_BASE ** (-np.arange(0, dk, 2, dtype=np.float64) / dk)
    inv_freq = np.concatenate([inv_freq, inv_freq])
    sign = np.concatenate([-np.ones(dk // 2), np.ones(dk // 2)])
    ang_a = (np.arange(s // tm, dtype=np.float64) * tm)[:, None] * inv_freq[None, :]
    ang_b = np.arange(tm, dtype=np.float64)[:, None] * inv_freq[None, :]
    rot_a = np.stack([np.cos(ang_a), np.sin(ang_a)], axis=1)
    rot_b = np.stack([np.cos(ang_b), np.sin(ang_b), sign * np.cos(ang_b), sign * np.sin(ang_b)])
    f32 = lambda a: jnp.asarray(a, F32)
    return f32(rot_a), f32(rot_b), f32(rot_a.transpose(0, 2, 1)), f32(rot_b.transpose(0, 2, 1))


def _decay_tables():
    c = RET_CHUNK
    log_gamma = np.log1p(-np.exp2(-5.0 - np.arange(RET_HEADS, dtype=np.float64)))
    idx = np.arange(c, dtype=np.float64)
    diff = idx[:, None] - idx[None, :]
    inner = np.where(diff[None] >= 0,
                     np.exp(np.maximum(diff, 0.0)[None] * log_gamma[:, None, None]), 0.0)
    xi = np.exp((idx[None, :] + 1.0) * log_gamma[:, None])
    zeta = np.exp((c - 1.0 - idx[None, :]) * log_gamma[:, None])
    g_chunk = np.exp(c * log_gamma)
    bshape = (RET_HEADS, c, RET_HEAD_DIM)
    f32 = lambda a: jnp.asarray(a, F32)
    return (f32(inner),
            f32(np.broadcast_to(xi[:, :, None], bshape)),
            f32(np.broadcast_to(zeta[:, None, :], bshape)),
            f32(np.broadcast_to(g_chunk[:, None, None], (RET_HEADS, 1, RET_HEAD_DIM))))


def kernel(x, c, w_ada, b_ada, w_in, b_f, w_out, ln1_g, ln1_b, w_up, conv_w, conv_b, w_down, ln2_g, ln2_b):
    b, s, d = x.shape
    assert (b, s, d) == (1, SEQ, D_MODEL) and w_ada.shape[0] == DEPTH
    xs = x[0]

    wi = w_in[0]
    o = 0
    parts = {}
    for name, width in (("fq", FOX_WIDTH), ("fk", FOX_WIDTH), ("fv", FOX_WIDTH), ("ff", FOX_HEADS),
                        ("rq", RET_WIDTH), ("rk", RET_WIDTH), ("rv", RET_WIDTH), ("rg", RET_WIDTH)):
        parts[name] = wi[:, o:o + width]
        o += width
    w_all = jnp.concatenate(
        [parts["fk"], jnp.pad(parts["ff"], ((0, 0), (0, LANES - FOX_HEADS))),
         parts["rq"], parts["rv"], parts["rg"]], axis=1).astype(BF16)
    w_t = jnp.concatenate([parts["fq"], parts["fv"], parts["rk"]], axis=1).T.astype(BF16)
    bf_row = jnp.pad(b_f[0][None, :], ((0, 0), (0, LANES - FOX_HEADS)))
    wo = w_out[0].astype(BF16)

    eq_t, e_k = _placement_tables()
    rot_a, rot_b, rot_at, rot_bt = _rotation_tables(s, TM_PROJ)
    inner, xi_b, zeta_b, gc_b = _decay_tables()

    mod = _adaln_mod(c.reshape(d, 1), w_ada[0], b_ada[0][None, :])
    sh1, sc1, g1, sh2, sc2, g2 = [mod[:, i * d:(i + 1) * d] for i in range(6)]

    qt0, qt1, k0, k1, vt, rq, rkt, rv, rg = _in_proj(
        xs, sc1, sh1, w_all, w_t, bf_row, rot_a, rot_b, rot_at, rot_bt, eq_t, e_k)
    fox = _fox_attention(qt0, qt1, k0, k1, vt)
    ret = _retention(rq, rkt, rv, rg, inner, xi_b, zeta_b, gc_b)
    out = _mix_ffn(fox, ret, xs, wo[:FOX_WIDTH], wo[FOX_WIDTH:], g1, ln1_g[0][None, :], ln1_b[0][None, :],
                   sc2, sh2, g2, w_up[0].astype(BF16), conv_w[0], conv_b[0][None, :],
                   w_down[0].astype(BF16), ln2_g[0][None, :], ln2_b[0][None, :])
    return out[None]
```

```python
import math

import jax
import jax.numpy as jnp
import numpy as np
from jax import lax
from jax.experimental import pallas as pl
from jax.experimental.pallas import tpu as pltpu

F32 = jnp.float32
BF16 = jnp.bfloat16

D_MODEL = 1024
SEQ = 16384
FOX_HEADS = 8
FOX_HEAD_DIM = 64
FOX_WIDTH = FOX_HEADS * FOX_HEAD_DIM
RET_HEADS = 4
RET_HEAD_DIM = 128
RET_WIDTH = RET_HEADS * RET_HEAD_DIM
D_FF = 2816
CONV_WIDTH = 3
RET_CHUNK = 128
ROPE_BASE = 10000.0
LN_EPS = 1e-5
GN_EPS = 1e-6
DEPTH = 1
ALPHA = (2 * DEPTH) ** 0.25

LANES = 128
SUBLANES = 8
BF16_ROWS = 16
VMEM_LIMIT = 56 * 1024 * 1024

FOX_PAIRS = FOX_HEADS // 2
BIAS_CUM = 0
BIAS_ONE = 3
N_PIECES = 3
ONE_LANE = N_PIECES * FOX_HEADS
PV_ROWS = FOX_HEAD_DIM + BF16_ROWS
NEG_BIG = -1e30
LOG2_E = math.log2(math.e)
RSQRT2 = 2.0 ** -0.5

COL_K = 0
COL_F = COL_K + FOX_WIDTH
COL_RQ = COL_F + LANES
COL_RV = COL_RQ + RET_WIDTH
COL_RG = COL_RV + RET_WIDTH
N_COLS = COL_RG + RET_WIDTH
ROW_QT = 0
ROW_VT = ROW_QT + FOX_WIDTH
ROW_RKT = ROW_VT + FOX_WIDTH
N_ROWS_T = ROW_RKT + RET_WIDTH

TM_PROJ = 512
TQ = 512
TK = 1024
TK_SUB = 256
PAIRS_PER_TRIP = 2
TR = 1024
TM_FFN = 512
FF_CHUNK = 256
SLAB_PAD_ROWS = 4


def _const_spec(shape):
    return pl.BlockSpec(shape, lambda *_: (0,) * len(shape), pipeline_mode=pl.Buffered(1))


def _split3(x):
    p0 = x.astype(BF16).astype(F32)
    r1 = x - p0
    p1 = r1.astype(BF16).astype(F32)
    p2 = (r1 - p1).astype(BF16).astype(F32)
    return p0, p1, p2


def _layer_norm(y, g, b):
    mu = jnp.mean(y, axis=-1, keepdims=True)
    d = y - mu
    var = jnp.mean(d * d, axis=-1, keepdims=True)
    return d * lax.rsqrt(var + LN_EPS) * g + b


def _mod_kernel(c_ref, w_ref, b_ref, o_ref):
    c = c_ref[...]
    sc = c * jax.nn.sigmoid(c)
    o_ref[...] = jnp.sum(w_ref[...] * sc, axis=0, keepdims=True) + b_ref[...]


def _adaln_mod(c_col, w_ada, b_ada):
    d, n = w_ada.shape
    tn = 2048
    return pl.pallas_call(
        _mod_kernel,
        grid=(n // tn,),
        in_specs=[pl.BlockSpec((d, 1), lambda j: (0, 0)),
                  pl.BlockSpec((d, tn), lambda j: (0, j)),
                  pl.BlockSpec((1, tn), lambda j: (0, j))],
        out_specs=pl.BlockSpec((1, tn), lambda j: (0, j)),
        out_shape=jax.ShapeDtypeStruct((1, n), F32),
        compiler_params=pltpu.CompilerParams(dimension_semantics=("arbitrary",)),
        name="adaln_mod",
    )(c_col, w_ada, b_ada)


def _in_proj_kernel(x_ref, sc_ref, sh_ref, w_ref, wt_ref, bf_ref, ra_ref, rb_ref, rat_ref, rbt_ref,
                    eqt_ref, ek_ref,
                    qt0_out, qt1_out, k0_out, k1_out, vt_out, rq_out, rkt_out, rv_out, rg_out,
                    carry_ref):
    tm = x_ref.shape[0]

    @pl.when(pl.program_id(0) == 0)
    def _():
        carry_ref[...] = jnp.zeros_like(carry_ref)

    h = (x_ref[...] * (1.0 + sc_ref[...]) + sh_ref[...]).astype(BF16)

    def proj(c0, width):
        return jnp.dot(h, w_ref[:, c0:c0 + width], preferred_element_type=F32)

    lane = lax.broadcasted_iota(jnp.int32, (tm, LANES), 1)
    head_lane = lane < FOX_HEADS
    ff = proj(COL_F, LANES) + bf_ref[...]
    k = proj(COL_K, FOX_WIDTH)
    logf = jnp.minimum(ff, 0.0) - jnp.log1p(jnp.exp(-jnp.abs(ff)))
    logf = jnp.where(head_lane, logf, 0.0)
    p0, p1, p2 = _split3(logf)
    pieces = p0 + pltpu.roll(p1, FOX_HEADS, 1) + pltpu.roll(p2, 2 * FOX_HEADS, 1)
    row = lax.broadcasted_iota(jnp.int32, (tm, tm), 0)
    col = lax.broadcasted_iota(jnp.int32, (tm, tm), 1)
    tril = jnp.where(row >= col, 1.0, 0.0).astype(BF16)
    c3 = jnp.dot(tril, pieces.astype(BF16), preferred_element_type=F32)
    tp = lax.dot_general(wt_ref[...], h, (((1,), (1,)), ((), ())), preferred_element_type=F32)
    cum = c3 + pltpu.roll(c3, LANES - FOX_HEADS, 1) + pltpu.roll(c3, LANES - 2 * FOX_HEADS, 1)
    cum = jnp.where(head_lane, cum, 0.0) + carry_ref[0:1, :]
    carry_ref[0:1, :] = cum[tm - 1:tm, :]
    c0_, c1_, c2_ = _split3(cum * LOG2_E)
    cum_pieces = c0_ + pltpu.roll(c1_, FOX_HEADS, 1) + pltpu.roll(c2_, 2 * FOX_HEADS, 1)
    cum_pieces = jnp.where(lane == ONE_LANE, 1.0, cum_pieces)
    kb = jnp.dot(cum_pieces.astype(BF16), ek_ref[...], preferred_element_type=F32)
    qbt = jnp.dot(eqt_ref[...], cum_pieces.T.astype(BF16), preferred_element_type=F32)
    qt = tp[ROW_QT:ROW_QT + FOX_WIDTH] * (FOX_HEAD_DIM ** -0.5 * LOG2_E)
    even_own = (lax.broadcasted_iota(jnp.int32, (1, FOX_WIDTH), 1) % LANES) < FOX_HEAD_DIM
    even_own_t = (lax.broadcasted_iota(jnp.int32, (FOX_WIDTH, 1), 0) % LANES) < FOX_HEAD_DIM
    k0_out[...] = jnp.where(even_own, k, kb).astype(BF16)
    k1_out[...] = jnp.where(even_own, kb, k).astype(BF16)
    qt0_out[...] = jnp.where(even_own_t, qt, qbt).astype(BF16)
    qt1_out[...] = jnp.where(even_own_t, qbt, qt).astype(BF16)
    vt_out[...] = tp[ROW_VT:ROW_VT + FOX_WIDTH].astype(BF16)

    half = RET_HEAD_DIM // 2
    ca, sa = ra_ref[0, 0:1, :], ra_ref[0, 1:2, :]
    cos = ca * rb_ref[0] - sa * rb_ref[1]
    sin = sa * rb_ref[2] + ca * rb_ref[3]
    cat, sat = rat_ref[0, :, 0:1], rat_ref[0, :, 1:2]
    cost = cat * rbt_ref[0] - sat * rbt_ref[1]
    sint = sat * rbt_ref[2] + cat * rbt_ref[3]
    kscale = RET_HEAD_DIM ** -0.5
    two = 2 * RET_HEAD_DIM
    for pair in range(RET_HEADS // 2):
        a2 = proj(COL_RQ + pair * two, two)
        for e in range(2):
            lo = pair * two + e * RET_HEAD_DIM
            a = a2[:, e * RET_HEAD_DIM:(e + 1) * RET_HEAD_DIM]
            rq_out[:, lo:lo + RET_HEAD_DIM] = (a * cos + pltpu.roll(a, half, 1) * sin).astype(BF16)
    for hh in range(RET_HEADS):
        lo = hh * RET_HEAD_DIM
        b = tp[ROW_RKT + lo:ROW_RKT + lo + RET_HEAD_DIM]
        b_rot = jnp.concatenate([b[half:], b[:half]], axis=0)
        rkt_out[lo:lo + RET_HEAD_DIM, :] = ((b * cost + b_rot * sint) * kscale).astype(BF16)
    rv_out[...] = proj(COL_RV, RET_WIDTH).astype(BF16)
    rg_out[...] = proj(COL_RG, RET_WIDTH).astype(BF16)


def _in_proj(x, sc1, sh1, w_all, w_t, bf_row, rot_a, rot_b, rot_at, rot_bt, eq_t, e_k):
    s, d = x.shape
    tm = TM_PROJ
    row = lambda w: pl.BlockSpec((tm, w), lambda i: (i, 0))
    col = lambda h: pl.BlockSpec((h, tm), lambda i: (0, i))
    out_shapes = ([jax.ShapeDtypeStruct((FOX_WIDTH, s), BF16)] * 2
                  + [jax.ShapeDtypeStruct((s, FOX_WIDTH), BF16)] * 2
                  + [jax.ShapeDtypeStruct((FOX_WIDTH, s), BF16),
                     jax.ShapeDtypeStruct((s, RET_WIDTH), BF16),
                     jax.ShapeDtypeStruct((RET_WIDTH, s), BF16)]
                  + [jax.ShapeDtypeStruct((s, RET_WIDTH), BF16)] * 2)
    return pl.pallas_call(
        _in_proj_kernel,
        grid=(s // tm,),
        in_specs=[row(d), _const_spec((1, d)), _const_spec((1, d)),
                  _const_spec((d, N_COLS)), _const_spec((N_ROWS_T, d)), _const_spec((1, LANES)),
                  pl.BlockSpec((1, 2, RET_HEAD_DIM), lambda i: (i, 0, 0)),
                  _const_spec((4, tm, RET_HEAD_DIM)),
                  pl.BlockSpec((1, RET_HEAD_DIM, 2), lambda i: (i, 0, 0)),
                  _const_spec((4, RET_HEAD_DIM, tm)),
                  _const_spec((FOX_WIDTH, LANES)), _const_spec((LANES, FOX_WIDTH))],
        out_specs=([col(FOX_WIDTH)] * 2 + [row(FOX_WIDTH)] * 2
                   + [col(FOX_WIDTH), row(RET_WIDTH), col(RET_WIDTH), row(RET_WIDTH), row(RET_WIDTH)]),
        out_shape=out_shapes,
        scratch_shapes=[pltpu.VMEM((SUBLANES, LANES), F32)],
        compiler_params=pltpu.CompilerParams(dimension_semantics=("arbitrary",),
                                             vmem_limit_bytes=VMEM_LIMIT),
        name="in_proj",
    )(x, sc1, sh1, w_all, w_t, bf_row, rot_a, rot_b, rot_at, rot_bt, eq_t, e_k)


def _fox_kernel(qt0_ref, qt1_ref, qtn0_ref, qtn1_ref, k0_ref, k1_ref, vt_ref, o_ref, m_ref, acc_ref,
                sa_ref, ca_ref, sb_ref, cb_ref):
    i = pl.program_id(1)
    k_refs = (k0_ref, k1_ref)
    qt = [qt0_ref[...], qt1_ref[...]]
    qt_next = [qtn0_ref[...], qtn1_ref[...]]
    m_ref[...] = jnp.full_like(m_ref, NEG_BIG)
    acc_ref[...] = jnp.zeros_like(acc_ref)
    n_sub = TK // TK_SUB

    def qk(e, q_aug, start):
        return jnp.dot(k_refs[e][pl.ds(start, TK_SUB), :], q_aug, preferred_element_type=F32)

    def values(e, start, width):
        return jnp.concatenate(
            [vt_ref[e * FOX_HEAD_DIM:(e + 1) * FOX_HEAD_DIM, pl.ds(start, width)],
             jnp.ones((BF16_ROWS, width), BF16)], axis=0)

    def pv(e, st, m_new, start, width):
        pt = jnp.exp2(st - m_new).astype(BF16)
        return jnp.dot(values(e, start, width), pt, preferred_element_type=F32)

    def score_sub(e, q_aug, j, sub, s_ref, cmax):
        st = qk(e, q_aug, pl.multiple_of(j * TK + sub * TK_SUB, TK_SUB))
        s_ref[e, sub * TK_SUB:(sub + 1) * TK_SUB, :] = st
        cm = jnp.max(st, axis=0, keepdims=True)
        return cm if cmax is None else jnp.maximum(cmax, cm)

    def scores(j, e, s_ref, c_ref):
        cmax = None
        for sub in range(n_sub):
            cmax = score_sub(e, qt[e], j, sub, s_ref, cmax)
        c_ref[e] = cmax

    def stage(e, j_next, buf_next, j_cur, buf_cur):
        (s_next, c_next), (s_cur, c_cur) = buf_next, buf_cur
        m_prev = m_ref[e]
        m_new = jnp.maximum(m_prev, c_cur[e])
        acc = jnp.exp2(m_prev - m_new) * acc_ref[e]
        cmax = None
        for sub in range(n_sub):
            rows = slice(sub * TK_SUB, (sub + 1) * TK_SUB)
            cmax = score_sub(e, qt[e], j_next, sub, s_next, cmax)
            acc = acc + pv(e, s_cur[e, rows, :], m_new,
                           pl.multiple_of(j_cur * TK + sub * TK_SUB, TK_SUB), TK_SUB)
        c_next[e] = cmax
        acc_ref[e] = acc
        m_ref[e] = m_new

    def finish(j, s_ref, fill_next):
        n_pieces = TK // TQ
        tri = (lax.broadcasted_iota(jnp.int32, (TQ, TQ), 0)
               <= lax.broadcasted_iota(jnp.int32, (TQ, TQ), 1))
        for diag in range(n_pieces):
            @pl.when(i % n_pieces == diag)
            def _():
                for e in range(2):
                    cmax = None
                    for piece in range(n_pieces):
                        if fill_next:
                            for sub in range(piece * n_sub // n_pieces, (piece + 1) * n_sub // n_pieces):
                                cmax = score_sub(e, qt_next[e], 0, sub, sa_ref, cmax)
                        if piece > diag:
                            continue
                        st = s_ref[e, piece * TQ:(piece + 1) * TQ, :]
                        if piece == diag:
                            st = jnp.where(tri, st, NEG_BIG)
                        m_prev = m_ref[e]
                        m_new = jnp.maximum(m_prev, jnp.max(st, axis=0, keepdims=True))
                        start = pl.multiple_of(j * TK + piece * TQ, TQ)
                        acc_ref[e] = (jnp.exp2(m_prev - m_new) * acc_ref[e]
                                      + pv(e, st, m_new, start, TQ))
                        m_ref[e] = m_new
                    if fill_next:
                        ca_ref[e] = cmax

    buf_a = (sa_ref, ca_ref)
    buf_b = (sb_ref, cb_ref)

    nfull = (i * TQ) // TK
    prefilled = jnp.logical_and(i > 0, (((i - 1) * TQ) // TK) % 2 == 1)

    @pl.when(jnp.logical_not(prefilled))
    def _():
        for e in range(2):
            scores(0, e, *buf_a)

    def pair(jj, carry):
        j = 2 * jj
        for e in range(2):
            stage(e, j + 1, buf_b, j, buf_a)
        for e in range(2):
            stage(e, j + 2, buf_a, j + 1, buf_b)
        return carry

    def group(jg, carry):
        for u in range(PAIRS_PER_TRIP):
            carry = pair(PAIRS_PER_TRIP * jg + u, carry)
        return carry

    n_pairs = nfull // 2
    n_groups = n_pairs // PAIRS_PER_TRIP
    lax.fori_loop(0, n_groups, group, 0)
    lax.fori_loop(PAIRS_PER_TRIP * n_groups, n_pairs, pair, 0)

    @pl.when(nfull % 2 == 1)
    def _():
        for e in range(2):
            stage(e, nfull, buf_b, nfull - 1, buf_a)
        finish(nfull, sb_ref, True)

    @pl.when(nfull % 2 == 0)
    def _():
        finish(nfull, sa_ref, False)

    outs = []
    for e in range(2):
        acc = acc_ref[e]
        outs.append(acc[:FOX_HEAD_DIM] / acc[FOX_HEAD_DIM:FOX_HEAD_DIM + 1])
    o_ref[...] = jnp.concatenate(outs, axis=0).T.astype(o_ref.dtype)


def _fox_attention(qt0, qt1, k0, k1, vt):
    s = k0.shape[0]
    assert TK % TQ == 0 and TK % TK_SUB == 0 and s % TK == 0
    n_q = s // TQ
    blk_t = pl.BlockSpec((LANES, TQ), lambda p, i: (p, i))
    nxt_t = pl.BlockSpec((LANES, TQ), lambda p, i: (p, jnp.minimum(i + 1, n_q - 1)))
    res = pl.BlockSpec((s, LANES), lambda p, i: (0, p))
    return pl.pallas_call(
        _fox_kernel,
        grid=(FOX_PAIRS, n_q),
        in_specs=[blk_t, blk_t, nxt_t, nxt_t, res, res, pl.BlockSpec((LANES, s), lambda p, i: (p, 0))],
        out_specs=pl.BlockSpec((TQ, LANES), lambda p, i: (i, p)),
        out_shape=jax.ShapeDtypeStruct((s, FOX_WIDTH), BF16),
        scratch_shapes=[pltpu.VMEM((2, 1, TQ), F32), pltpu.VMEM((2, PV_ROWS, TQ), F32),
                        pltpu.VMEM((2, TK, TQ), F32), pltpu.VMEM((2, 1, TQ), F32),
                        pltpu.VMEM((2, TK, TQ), F32), pltpu.VMEM((2, 1, TQ), F32)],
        compiler_params=pltpu.CompilerParams(dimension_semantics=("arbitrary", "arbitrary"),
                                             vmem_limit_bytes=VMEM_LIMIT),
        name="fox_attention",
    )(qt0, qt1, qt0, qt1, k0, k1, vt)


def _ret_kernel(q_ref, kt_ref, v_ref, g_ref, inner_ref, xi_ref, zeta_ref, gc_ref, o_ref, r_ref):
    @pl.when(pl.program_id(0) == 0)
    def _():
        r_ref[...] = jnp.zeros_like(r_ref)

    c = RET_CHUNK
    n_chunks = q_ref.shape[0] // c
    tiles = [(hh, ci) for hh in range(RET_HEADS) for ci in range(n_chunks)]

    def head_cols(hh):
        return slice(hh * RET_HEAD_DIM, (hh + 1) * RET_HEAD_DIM)

    def chunk_rows(ci):
        return slice(ci * c, (ci + 1) * c)

    intra, kv = {}, {}
    for hh, ci in tiles:
        q = q_ref[chunk_rows(ci), head_cols(hh)]
        kt = kt_ref[head_cols(hh), chunk_rows(ci)]
        v = v_ref[chunk_rows(ci), head_cols(hh)]
        intra[hh, ci] = (jnp.dot(q, kt, preferred_element_type=F32) * inner_ref[hh]).astype(BF16)
        kzt = (kt.astype(F32) * zeta_ref[hh]).astype(BF16)
        kv[hh, ci] = jnp.dot(kzt, v, preferred_element_type=F32)

    state = {}
    for hh in range(RET_HEADS):
        r_state = r_ref[hh]
        for ci in range(n_chunks):
            state[hh, ci] = r_state.astype(BF16)
            r_state = r_state * gc_ref[hh] + kv[hh, ci]
        r_ref[hh] = r_state

    for hh, ci in tiles:
        q = q_ref[chunk_rows(ci), head_cols(hh)]
        v = v_ref[chunk_rows(ci), head_cols(hh)]
        o = (jnp.dot(intra[hh, ci], v, preferred_element_type=F32)
             + jnp.dot(q, state[hh, ci], preferred_element_type=F32) * xi_ref[hh])
        mu = jnp.mean(o, axis=-1, keepdims=True)
        d = o - mu
        var = jnp.mean(d * d, axis=-1, keepdims=True)
        on = d * lax.rsqrt(var + GN_EPS)
        g = g_ref[chunk_rows(ci), head_cols(hh)].astype(F32)
        o_ref[chunk_rows(ci), head_cols(hh)] = (g * jax.nn.sigmoid(g) * on).astype(o_ref.dtype)


def _retention(rq, rkt, rv, rg, inner, xi_b, zeta_b, gc_b):
    s = rq.shape[0]
    blk = pl.BlockSpec((TR, RET_WIDTH), lambda i: (i, 0))
    blk_t = pl.BlockSpec((RET_WIDTH, TR), lambda i: (0, i))
    tab = _const_spec((RET_HEADS, RET_CHUNK, RET_CHUNK))
    return pl.pallas_call(
        _ret_kernel,
        grid=(s // TR,),
        in_specs=[blk, blk_t, blk, blk, tab, tab, tab, _const_spec((RET_HEADS, 1, RET_HEAD_DIM))],
        out_specs=blk,
        out_shape=jax.ShapeDtypeStruct((s, RET_WIDTH), BF16),
        scratch_shapes=[pltpu.VMEM((RET_HEADS, RET_HEAD_DIM, RET_HEAD_DIM), F32)],
        compiler_params=pltpu.CompilerParams(dimension_semantics=("arbitrary",)),
        name="retention",
    )(rq, rkt, rv, rg, inner, xi_b, zeta_b, gc_b)


def _mix_ffn_kernel(fox_ref, ret_ref, x_ref, fox0_ref, ret0_ref, x0_ref,
                    wf_ref, wr_ref, g1_ref, l1g_ref, l1b_ref,
                    sc_ref, sh_ref, g2_ref, wu_ref, cw_ref, cb_ref, wd_ref, l2g_ref, l2b_ref,
                    o_ref, carry_ref, slab_in_ref, slab_out_ref, y_ref, x1_ref):
    tm = x_ref.shape[0]
    grp = tm // SUBLANES
    pitch = grp + SLAB_PAD_ROWS
    n_slab = D_MODEL // LANES

    def front(fox, ret, x):
        mix = (jnp.dot(fox, wf_ref[...], preferred_element_type=F32)
               + jnp.dot(ret, wr_ref[...], preferred_element_type=F32))
        x1_nat = _layer_norm(ALPHA * x + g1_ref[...] * mix, l1g_ref[...], l1b_ref[...])
        for c in range(n_slab):
            for s in range(SUBLANES):
                slab_in_ref[c, s * pitch:s * pitch + grp, :] = x1_nat[s * grp:(s + 1) * grp,
                                                                      c * LANES:(c + 1) * LANES]
        return jnp.concatenate(
            [jnp.concatenate([slab_in_ref[c, pl.ds(v, SUBLANES, stride=pitch), :] for v in range(grp)],
                             axis=0)
             for c in range(n_slab)], axis=1)

    @pl.when(pl.program_id(0) == 0)
    def _():
        carry_ref[...] = jnp.zeros_like(carry_ref)
        x1_ref[...] = front(fox0_ref[...], ret0_ref[...], x0_ref[...])

    h = (x1_ref[...] * (1.0 + sc_ref[...]) + sh_ref[...]).astype(BF16)
    first_sublane = lax.broadcasted_iota(jnp.int32, (SUBLANES, FF_CHUNK), 0) == 0

    def conv_up(c0):
        cols = slice(c0, c0 + FF_CHUNK)
        up = jnp.dot(h, wu_ref[:, cols], preferred_element_type=F32)
        prev = carry_ref[:, cols]
        tail = []
        for g in range(2):
            rows = slice(tm - (2 - g) * SUBLANES, tm - (1 - g) * SUBLANES)
            tail.append(jnp.where(first_sublane,
                                  pltpu.roll(prev[g * SUBLANES:(g + 1) * SUBLANES], 1, 0),
                                  pltpu.roll(up[rows], 1, 0)))
        carry_ref[:, cols] = up[tm - 2 * SUBLANES:tm]
        back1 = jnp.concatenate([tail[1], up[:tm - SUBLANES]], axis=0)
        back2 = jnp.concatenate([tail[0], tail[1], up[:tm - 2 * SUBLANES]], axis=0)
        cw = cw_ref[:, cols] * RSQRT2
        return cb_ref[:, cols] * RSQRT2 + back2 * cw[0:1] + back1 * cw[1:2] + up * cw[2:3]

    for ci in range(D_FF // FF_CHUNK):
        a = conv_up(ci * FF_CHUNK)
        b = conv_up(D_FF + ci * FF_CHUNK)
        y_ref[:, ci * FF_CHUNK:(ci + 1) * FF_CHUNK] = (a * (1.0 + lax.erf(a)) * b).astype(BF16)

    x1_next = front(fox_ref[...], ret_ref[...], x_ref[...])

    ffn = jnp.dot(y_ref[...], wd_ref[...], preferred_element_type=F32)
    out = _layer_norm(ALPHA * x1_ref[...] + g2_ref[...] * ffn, l2g_ref[...], l2b_ref[...])
    x1_ref[...] = x1_next

    for c in range(n_slab):
        for v in range(grp):
            slab_out_ref[c, pl.ds(v, SUBLANES, stride=pitch), :] = out[v * SUBLANES:(v + 1) * SUBLANES,
                                                                       c * LANES:(c + 1) * LANES]
    for c in range(n_slab):
        for s in range(SUBLANES):
            o_ref[s * grp:(s + 1) * grp, c * LANES:(c + 1) * LANES] = slab_out_ref[c, s * pitch:s * pitch + grp, :]


def _mix_ffn(fox, ret, x, w_fox, w_ret, g1, ln1_g, ln1_b, sc2, sh2, g2, w_up, conv_w, conv_b, w_down,
             ln2_g, ln2_b):
    s, d = x.shape
    tm = TM_FFN
    n_tiles = s // tm
    nxt = lambda w: pl.BlockSpec((tm, w), lambda i: (jnp.minimum(i + 1, n_tiles - 1), 0))
    vec = _const_spec((1, d))
    slab = pltpu.VMEM((d // LANES, tm + SUBLANES * SLAB_PAD_ROWS, LANES), F32)
    return pl.pallas_call(
        _mix_ffn_kernel,
        grid=(n_tiles,),
        in_specs=[nxt(FOX_WIDTH), nxt(RET_WIDTH), nxt(d),
                  _const_spec((tm, FOX_WIDTH)), _const_spec((tm, RET_WIDTH)), _const_spec((tm, d)),
                  _const_spec((FOX_WIDTH, d)), _const_spec((RET_WIDTH, d)), vec, vec, vec,
                  vec, vec, vec,
                  _const_spec((d, 2 * D_FF)), _const_spec((CONV_WIDTH, 2 * D_FF)),
                  _const_spec((1, 2 * D_FF)), _const_spec((D_FF, d)), vec, vec],
        out_specs=pl.BlockSpec((tm, d), lambda i: (i, 0)),
        out_shape=jax.ShapeDtypeStruct((s, d), F32),
        scratch_shapes=[pltpu.VMEM((2 * SUBLANES, 2 * D_FF), F32), slab, slab,
                        pltpu.VMEM((tm, D_FF), BF16), pltpu.VMEM((tm, d), F32)],
        compiler_params=pltpu.CompilerParams(dimension_semantics=("arbitrary",),
                                             vmem_limit_bytes=VMEM_LIMIT),
        name="mix_ffn",
    )(fox, ret, x, fox, ret, x, w_fox, w_ret, g1, ln1_g, ln1_b, sc2, sh2, g2, w_up, conv_w, conv_b,
      w_down, ln2_g, ln2_b)


def _placement_tables():
    e_q = np.zeros((LANES, FOX_WIDTH), np.float32)
    e_k = np.zeros((LANES, FOX_WIDTH), np.float32)
    for h in range(FOX_HEADS):
        base = (h // 2) * LANES + (FOX_HEAD_DIM if h % 2 == 0 else 0)
        for j in range(N_PIECES):
            e_q[j * FOX_HEADS + h, base + BIAS_CUM + j] = 1.0
            e_q[ONE_LANE, base + BIAS_ONE + j] = 1.0
            e_k[ONE_LANE, base + BIAS_CUM + j] = 1.0
            e_k[j * FOX_HEADS + h, base + BIAS_ONE + j] = -1.0
    return jnp.asarray(e_q.T, BF16), jnp.asarray(e_k, BF16)


def _rotation_tables(s, tm):
    dk = RET_HEAD_DIM
    inv_freq = ROPE_BASE ** (-np.arange(0, dk, 2, dtype=np.float64) / dk)
    inv_freq = np.concatenate([inv_freq, inv_freq])
    sign = np.concatenate([-np.ones(dk // 2), np.ones(dk // 2)])
    ang_a = (np.arange(s // tm, dtype=np.float64) * tm)[:, None] * inv_freq[None, :]
    ang_b = np.arange(tm, dtype=np.float64)[:, None] * inv_freq[None, :]
    rot_a = np.stack([np.cos(ang_a), np.sin(ang_a)], axis=1)
    rot_b = np.stack([np.cos(ang_b), np.sin(ang_b), sign * np.cos(ang_b), sign * np.sin(ang_b)])
    f32 = lambda a: jnp.asarray(a, F32)
    return f32(rot_a), f32(rot_b), f32(rot_a.transpose(0, 2, 1)), f32(rot_b.transpose(0, 2, 1))


def _decay_tables():
    c = RET_CHUNK
    log_gamma = np.log1p(-np.exp2(-5.0 - np.arange(RET_HEADS, dtype=np.float64)))
    idx = np.arange(c, dtype=np.float64)
    diff = idx[:, None] - idx[None, :]
    inner = np.where(diff[None] >= 0,
                     np.exp(np.maximum(diff, 0.0)[None] * log_gamma[:, None, None]), 0.0)
    xi = np.exp((idx[None, :] + 1.0) * log_gamma[:, None])
    zeta = np.exp((c - 1.0 - idx[None, :]) * log_gamma[:, None])
    g_chunk = np.exp(c * log_gamma)
    bshape = (RET_HEADS, c, RET_HEAD_DIM)
    f32 = lambda a: jnp.asarray(a, F32)
    return (f32(inner),
            f32(np.broadcast_to(xi[:, :, None], bshape)),
            f32(np.broadcast_to(zeta[:, None, :], bshape)),
            f32(np.broadcast_to(g_chunk[:, None, None], (RET_HEADS, 1, RET_HEAD_DIM))))


def kernel(x, c, w_ada, b_ada, w_in, b_f, w_out, ln1_g, ln1_b, w_up, conv_w, conv_b, w_down, ln2_g, ln2_b):
    b, s, d = x.shape
    assert (b, s, d) == (1, SEQ, D_MODEL) and w_ada.shape[0] == DEPTH
    xs = x[0]

    wi = w_in[0]
    o = 0
    parts = {}
    for name, width in (("fq", FOX_WIDTH), ("fk", FOX_WIDTH), ("fv", FOX_WIDTH), ("ff", FOX_HEADS),
                        ("rq", RET_WIDTH), ("rk", RET_WIDTH), ("rv", RET_WIDTH), ("rg", RET_WIDTH)):
        parts[name] = wi[:, o:o + width]
        o += width
    w_all = jnp.concatenate(
        [parts["fk"], jnp.pad(parts["ff"], ((0, 0), (0, LANES - FOX_HEADS))),
         parts["rq"], parts["rv"], parts["rg"]], axis=1).astype(BF16)
    w_t = jnp.concatenate([parts["fq"], parts["fv"], parts["rk"]], axis=1).T.astype(BF16)
    bf_row = jnp.pad(b_f[0][None, :], ((0, 0), (0, LANES - FOX_HEADS)))
    wo = w_out[0].astype(BF16)

    eq_t, e_k = _placement_tables()
    rot_a, rot_b, rot_at, rot_bt = _rotation_tables(s, TM_PROJ)
    inner, xi_b, zeta_b, gc_b = _decay_tables()

    mod = _adaln_mod(c.reshape(d, 1), w_ada[0], b_ada[0][None, :])
    sh1, sc1, g1, sh2, sc2, g2 = [mod[:, i * d:(i + 1) * d] for i in range(6)]

    qt0, qt1, k0, k1, vt, rq, rkt, rv, rg = _in_proj(
        xs, sc1, sh1, w_all, w_t, bf_row, rot_a, rot_b, rot_at, rot_bt, eq_t, e_k)
    fox = _fox_attention(qt0, qt1, k0, k1, vt)
    ret = _retention(rq, rkt, rv, rg, inner, xi_b, zeta_b, gc_b)
    out = _mix_ffn(fox, ret, xs, wo[:FOX_WIDTH], wo[FOX_WIDTH:], g1, ln1_g[0][None, :], ln1_b[0][None, :],
                   sc2, sh2, g2, w_up[0].astype(BF16), conv_w[0], conv_b[0][None, :],
                   w_down[0].astype(BF16), ln2_g[0][None, :], ln2_b[0][None, :])
    return out[None]
```

```python
import math

import jax
import jax.numpy as jnp
import numpy as np
from jax import lax
from jax.experimental import pallas as pl
from jax.experimental.pallas import tpu as pltpu

F32 = jnp.float32
BF16 = jnp.bfloat16

D_MODEL = 1024
SEQ = 16384
FOX_HEADS = 8
FOX_HEAD_DIM = 64
FOX_WIDTH = FOX_HEADS * FOX_HEAD_DIM
RET_HEADS = 4
RET_HEAD_DIM = 128
RET_WIDTH = RET_HEADS * RET_HEAD_DIM
D_FF = 2816
CONV_WIDTH = 3
RET_CHUNK = 128
ROPE_BASE = 10000.0
LN_EPS = 1e-5
GN_EPS = 1e-6
DEPTH = 1
ALPHA = (2 * DEPTH) ** 0.25

LANES = 128
SUBLANES = 8
BF16_ROWS = 16
VMEM_LIMIT = 56 * 1024 * 1024

FOX_PAIRS = FOX_HEADS // 2
BIAS_CUM = 0
BIAS_ONE = 3
N_PIECES = 3
ONE_LANE = N_PIECES * FOX_HEADS
PV_ROWS = FOX_HEAD_DIM + BF16_ROWS
NEG_BIG = -1e30
LOG2_E = math.log2(math.e)
RSQRT2 = 2.0 ** -0.5

COL_K = 0
COL_F = COL_K + FOX_WIDTH
COL_RQ = COL_F + LANES
COL_RV = COL_RQ + RET_WIDTH
COL_RG = COL_RV + RET_WIDTH
N_COLS = COL_RG + RET_WIDTH
ROW_QT = 0
ROW_VT = ROW_QT + FOX_WIDTH
ROW_RKT = ROW_VT + FOX_WIDTH
N_ROWS_T = ROW_RKT + RET_WIDTH

TM_PROJ = 512
TQ = 512
TK = 1024
TK_SUB = 256
PAIRS_PER_TRIP = 2
TR = 1024
TM_FFN = 512
FF_CHUNK = 256
SLAB_PAD_ROWS = 4


def _const_spec(shape):
    return pl.BlockSpec(shape, lambda *_: (0,) * len(shape), pipeline_mode=pl.Buffered(1))


def _split3(x):
    p0 = x.astype(BF16).astype(F32)
    r1 = x - p0
    p1 = r1.astype(BF16).astype(F32)
    p2 = (r1 - p1).astype(BF16).astype(F32)
    return p0, p1, p2


def _layer_norm(y, g, b):
    mu = jnp.mean(y, axis=-1, keepdims=True)
    d = y - mu
    var = jnp.mean(d * d, axis=-1, keepdims=True)
    return d * lax.rsqrt(var + LN_EPS) * g + b


def _mod_kernel(c_ref, w_ref, b_ref, o_ref):
    c = c_ref[...]
    sc = c * jax.nn.sigmoid(c)
    o_ref[...] = jnp.sum(w_ref[...] * sc, axis=0, keepdims=True) + b_ref[...]


def _adaln_mod(c_col, w_ada, b_ada):
    d, n = w_ada.shape
    tn = 2048
    return pl.pallas_call(
        _mod_kernel,
        grid=(n // tn,),
        in_specs=[pl.BlockSpec((d, 1), lambda j: (0, 0)),
                  pl.BlockSpec((d, tn), lambda j: (0, j)),
                  pl.BlockSpec((1, tn), lambda j: (0, j))],
        out_specs=pl.BlockSpec((1, tn), lambda j: (0, j)),
        out_shape=jax.ShapeDtypeStruct((1, n), F32),
        compiler_params=pltpu.CompilerParams(dimension_semantics=("arbitrary",)),
        name="adaln_mod",
    )(c_col, w_ada, b_ada)


def _in_proj_kernel(x_ref, sc_ref, sh_ref, w_ref, wt_ref, bf_ref, ra_ref, rb_ref, rat_ref, rbt_ref,
                    eqt_ref, ek_ref,
                    qt0_out, qt1_out, k0_out, k1_out, vt_out, rq_out, rkt_out, rv_out, rg_out,
                    carry_ref):
    tm = x_ref.shape[0]

    @pl.when(pl.program_id(0) == 0)
    def _():
        carry_ref[...] = jnp.zeros_like(carry_ref)

    h = (x_ref[...] * (1.0 + sc_ref[...]) + sh_ref[...]).astype(BF16)

    def proj(c0, width):
        return jnp.dot(h, w_ref[:, c0:c0 + width], preferred_element_type=F32)

    lane = lax.broadcasted_iota(jnp.int32, (tm, LANES), 1)
    head_lane = lane < FOX_HEADS
    ff = proj(COL_F, LANES) + bf_ref[...]
    k = proj(COL_K, FOX_WIDTH)
    logf = jnp.minimum(ff, 0.0) - jnp.log1p(jnp.exp(-jnp.abs(ff)))
    logf = jnp.where(head_lane, logf, 0.0)
    p0, p1, p2 = _split3(logf)
    pieces = p0 + pltpu.roll(p1, FOX_HEADS, 1) + pltpu.roll(p2, 2 * FOX_HEADS, 1)
    row = lax.broadcasted_iota(jnp.int32, (tm, tm), 0)
    col = lax.broadcasted_iota(jnp.int32, (tm, tm), 1)
    tril = jnp.where(row >= col, 1.0, 0.0).astype(BF16)
    c3 = jnp.dot(tril, pieces.astype(BF16), preferred_element_type=F32)
    tp = lax.dot_general(wt_ref[...], h, (((1,), (1,)), ((), ())), preferred_element_type=F32)
    cum = c3 + pltpu.roll(c3, LANES - FOX_HEADS, 1) + pltpu.roll(c3, LANES - 2 * FOX_HEADS, 1)
    cum = jnp.where(head_lane, cum, 0.0) + carry_ref[0:1, :]
    carry_ref[0:1, :] = cum[tm - 1:tm, :]
    c0_, c1_, c2_ = _split3(cum * LOG2_E)
    cum_pieces = c0_ + pltpu.roll(c1_, FOX_HEADS, 1) + pltpu.roll(c2_, 2 * FOX_HEADS, 1)
    cum_pieces = jnp.where(lane == ONE_LANE, 1.0, cum_pieces)
    kb = jnp.dot(cum_pieces.astype(BF16), ek_ref[...], preferred_element_type=F32)
    qbt = jnp.dot(eqt_ref[...], cum_pieces.T.astype(BF16), preferred_element_type=F32)
    qt = tp[ROW_QT:ROW_QT + FOX_WIDTH] * (FOX_HEAD_DIM ** -0.5 * LOG2_E)
    even_own = (lax.broadcasted_iota(jnp.int32, (1, FOX_WIDTH), 1) % LANES) < FOX_HEAD_DIM
    even_own_t = (lax.broadcasted_iota(jnp.int32, (FOX_WIDTH, 1), 0) % LANES) < FOX_HEAD_DIM
    k0_out[...] = jnp.where(even_own, k, kb).astype(BF16)
    k1_out[...] = jnp.where(even_own, kb, k).astype(BF16)
    qt0_out[...] = jnp.where(even_own_t, qt, qbt).astype(BF16)
    qt1_out[...] = jnp.where(even_own_t, qbt, qt).astype(BF16)
    vt_out[...] = tp[ROW_VT:ROW_VT + FOX_WIDTH].astype(BF16)

    half = RET_HEAD_DIM // 2
    ca, sa = ra_ref[0, 0:1, :], ra_ref[0, 1:2, :]
    cos = ca * rb_ref[0] - sa * rb_ref[1]
    sin = sa * rb_ref[2] + ca * rb_ref[3]
    cat, sat = rat_ref[0, :, 0:1], rat_ref[0, :, 1:2]
    cost = cat * rbt_ref[0] - sat * rbt_ref[1]
    sint = sat * rbt_ref[2] + cat * rbt_ref[3]
    kscale = RET_HEAD_DIM ** -0.5
    two = 2 * RET_HEAD_DIM
    for pair in range(RET_HEADS // 2):
        a2 = proj(COL_RQ + pair * two, two)
        for e in range(2):
            lo = pair * two + e * RET_HEAD_DIM
            a = a2[:, e * RET_HEAD_DIM:(e + 1) * RET_HEAD_DIM]
            rq_out[:, lo:lo + RET_HEAD_DIM] = (a * cos + pltpu.roll(a, half, 1) * sin).astype(BF16)
    for hh in range(RET_HEADS):
        lo = hh * RET_HEAD_DIM
        b = tp[ROW_RKT + lo:ROW_RKT + lo + RET_HEAD_DIM]
        b_rot = jnp.concatenate([b[half:], b[:half]], axis=0)
        rkt_out[lo:lo + RET_HEAD_DIM, :] = ((b * cost + b_rot * sint) * kscale).astype(BF16)
    rv_out[...] = proj(COL_RV, RET_WIDTH).astype(BF16)
    rg_out[...] = proj(COL_RG, RET_WIDTH).astype(BF16)


def _in_proj(x, sc1, sh1, w_all, w_t, bf_row, rot_a, rot_b, rot_at, rot_bt, eq_t, e_k):
    s, d = x.shape
    tm = TM_PROJ
    row = lambda w: pl.BlockSpec((tm, w), lambda i: (i, 0))
    col = lambda h: pl.BlockSpec((h, tm), lambda i: (0, i))
    out_shapes = ([jax.ShapeDtypeStruct((FOX_WIDTH, s), BF16)] * 2
                  + [jax.ShapeDtypeStruct((s, FOX_WIDTH), BF16)] * 2
                  + [jax.ShapeDtypeStruct((FOX_WIDTH, s), BF16),
                     jax.ShapeDtypeStruct((s, RET_WIDTH), BF16),
                     jax.ShapeDtypeStruct((RET_WIDTH, s), BF16)]
                  + [jax.ShapeDtypeStruct((s, RET_WIDTH), BF16)] * 2)
    return pl.pallas_call(
        _in_proj_kernel,
        grid=(s // tm,),
        in_specs=[row(d), _const_spec((1, d)), _const_spec((1, d)),
                  _const_spec((d, N_COLS)), _const_spec((N_ROWS_T, d)), _const_spec((1, LANES)),
                  pl.BlockSpec((1, 2, RET_HEAD_DIM), lambda i: (i, 0, 0)),
                  _const_spec((4, tm, RET_HEAD_DIM)),
                  pl.BlockSpec((1, RET_HEAD_DIM, 2), lambda i: (i, 0, 0)),
                  _const_spec((4, RET_HEAD_DIM, tm)),
                  _const_spec((FOX_WIDTH, LANES)), _const_spec((LANES, FOX_WIDTH))],
        out_specs=([col(FOX_WIDTH)] * 2 + [row(FOX_WIDTH)] * 2
                   + [col(FOX_WIDTH), row(RET_WIDTH), col(RET_WIDTH), row(RET_WIDTH), row(RET_WIDTH)]),
        out_shape=out_shapes,
        scratch_shapes=[pltpu.VMEM((SUBLANES, LANES), F32)],
        compiler_params=pltpu.CompilerParams(dimension_semantics=("arbitrary",),
                                             vmem_limit_bytes=VMEM_LIMIT),
        name="in_proj",
    )(x, sc1, sh1, w_all, w_t, bf_row, rot_a, rot_b, rot_at, rot_bt, eq_t, e_k)


def _fox_kernel(qt0_ref, qt1_ref, qtn0_ref, qtn1_ref, k0_ref, k1_ref, vt_ref, o_ref, m_ref, acc_ref,
                sa_ref, ca_ref, sb_ref, cb_ref, sc_ref, cc_ref):
    i = pl.program_id(1)
    k_refs = (k0_ref, k1_ref)
    qt = [qt0_ref[...], qt1_ref[...]]
    qt_next = [qtn0_ref[...], qtn1_ref[...]]
    m_ref[...] = jnp.full_like(m_ref, NEG_BIG)
    acc_ref[...] = jnp.zeros_like(acc_ref)
    n_sub = TK // TK_SUB

    def qk(e, q_aug, start):
        return jnp.dot(k_refs[e][pl.ds(start, TK_SUB), :], q_aug, preferred_element_type=F32)

    def values(e, start, width):
        return jnp.concatenate(
            [vt_ref[e * FOX_HEAD_DIM:(e + 1) * FOX_HEAD_DIM, pl.ds(start, width)],
             jnp.ones((BF16_ROWS, width), BF16)], axis=0)

    def pv(e, st, m_new, start, width):
        pt = jnp.exp2(st - m_new).astype(BF16)
        return jnp.dot(values(e, start, width), pt, preferred_element_type=F32)

    def score_sub(e, q_aug, j, sub, s_ref, cmax):
        st = qk(e, q_aug, pl.multiple_of(j * TK + sub * TK_SUB, TK_SUB))
        s_ref[e, sub * TK_SUB:(sub + 1) * TK_SUB, :] = st
        cm = jnp.max(st, axis=0, keepdims=True)
        return cm if cmax is None else jnp.maximum(cmax, cm)

    def scores(j, e, s_ref, c_ref):
        cmax = None
        for sub in range(n_sub):
            cmax = score_sub(e, qt[e], j, sub, s_ref, cmax)
        c_ref[e] = cmax

    def stage(e, j_next, buf_next, j_cur, buf_cur):
        (s_next, c_next), (s_cur, c_cur) = buf_next, buf_cur
        m_prev = m_ref[e]
        m_new = jnp.maximum(m_prev, c_cur[e])
        acc = jnp.exp2(m_prev - m_new) * acc_ref[e]
        cmax = None
        for sub in range(n_sub):
            rows = slice(sub * TK_SUB, (sub + 1) * TK_SUB)
            cmax = score_sub(e, qt[e], j_next, sub, s_next, cmax)
            acc = acc + pv(e, s_cur[e, rows, :], m_new,
                           pl.multiple_of(j_cur * TK + sub * TK_SUB, TK_SUB), TK_SUB)
        c_next[e] = cmax
        acc_ref[e] = acc
        m_ref[e] = m_new

    def finish(j, s_ref, fill_next):
        n_pieces = TK // TQ
        tri = (lax.broadcasted_iota(jnp.int32, (TQ, TQ), 0)
               <= lax.broadcasted_iota(jnp.int32, (TQ, TQ), 1))
        for diag in range(n_pieces):
            @pl.when(i % n_pieces == diag)
            def _():
                for e in range(2):
                    cmax = None
                    for piece in range(n_pieces):
                        if fill_next:
                            for sub in range(piece * n_sub // n_pieces, (piece + 1) * n_sub // n_pieces):
                                cmax = score_sub(e, qt_next[e], 0, sub, sc_ref, cmax)
                        if piece > diag:
                            continue
                        st = s_ref[e, piece * TQ:(piece + 1) * TQ, :]
                        if piece == diag:
                            st = jnp.where(tri, st, NEG_BIG)
                        m_prev = m_ref[e]
                        m_new = jnp.maximum(m_prev, jnp.max(st, axis=0, keepdims=True))
                        start = pl.multiple_of(j * TK + piece * TQ, TQ)
                        acc_ref[e] = (jnp.exp2(m_prev - m_new) * acc_ref[e]
                                      + pv(e, st, m_new, start, TQ))
                        m_ref[e] = m_new
                    if fill_next:
                        cc_ref[e] = cmax

    buf_a = (sa_ref, ca_ref)
    buf_b = (sb_ref, cb_ref)
    buf_c = (sc_ref, cc_ref)

    nfull = (i * TQ) // TK
    prefilled = jnp.logical_and(i > 0, ((i - 1) * TQ) // TK >= 1)

    @pl.when(jnp.logical_not(prefilled))
    def _():
        for e in range(2):
            scores(0, e, *buf_c)

    @pl.when(nfull >= 1)
    def _():
        for e in range(2):
            stage(e, 1, buf_a, 0, buf_c)

    def pair(jj, carry):
        j = 2 * jj + 1
        for e in range(2):
            stage(e, j + 1, buf_b, j, buf_a)
        for e in range(2):
            stage(e, j + 2, buf_a, j + 1, buf_b)
        return carry

    def group(jg, carry):
        for u in range(PAIRS_PER_TRIP):
            carry = pair(PAIRS_PER_TRIP * jg + u, carry)
        return carry

    n_pairs = jnp.maximum(nfull - 1, 0) // 2
    n_groups = n_pairs // PAIRS_PER_TRIP
    lax.fori_loop(0, n_groups, group, 0)
    lax.fori_loop(PAIRS_PER_TRIP * n_groups, n_pairs, pair, 0)

    @pl.when(nfull == 0)
    def _():
        finish(nfull, sc_ref, False)

    @pl.when(nfull % 2 == 1)
    def _():
        finish(nfull, sa_ref, True)

    @pl.when(jnp.logical_and(nfull >= 2, nfull % 2 == 0))
    def _():
        for e in range(2):
            stage(e, nfull, buf_b, nfull - 1, buf_a)
        finish(nfull, sb_ref, True)

    outs = []
    for e in range(2):
        acc = acc_ref[e]
        outs.append(acc[:FOX_HEAD_DIM] / acc[FOX_HEAD_DIM:FOX_HEAD_DIM + 1])
    o_ref[...] = jnp.concatenate(outs, axis=0).T.astype(o_ref.dtype)


def _fox_attention(qt0, qt1, k0, k1, vt):
    s = k0.shape[0]
    assert TK % TQ == 0 and TK % TK_SUB == 0 and s % TK == 0
    n_q = s // TQ
    blk_t = pl.BlockSpec((LANES, TQ), lambda p, i: (p, i))
    nxt_t = pl.BlockSpec((LANES, TQ), lambda p, i: (p, jnp.minimum(i + 1, n_q - 1)))
    res = pl.BlockSpec((s, LANES), lambda p, i: (0, p))
    return pl.pallas_call(
        _fox_kernel,
        grid=(FOX_PAIRS, n_q),
        in_specs=[blk_t, blk_t, nxt_t, nxt_t, res, res, pl.BlockSpec((LANES, s), lambda p, i: (p, 0))],
        out_specs=pl.BlockSpec((TQ, LANES), lambda p, i: (i, p)),
        out_shape=jax.ShapeDtypeStruct((s, FOX_WIDTH), BF16),
        scratch_shapes=[pltpu.VMEM((2, 1, TQ), F32), pltpu.VMEM((2, PV_ROWS, TQ), F32),
                        pltpu.VMEM((2, TK, TQ), F32), pltpu.VMEM((2, 1, TQ), F32),
                        pltpu.VMEM((2, TK, TQ), F32), pltpu.VMEM((2, 1, TQ), F32),
                        pltpu.VMEM((2, TK, TQ), F32), pltpu.VMEM((2, 1, TQ), F32)],
        compiler_params=pltpu.CompilerParams(dimension_semantics=("arbitrary", "arbitrary"),
                                             vmem_limit_bytes=VMEM_LIMIT),
        name="fox_attention",
    )(qt0, qt1, qt0, qt1, k0, k1, vt)


def _ret_kernel(q_ref, kt_ref, v_ref, g_ref, inner_ref, xi_ref, zeta_ref, gc_ref, o_ref, r_ref):
    @pl.when(pl.program_id(0) == 0)
    def _():
        r_ref[...] = jnp.zeros_like(r_ref)

    c = RET_CHUNK
    n_chunks = q_ref.shape[0] // c
    tiles = [(hh, ci) for hh in range(RET_HEADS) for ci in range(n_chunks)]

    def head_cols(hh):
        return slice(hh * RET_HEAD_DIM, (hh + 1) * RET_HEAD_DIM)

    def chunk_rows(ci):
        return slice(ci * c, (ci + 1) * c)

    intra, kv = {}, {}
    for hh, ci in tiles:
        q = q_ref[chunk_rows(ci), head_cols(hh)]
        kt = kt_ref[head_cols(hh), chunk_rows(ci)]
        v = v_ref[chunk_rows(ci), head_cols(hh)]
        intra[hh, ci] = (jnp.dot(q, kt, preferred_element_type=F32) * inner_ref[hh]).astype(BF16)
        kzt = (kt.astype(F32) * zeta_ref[hh]).astype(BF16)
        kv[hh, ci] = jnp.dot(kzt, v, preferred_element_type=F32)

    state = {}
    for hh in range(RET_HEADS):
        r_state = r_ref[hh]
        for ci in range(n_chunks):
            state[hh, ci] = r_state.astype(BF16)
            r_state = r_state * gc_ref[hh] + kv[hh, ci]
        r_ref[hh] = r_state

    for hh, ci in tiles:
        q = q_ref[chunk_rows(ci), head_cols(hh)]
        v = v_ref[chunk_rows(ci), head_cols(hh)]
        o = (jnp.dot(intra[hh, ci], v, preferred_element_type=F32)
             + jnp.dot(q, state[hh, ci], preferred_element_type=F32) * xi_ref[hh])
        mu = jnp.mean(o, axis=-1, keepdims=True)
        d = o - mu
        var = jnp.mean(d * d, axis=-1, keepdims=True)
        on = d * lax.rsqrt(var + GN_EPS)
        g = g_ref[chunk_rows(ci), head_cols(hh)].astype(F32)
        o_ref[chunk_rows(ci), head_cols(hh)] = (g * jax.nn.sigmoid(g) * on).astype(o_ref.dtype)


def _retention(rq, rkt, rv, rg, inner, xi_b, zeta_b, gc_b):
    s = rq.shape[0]
    blk = pl.BlockSpec((TR, RET_WIDTH), lambda i: (i, 0))
    blk_t = pl.BlockSpec((RET_WIDTH, TR), lambda i: (0, i))
    tab = _const_spec((RET_HEADS, RET_CHUNK, RET_CHUNK))
    return pl.pallas_call(
        _ret_kernel,
        grid=(s // TR,),
        in_specs=[blk, blk_t, blk, blk, tab, tab, tab, _const_spec((RET_HEADS, 1, RET_HEAD_DIM))],
        out_specs=blk,
        out_shape=jax.ShapeDtypeStruct((s, RET_WIDTH), BF16),
        scratch_shapes=[pltpu.VMEM((RET_HEADS, RET_HEAD_DIM, RET_HEAD_DIM), F32)],
        compiler_params=pltpu.CompilerParams(dimension_semantics=("arbitrary",)),
        name="retention",
    )(rq, rkt, rv, rg, inner, xi_b, zeta_b, gc_b)


def _mix_ffn_kernel(fox_ref, ret_ref, x_ref, fox0_ref, ret0_ref, x0_ref,
                    wf_ref, wr_ref, g1_ref, l1g_ref, l1b_ref,
                    sc_ref, sh_ref, g2_ref, wu_ref, cw_ref, cb_ref, wd_ref, l2g_ref, l2b_ref,
                    o_ref, carry_ref, slab_in_ref, slab_out_ref, y_ref, x1_ref):
    tm = x_ref.shape[0]
    grp = tm // SUBLANES
    pitch = grp + SLAB_PAD_ROWS
    n_slab = D_MODEL // LANES

    def front(fox, ret, x):
        mix = (jnp.dot(fox, wf_ref[...], preferred_element_type=F32)
               + jnp.dot(ret, wr_ref[...], preferred_element_type=F32))
        x1_nat = _layer_norm(ALPHA * x + g1_ref[...] * mix, l1g_ref[...], l1b_ref[...])
        for c in range(n_slab):
            for s in range(SUBLANES):
                slab_in_ref[c, s * pitch:s * pitch + grp, :] = x1_nat[s * grp:(s + 1) * grp,
                                                                      c * LANES:(c + 1) * LANES]
        return jnp.concatenate(
            [jnp.concatenate([slab_in_ref[c, pl.ds(v, SUBLANES, stride=pitch), :] for v in range(grp)],
                             axis=0)
             for c in range(n_slab)], axis=1)

    @pl.when(pl.program_id(0) == 0)
    def _():
        carry_ref[...] = jnp.zeros_like(carry_ref)
        x1_ref[...] = front(fox0_ref[...], ret0_ref[...], x0_ref[...])

    h = (x1_ref[...] * (1.0 + sc_ref[...]) + sh_ref[...]).astype(BF16)
    first_sublane = lax.broadcasted_iota(jnp.int32, (SUBLANES, FF_CHUNK), 0) == 0

    def conv_up(c0):
        cols = slice(c0, c0 + FF_CHUNK)
        up = jnp.dot(h, wu_ref[:, cols], preferred_element_type=F32)
        prev = carry_ref[:, cols]
        tail = []
        for g in range(2):
            rows = slice(tm - (2 - g) * SUBLANES, tm - (1 - g) * SUBLANES)
            tail.append(jnp.where(first_sublane,
                                  pltpu.roll(prev[g * SUBLANES:(g + 1) * SUBLANES], 1, 0),
                                  pltpu.roll(up[rows], 1, 0)))
        carry_ref[:, cols] = up[tm - 2 * SUBLANES:tm]
        back1 = jnp.concatenate([tail[1], up[:tm - SUBLANES]], axis=0)
        back2 = jnp.concatenate([tail[0], tail[1], up[:tm - 2 * SUBLANES]], axis=0)
        cw = cw_ref[:, cols] * RSQRT2
        return cb_ref[:, cols] * RSQRT2 + back2 * cw[0:1] + back1 * cw[1:2] + up * cw[2:3]

    for ci in range(D_FF // FF_CHUNK):
        a = conv_up(ci * FF_CHUNK)
        b = conv_up(D_FF + ci * FF_CHUNK)
        y_ref[:, ci * FF_CHUNK:(ci + 1) * FF_CHUNK] = (a * (1.0 + lax.erf(a)) * b).astype(BF16)

    x1_next = front(fox_ref[...], ret_ref[...], x_ref[...])

    ffn = jnp.dot(y_ref[...], wd_ref[...], preferred_element_type=F32)
    out = _layer_norm(ALPHA * x1_ref[...] + g2_ref[...] * ffn, l2g_ref[...], l2b_ref[...])
    x1_ref[...] = x1_next

    for c in range(n_slab):
        for v in range(grp):
            slab_out_ref[c, pl.ds(v, SUBLANES, stride=pitch), :] = out[v * SUBLANES:(v + 1) * SUBLANES,
                                                                       c * LANES:(c + 1) * LANES]
    for c in range(n_slab):
        for s in range(SUBLANES):
            o_ref[s * grp:(s + 1) * grp, c * LANES:(c + 1) * LANES] = slab_out_ref[c, s * pitch:s * pitch + grp, :]


def _mix_ffn(fox, ret, x, w_fox, w_ret, g1, ln1_g, ln1_b, sc2, sh2, g2, w_up, conv_w, conv_b, w_down,
             ln2_g, ln2_b):
    s, d = x.shape
    tm = TM_FFN
    n_tiles = s // tm
    nxt = lambda w: pl.BlockSpec((tm, w), lambda i: (jnp.minimum(i + 1, n_tiles - 1), 0))
    vec = _const_spec((1, d))
    slab = pltpu.VMEM((d // LANES, tm + SUBLANES * SLAB_PAD_ROWS, LANES), F32)
    return pl.pallas_call(
        _mix_ffn_kernel,
        grid=(n_tiles,),
        in_specs=[nxt(FOX_WIDTH), nxt(RET_WIDTH), nxt(d),
                  _const_spec((tm, FOX_WIDTH)), _const_spec((tm, RET_WIDTH)), _const_spec((tm, d)),
                  _const_spec((FOX_WIDTH, d)), _const_spec((RET_WIDTH, d)), vec, vec, vec,
                  vec, vec, vec,
                  _const_spec((d, 2 * D_FF)), _const_spec((CONV_WIDTH, 2 * D_FF)),
                  _const_spec((1, 2 * D_FF)), _const_spec((D_FF, d)), vec, vec],
        out_specs=pl.BlockSpec((tm, d), lambda i: (i, 0)),
        out_shape=jax.ShapeDtypeStruct((s, d), F32),
        scratch_shapes=[pltpu.VMEM((2 * SUBLANES, 2 * D_FF), F32), slab, slab,
                        pltpu.VMEM((tm, D_FF), BF16), pltpu.VMEM((tm, d), F32)],
        compiler_params=pltpu.CompilerParams(dimension_semantics=("arbitrary",),
                                             vmem_limit_bytes=VMEM_LIMIT),
        name="mix_ffn",
    )(fox, ret, x, fox, ret, x, w_fox, w_ret, g1, ln1_g, ln1_b, sc2, sh2, g2, w_up, conv_w, conv_b,
      w_down, ln2_g, ln2_b)


def _placement_tables():
    e_q = np.zeros((LANES, FOX_WIDTH), np.float32)
    e_k = np.zeros((LANES, FOX_WIDTH), np.float32)
    for h in range(FOX_HEADS):
        base = (h // 2) * LANES + (FOX_HEAD_DIM if h % 2 == 0 else 0)
        for j in range(N_PIECES):
            e_q[j * FOX_HEADS + h, base + BIAS_CUM + j] = 1.0
            e_q[ONE_LANE, base + BIAS_ONE + j] = 1.0
            e_k[ONE_LANE, base + BIAS_CUM + j] = 1.0
            e_k[j * FOX_HEADS + h, base + BIAS_ONE + j] = -1.0
    return jnp.asarray(e_q.T, BF16), jnp.asarray(e_k, BF16)


def _rotation_tables(s, tm):
    dk = RET_HEAD_DIM
    inv_freq = ROPE_BASE ** (-np.arange(0, dk, 2, dtype=np.float64) / dk)
    inv_freq = np.concatenate([inv_freq, inv_freq])
    sign = np.concatenate([-np.ones(dk // 2), np.ones(dk // 2)])
    ang_a = (np.arange(s // tm, dtype=np.float64) * tm)[:, None] * inv_freq[None, :]
    ang_b = np.arange(tm, dtype=np.float64)[:, None] * inv_freq[None, :]
    rot_a = np.stack([np.cos(ang_a), np.sin(ang_a)], axis=1)
    rot_b = np.stack([np.cos(ang_b), np.sin(ang_b), sign * np.cos(ang_b), sign * np.sin(ang_b)])
    f32 = lambda a: jnp.asarray(a, F32)
    return f32(rot_a), f32(rot_b), f32(rot_a.transpose(0, 2, 1)), f32(rot_b.transpose(0, 2, 1))


def _decay_tables():
    c = RET_CHUNK
    log_gamma = np.log1p(-np.exp2(-5.0 - np.arange(RET_HEADS, dtype=np.float64)))
    idx = np.arange(c, dtype=np.float64)
    diff = idx[:, None] - idx[None, :]
    inner = np.where(diff[None] >= 0,
                     np.exp(np.maximum(diff, 0.0)[None] * log_gamma[:, None, None]), 0.0)
    xi = np.exp((idx[None, :] + 1.0) * log_gamma[:, None])
    zeta = np.exp((c - 1.0 - idx[None, :]) * log_gamma[:, None])
    g_chunk = np.exp(c * log_gamma)
    bshape = (RET_HEADS, c, RET_HEAD_DIM)
    f32 = lambda a: jnp.asarray(a, F32)
    return (f32(inner),
            f32(np.broadcast_to(xi[:, :, None], bshape)),
            f32(np.broadcast_to(zeta[:, None, :], bshape)),
            f32(np.broadcast_to(g_chunk[:, None, None], (RET_HEADS, 1, RET_HEAD_DIM))))


def kernel(x, c, w_ada, b_ada, w_in, b_f, w_out, ln1_g, ln1_b, w_up, conv_w, conv_b, w_down, ln2_g, ln2_b):
    b, s, d = x.shape
    assert (b, s, d) == (1, SEQ, D_MODEL) and w_ada.shape[0] == DEPTH
    xs = x[0]

    wi = w_in[0]
    o = 0
    parts = {}
    for name, width in (("fq", FOX_WIDTH), ("fk", FOX_WIDTH), ("fv", FOX_WIDTH), ("ff", FOX_HEADS),
                        ("rq", RET_WIDTH), ("rk", RET_WIDTH), ("rv", RET_WIDTH), ("rg", RET_WIDTH)):
        parts[name] = wi[:, o:o + width]
        o += width
    w_all = jnp.concatenate(
        [parts["fk"], jnp.pad(parts["ff"], ((0, 0), (0, LANES - FOX_HEADS))),
         parts["rq"], parts["rv"], parts["rg"]], axis=1).astype(BF16)
    w_t = jnp.concatenate([parts["fq"], parts["fv"], parts["rk"]], axis=1).T.astype(BF16)
    bf_row = jnp.pad(b_f[0][None, :], ((0, 0), (0, LANES - FOX_HEADS)))
    wo = w_out[0].astype(BF16)

    eq_t, e_k = _placement_tables()
    rot_a, rot_b, rot_at, rot_bt = _rotation_tables(s, TM_PROJ)
    inner, xi_b, zeta_b, gc_b = _decay_tables()

    mod = _adaln_mod(c.reshape(d, 1), w_ada[0], b_ada[0][None, :])
    sh1, sc1, g1, sh2, sc2, g2 = [mod[:, i * d:(i + 1) * d] for i in range(6)]

    qt0, qt1, k0, k1, vt, rq, rkt, rv, rg = _in_proj(
        xs, sc1, sh1, w_all, w_t, bf_row, rot_a, rot_b, rot_at, rot_bt, eq_t, e_k)
    fox = _fox_attention(qt0, qt1, k0, k1, vt)
    ret = _retention(rq, rkt, rv, rg, inner, xi_b, zeta_b, gc_b)
    out = _mix_ffn(fox, ret, xs, wo[:FOX_WIDTH], wo[FOX_WIDTH:], g1, ln1_g[0][None, :], ln1_b[0][None, :],
                   sc2, sh2, g2, w_up[0].astype(BF16), conv_w[0], conv_b[0][None, :],
                   w_down[0].astype(BF16), ln2_g[0][None, :], ln2_b[0][None, :])
    return out[None]
```

```python
import math

import jax
import jax.numpy as jnp
import numpy as np
from jax import lax
from jax.experimental import pallas as pl
from jax.experimental.pallas import tpu as pltpu

F32 = jnp.float32
BF16 = jnp.bfloat16

D_MODEL = 1024
SEQ = 16384
FOX_HEADS = 8
FOX_HEAD_DIM = 64
FOX_WIDTH = FOX_HEADS * FOX_HEAD_DIM
RET_HEADS = 4
RET_HEAD_DIM = 128
RET_WIDTH = RET_HEADS * RET_HEAD_DIM
D_FF = 2816
CONV_WIDTH = 3
RET_CHUNK = 128
ROPE_BASE = 10000.0
LN_EPS = 1e-5
GN_EPS = 1e-6
DEPTH = 1
ALPHA = (2 * DEPTH) ** 0.25

LANES = 128
SUBLANES = 8
BF16_ROWS = 16
VMEM_LIMIT = 56 * 1024 * 1024

FOX_PAIRS = FOX_HEADS // 2
BIAS_CUM = 0
BIAS_ONE = 3
N_PIECES = 3
ONE_LANE = N_PIECES * FOX_HEADS
PV_ROWS = FOX_HEAD_DIM + BF16_ROWS
NEG_BIG = -1e30
LOG2_E = math.log2(math.e)
RSQRT2 = 2.0 ** -0.5

COL_K = 0
COL_F = COL_K + FOX_WIDTH
COL_RQ = COL_F + LANES
COL_RV = COL_RQ + RET_WIDTH
COL_RG = COL_RV + RET_WIDTH
N_COLS = COL_RG + RET_WIDTH
ROW_QT = 0
ROW_VT = ROW_QT + FOX_WIDTH
ROW_RKT = ROW_VT + FOX_WIDTH
N_ROWS_T = ROW_RKT + RET_WIDTH

TM_PROJ = 512
TQ = 512
TK = 1024
TK_SUB = 256
PAIRS_PER_TRIP = 2
TM_FFN = 512
FF_CHUNK = 256
SLAB_PAD_ROWS = 4


def _const_spec(shape):
    return pl.BlockSpec(shape, lambda *_: (0,) * len(shape), pipeline_mode=pl.Buffered(1))


def _split3(x):
    p0 = x.astype(BF16).astype(F32)
    r1 = x - p0
    p1 = r1.astype(BF16).astype(F32)
    p2 = (r1 - p1).astype(BF16).astype(F32)
    return p0, p1, p2


def _layer_norm(y, g, b):
    mu = jnp.mean(y, axis=-1, keepdims=True)
    d = y - mu
    var = jnp.mean(d * d, axis=-1, keepdims=True)
    return d * lax.rsqrt(var + LN_EPS) * g + b


def _mod_kernel(c_ref, w_ref, b_ref, o_ref):
    c = c_ref[...]
    sc = c * jax.nn.sigmoid(c)
    o_ref[...] = jnp.sum(w_ref[...] * sc, axis=0, keepdims=True) + b_ref[...]


def _adaln_mod(c_col, w_ada, b_ada):
    d, n = w_ada.shape
    tn = 2048
    return pl.pallas_call(
        _mod_kernel,
        grid=(n // tn,),
        in_specs=[pl.BlockSpec((d, 1), lambda j: (0, 0)),
                  pl.BlockSpec((d, tn), lambda j: (0, j)),
                  pl.BlockSpec((1, tn), lambda j: (0, j))],
        out_specs=pl.BlockSpec((1, tn), lambda j: (0, j)),
        out_shape=jax.ShapeDtypeStruct((1, n), F32),
        compiler_params=pltpu.CompilerParams(dimension_semantics=("arbitrary",)),
        name="adaln_mod",
    )(c_col, w_ada, b_ada)


def _in_proj_kernel(x_ref, sc_ref, sh_ref, w_ref, wt_ref, bf_ref, ra_ref, rb_ref, rat_ref, rbt_ref,
                    eqt_ref, ek_ref, inner_ref, xi_ref, zeta_ref, gc_ref,
                    qt0_out, qt1_out, k0_out, k1_out, vt_out, ret_out,
                    carry_ref, r_ref):
    tm = x_ref.shape[0]

    @pl.when(pl.program_id(0) == 0)
    def _():
        carry_ref[...] = jnp.zeros_like(carry_ref)
        r_ref[...] = jnp.zeros_like(r_ref)

    h = (x_ref[...] * (1.0 + sc_ref[...]) + sh_ref[...]).astype(BF16)

    def proj(c0, width):
        return jnp.dot(h, w_ref[:, c0:c0 + width], preferred_element_type=F32)

    def proj_t(r0, height):
        return lax.dot_general(wt_ref[r0:r0 + height, :], h, (((1,), (1,)), ((), ())),
                               preferred_element_type=F32)

    lane = lax.broadcasted_iota(jnp.int32, (tm, LANES), 1)
    head_lane = lane < FOX_HEADS
    ff = proj(COL_F, LANES) + bf_ref[...]
    two = 2 * RET_HEAD_DIM
    rq_raw = [proj(COL_RQ + pair * two, two) for pair in range(RET_HEADS // 2)]
    rv = proj(COL_RV, RET_WIDTH).astype(BF16)
    rg = proj(COL_RG, RET_WIDTH)
    rkt_raw = proj_t(ROW_RKT, RET_WIDTH)

    logf = jnp.minimum(ff, 0.0) - jnp.log1p(jnp.exp(-jnp.abs(ff)))
    logf = jnp.where(head_lane, logf, 0.0)
    p0, p1, p2 = _split3(logf)
    pieces = p0 + pltpu.roll(p1, FOX_HEADS, 1) + pltpu.roll(p2, 2 * FOX_HEADS, 1)
    row = lax.broadcasted_iota(jnp.int32, (tm, tm), 0)
    col = lax.broadcasted_iota(jnp.int32, (tm, tm), 1)
    tril = jnp.where(row >= col, 1.0, 0.0).astype(BF16)
    c3 = jnp.dot(tril, pieces.astype(BF16), preferred_element_type=F32)

    half = RET_HEAD_DIM // 2
    ca, sa = ra_ref[0, 0:1, :], ra_ref[0, 1:2, :]
    cos = ca * rb_ref[0] - sa * rb_ref[1]
    sin = sa * rb_ref[2] + ca * rb_ref[3]
    cat, sat = rat_ref[0, :, 0:1], rat_ref[0, :, 1:2]
    cost = cat * rbt_ref[0] - sat * rbt_ref[1]
    sint = sat * rbt_ref[2] + cat * rbt_ref[3]
    kscale = RET_HEAD_DIM ** -0.5
    rq, rkt = [], []
    for a2 in rq_raw:
        for e in range(2):
            a = a2[:, e * RET_HEAD_DIM:(e + 1) * RET_HEAD_DIM]
            rq.append((a * cos + pltpu.roll(a, half, 1) * sin).astype(BF16))
    for hh in range(RET_HEADS):
        b = rkt_raw[hh * RET_HEAD_DIM:(hh + 1) * RET_HEAD_DIM]
        b_rot = jnp.concatenate([b[half:], b[:half]], axis=0)
        rkt.append(((b * cost + b_rot * sint) * kscale).astype(BF16))

    c = RET_CHUNK
    n_chunks = tm // c
    tiles = [(hh, ci) for hh in range(RET_HEADS) for ci in range(n_chunks)]

    def head_cols(hh):
        return slice(hh * RET_HEAD_DIM, (hh + 1) * RET_HEAD_DIM)

    def chunk_rows(ci):
        return slice(ci * c, (ci + 1) * c)

    intra, kv = {}, {}
    for hh, ci in tiles:
        kt = rkt[hh][:, chunk_rows(ci)]
        v = rv[chunk_rows(ci), head_cols(hh)]
        intra[hh, ci] = (jnp.dot(rq[hh][chunk_rows(ci)], kt, preferred_element_type=F32)
                         * inner_ref[hh]).astype(BF16)
        kzt = (kt.astype(F32) * zeta_ref[hh]).astype(BF16)
        kv[hh, ci] = jnp.dot(kzt, v, preferred_element_type=F32)

    k = proj(COL_K, FOX_WIDTH)
    cum = c3 + pltpu.roll(c3, LANES - FOX_HEADS, 1) + pltpu.roll(c3, LANES - 2 * FOX_HEADS, 1)
    cum = jnp.where(head_lane, cum, 0.0) + carry_ref[0:1, :]
    carry_ref[0:1, :] = cum[tm - 1:tm, :]
    c0_, c1_, c2_ = _split3(cum * LOG2_E)
    cum_pieces = c0_ + pltpu.roll(c1_, FOX_HEADS, 1) + pltpu.roll(c2_, 2 * FOX_HEADS, 1)
    cum_pieces = jnp.where(lane == ONE_LANE, 1.0, cum_pieces)
    kb = jnp.dot(cum_pieces.astype(BF16), ek_ref[...], preferred_element_type=F32)
    qbt = jnp.dot(eqt_ref[...], cum_pieces.T.astype(BF16), preferred_element_type=F32)
    even_own = (lax.broadcasted_iota(jnp.int32, (1, FOX_WIDTH), 1) % LANES) < FOX_HEAD_DIM
    even_own_t = (lax.broadcasted_iota(jnp.int32, (FOX_WIDTH, 1), 0) % LANES) < FOX_HEAD_DIM
    k0_out[...] = jnp.where(even_own, k, kb).astype(BF16)
    k1_out[...] = jnp.where(even_own, kb, k).astype(BF16)

    state = {}
    for hh in range(RET_HEADS):
        r_state = r_ref[hh]
        for ci in range(n_chunks):
            state[hh, ci] = r_state.astype(BF16)
            r_state = r_state * gc_ref[hh] + kv[hh, ci]
        r_ref[hh] = r_state

    for hh, ci in tiles:
        v = rv[chunk_rows(ci), head_cols(hh)]
        o = (jnp.dot(intra[hh, ci], v, preferred_element_type=F32)
             + jnp.dot(rq[hh][chunk_rows(ci)], state[hh, ci], preferred_element_type=F32) * xi_ref[hh])
        mu = jnp.mean(o, axis=-1, keepdims=True)
        d = o - mu
        var = jnp.mean(d * d, axis=-1, keepdims=True)
        on = d * lax.rsqrt(var + GN_EPS)
        g = rg[chunk_rows(ci), head_cols(hh)]
        ret_out[chunk_rows(ci), head_cols(hh)] = (g * jax.nn.sigmoid(g) * on).astype(BF16)

    tp = proj_t(ROW_QT, 2 * FOX_WIDTH)
    qt = tp[:FOX_WIDTH] * (FOX_HEAD_DIM ** -0.5 * LOG2_E)
    qt0_out[...] = jnp.where(even_own_t, qt, qbt).astype(BF16)
    qt1_out[...] = jnp.where(even_own_t, qbt, qt).astype(BF16)
    vt_out[...] = tp[FOX_WIDTH:].astype(BF16)


def _in_proj(x, sc1, sh1, w_all, w_t, bf_row, rot_a, rot_b, rot_at, rot_bt, eq_t, e_k,
             inner, xi_b, zeta_b, gc_b):
    s, d = x.shape
    tm = TM_PROJ
    row = lambda w: pl.BlockSpec((tm, w), lambda i: (i, 0))
    col = lambda h: pl.BlockSpec((h, tm), lambda i: (0, i))
    tab = _const_spec((RET_HEADS, RET_CHUNK, RET_CHUNK))
    out_shapes = ([jax.ShapeDtypeStruct((FOX_WIDTH, s), BF16)] * 2
                  + [jax.ShapeDtypeStruct((s, FOX_WIDTH), BF16)] * 2
                  + [jax.ShapeDtypeStruct((FOX_WIDTH, s), BF16),
                     jax.ShapeDtypeStruct((s, RET_WIDTH), BF16)])
    return pl.pallas_call(
        _in_proj_kernel,
        grid=(s // tm,),
        in_specs=[row(d), _const_spec((1, d)), _const_spec((1, d)),
                  _const_spec((d, N_COLS)), _const_spec((N_ROWS_T, d)), _const_spec((1, LANES)),
                  pl.BlockSpec((1, 2, RET_HEAD_DIM), lambda i: (i, 0, 0)),
                  _const_spec((4, tm, RET_HEAD_DIM)),
                  pl.BlockSpec((1, RET_HEAD_DIM, 2), lambda i: (i, 0, 0)),
                  _const_spec((4, RET_HEAD_DIM, tm)),
                  _const_spec((FOX_WIDTH, LANES)), _const_spec((LANES, FOX_WIDTH)),
                  tab, tab, tab, _const_spec((RET_HEADS, 1, RET_HEAD_DIM))],
        out_specs=([col(FOX_WIDTH)] * 2 + [row(FOX_WIDTH)] * 2 + [col(FOX_WIDTH), row(RET_WIDTH)]),
        out_shape=out_shapes,
        scratch_shapes=[pltpu.VMEM((SUBLANES, LANES), F32),
                        pltpu.VMEM((RET_HEADS, RET_HEAD_DIM, RET_HEAD_DIM), F32)],
        compiler_params=pltpu.CompilerParams(dimension_semantics=("arbitrary",),
                                             vmem_limit_bytes=VMEM_LIMIT),
        name="in_proj",
    )(x, sc1, sh1, w_all, w_t, bf_row, rot_a, rot_b, rot_at, rot_bt, eq_t, e_k,
      inner, xi_b, zeta_b, gc_b)


def _fox_kernel(qt0_ref, qt1_ref, qtn0_ref, qtn1_ref, k0_ref, k1_ref, vt_ref, o_ref, m_ref, acc_ref,
                sa_ref, ca_ref, sb_ref, cb_ref, sc_ref, cc_ref):
    i = pl.program_id(1)
    k_refs = (k0_ref, k1_ref)
    qt = [qt0_ref[...], qt1_ref[...]]
    qt_next = [qtn0_ref[...], qtn1_ref[...]]
    m_ref[...] = jnp.full_like(m_ref, NEG_BIG)
    acc_ref[...] = jnp.zeros_like(acc_ref)
    n_sub = TK // TK_SUB

    def qk(e, q_aug, start):
        return jnp.dot(k_refs[e][pl.ds(start, TK_SUB), :], q_aug, preferred_element_type=F32)

    def values(e, start, width):
        return jnp.concatenate(
            [vt_ref[e * FOX_HEAD_DIM:(e + 1) * FOX_HEAD_DIM, pl.ds(start, width)],
             jnp.ones((BF16_ROWS, width), BF16)], axis=0)

    def pv(e, st, m_new, start, width):
        pt = jnp.exp2(st - m_new).astype(BF16)
        return jnp.dot(values(e, start, width), pt, preferred_element_type=F32)

    def score_sub(e, q_aug, j, sub, s_ref, cmax):
        st = qk(e, q_aug, pl.multiple_of(j * TK + sub * TK_SUB, TK_SUB))
        s_ref[e, sub * TK_SUB:(sub + 1) * TK_SUB, :] = st
        cm = jnp.max(st, axis=0, keepdims=True)
        return cm if cmax is None else jnp.maximum(cmax, cm)

    def scores(j, e, s_ref, c_ref):
        cmax = None
        for sub in range(n_sub):
            cmax = score_sub(e, qt[e], j, sub, s_ref, cmax)
        c_ref[e] = cmax

    def stage(e, j_next, buf_next, j_cur, buf_cur):
        (s_next, c_next), (s_cur, c_cur) = buf_next, buf_cur
        m_prev = m_ref[e]
        m_new = jnp.maximum(m_prev, c_cur[e])
        acc = jnp.exp2(m_prev - m_new) * acc_ref[e]
        cmax = None
        for sub in range(n_sub):
            rows = slice(sub * TK_SUB, (sub + 1) * TK_SUB)
            cmax = score_sub(e, qt[e], j_next, sub, s_next, cmax)
            acc = acc + pv(e, s_cur[e, rows, :], m_new,
                           pl.multiple_of(j_cur * TK + sub * TK_SUB, TK_SUB), TK_SUB)
        c_next[e] = cmax
        acc_ref[e] = acc
        m_ref[e] = m_new

    def finish(j, s_ref, fill_next):
        n_pieces = TK // TQ
        tri = (lax.broadcasted_iota(jnp.int32, (TQ, TQ), 0)
               <= lax.broadcasted_iota(jnp.int32, (TQ, TQ), 1))
        for diag in range(n_pieces):
            @pl.when(i % n_pieces == diag)
            def _():
                for e in range(2):
                    cmax = None
                    for piece in range(n_pieces):
                        if fill_next:
                            for sub in range(piece * n_sub // n_pieces, (piece + 1) * n_sub // n_pieces):
                                cmax = score_sub(e, qt_next[e], 0, sub, sc_ref, cmax)
                        if piece > diag:
                            continue
                        st = s_ref[e, piece * TQ:(piece + 1) * TQ, :]
                        if piece == diag:
                            st = jnp.where(tri, st, NEG_BIG)
                        m_prev = m_ref[e]
                        m_new = jnp.maximum(m_prev, jnp.max(st, axis=0, keepdims=True))
                        start = pl.multiple_of(j * TK + piece * TQ, TQ)
                        acc_ref[e] = (jnp.exp2(m_prev - m_new) * acc_ref[e]
                                      + pv(e, st, m_new, start, TQ))
                        m_ref[e] = m_new
                    if fill_next:
                        cc_ref[e] = cmax

    buf_a = (sa_ref, ca_ref)
    buf_b = (sb_ref, cb_ref)
    buf_c = (sc_ref, cc_ref)

    nfull = (i * TQ) // TK
    prefilled = jnp.logical_and(i > 0, ((i - 1) * TQ) // TK >= 1)

    @pl.when(jnp.logical_not(prefilled))
    def _():
        for e in range(2):
            scores(0, e, *buf_c)

    @pl.when(nfull >= 1)
    def _():
        for e in range(2):
            stage(e, 1, buf_a, 0, buf_c)

    def pair(jj, carry):
        j = 2 * jj + 1
        for e in range(2):
            stage(e, j + 1, buf_b, j, buf_a)
        for e in range(2):
            stage(e, j + 2, buf_a, j + 1, buf_b)
        return carry

    def group(jg, carry):
        for u in range(PAIRS_PER_TRIP):
            carry = pair(PAIRS_PER_TRIP * jg + u, carry)
        return carry

    n_pairs = jnp.maximum(nfull - 1, 0) // 2
    n_groups = n_pairs // PAIRS_PER_TRIP
    lax.fori_loop(0, n_groups, group, 0)
    lax.fori_loop(PAIRS_PER_TRIP * n_groups, n_pairs, pair, 0)

    @pl.when(nfull == 0)
    def _():
        finish(nfull, sc_ref, False)

    @pl.when(nfull % 2 == 1)
    def _():
        finish(nfull, sa_ref, True)

    @pl.when(jnp.logical_and(nfull >= 2, nfull % 2 == 0))
    def _():
        for e in range(2):
            stage(e, nfull, buf_b, nfull - 1, buf_a)
        finish(nfull, sb_ref, True)

    outs = []
    for e in range(2):
        acc = acc_ref[e]
        outs.append(acc[:FOX_HEAD_DIM] / acc[FOX_HEAD_DIM:FOX_HEAD_DIM + 1])
    o_ref[...] = jnp.concatenate(outs, axis=0).T.astype(o_ref.dtype)


def _fox_attention(qt0, qt1, k0, k1, vt):
    s = k0.shape[0]
    assert TK % TQ == 0 and TK % TK_SUB == 0 and s % TK == 0
    n_q = s // TQ
    blk_t = pl.BlockSpec((LANES, TQ), lambda p, i: (p, i))
    nxt_t = pl.BlockSpec((LANES, TQ), lambda p, i: (p, jnp.minimum(i + 1, n_q - 1)))
    res = pl.BlockSpec((s, LANES), lambda p, i: (0, p))
    return pl.pallas_call(
        _fox_kernel,
        grid=(FOX_PAIRS, n_q),
        in_specs=[blk_t, blk_t, nxt_t, nxt_t, res, res, pl.BlockSpec((LANES, s), lambda p, i: (p, 0))],
        out_specs=pl.BlockSpec((TQ, LANES), lambda p, i: (i, p)),
        out_shape=jax.ShapeDtypeStruct((s, FOX_WIDTH), BF16),
        scratch_shapes=[pltpu.VMEM((2, 1, TQ), F32), pltpu.VMEM((2, PV_ROWS, TQ), F32),
                        pltpu.VMEM((2, TK, TQ), F32), pltpu.VMEM((2, 1, TQ), F32),
                        pltpu.VMEM((2, TK, TQ), F32), pltpu.VMEM((2, 1, TQ), F32),
                        pltpu.VMEM((2, TK, TQ), F32), pltpu.VMEM((2, 1, TQ), F32)],
        compiler_params=pltpu.CompilerParams(dimension_semantics=("arbitrary", "arbitrary"),
                                             vmem_limit_bytes=VMEM_LIMIT),
        name="fox_attention",
    )(qt0, qt1, qt0, qt1, k0, k1, vt)


def _mix_ffn_kernel(fox_ref, ret_ref, x_ref, fox0_ref, ret0_ref, x0_ref,
                    wf_ref, wr_ref, g1_ref, l1g_ref, l1b_ref,
                    sc_ref, sh_ref, g2_ref, wu_ref, cw_ref, cb_ref, wd_ref, l2g_ref, l2b_ref,
                    o_ref, carry_ref, slab_in_ref, slab_out_ref, y_ref, x1_ref):
    tm = x_ref.shape[0]
    grp = tm // SUBLANES
    pitch = grp + SLAB_PAD_ROWS
    n_slab = D_MODEL // LANES

    def front(fox, ret, x):
        mix = (jnp.dot(fox, wf_ref[...], preferred_element_type=F32)
               + jnp.dot(ret, wr_ref[...], preferred_element_type=F32))
        x1_nat = _layer_norm(ALPHA * x + g1_ref[...] * mix, l1g_ref[...], l1b_ref[...])
        for c in range(n_slab):
            for s in range(SUBLANES):
                slab_in_ref[c, s * pitch:s * pitch + grp, :] = x1_nat[s * grp:(s + 1) * grp,
                                                                      c * LANES:(c + 1) * LANES]
        return jnp.concatenate(
            [jnp.concatenate([slab_in_ref[c, pl.ds(v, SUBLANES, stride=pitch), :] for v in range(grp)],
                             axis=0)
             for c in range(n_slab)], axis=1)

    @pl.when(pl.program_id(0) == 0)
    def _():
        carry_ref[...] = jnp.zeros_like(carry_ref)
        x1_ref[...] = front(fox0_ref[...], ret0_ref[...], x0_ref[...])

    h = (x1_ref[...] * (1.0 + sc_ref[...]) + sh_ref[...]).astype(BF16)
    first_sublane = lax.broadcasted_iota(jnp.int32, (SUBLANES, FF_CHUNK), 0) == 0

    def conv_up(c0):
        cols = slice(c0, c0 + FF_CHUNK)
        up = jnp.dot(h, wu_ref[:, cols], preferred_element_type=F32)
        prev = carry_ref[:, cols]
        tail = []
        for g in range(2):
            rows = slice(tm - (2 - g) * SUBLANES, tm - (1 - g) * SUBLANES)
            tail.append(jnp.where(first_sublane,
                                  pltpu.roll(prev[g * SUBLANES:(g + 1) * SUBLANES], 1, 0),
                                  pltpu.roll(up[rows], 1, 0)))
        carry_ref[:, cols] = up[tm - 2 * SUBLANES:tm]
        back1 = jnp.concatenate([tail[1], up[:tm - SUBLANES]], axis=0)
        back2 = jnp.concatenate([tail[0], tail[1], up[:tm - 2 * SUBLANES]], axis=0)
        cw = cw_ref[:, cols] * RSQRT2
        return cb_ref[:, cols] * RSQRT2 + back2 * cw[0:1] + back1 * cw[1:2] + up * cw[2:3]

    for ci in range(D_FF // FF_CHUNK):
        a = conv_up(ci * FF_CHUNK)
        b = conv_up(D_FF + ci * FF_CHUNK)
        y_ref[:, ci * FF_CHUNK:(ci + 1) * FF_CHUNK] = (a * (1.0 + lax.erf(a)) * b).astype(BF16)

    x1_next = front(fox_ref[...], ret_ref[...], x_ref[...])

    ffn = jnp.dot(y_ref[...], wd_ref[...], preferred_element_type=F32)
    out = _layer_norm(ALPHA * x1_ref[...] + g2_ref[...] * ffn, l2g_ref[...], l2b_ref[...])
    x1_ref[...] = x1_next

    for c in range(n_slab):
        for v in range(grp):
            slab_out_ref[c, pl.ds(v, SUBLANES, stride=pitch), :] = out[v * SUBLANES:(v + 1) * SUBLANES,
                                                                       c * LANES:(c + 1) * LANES]
    for c in range(n_slab):
        for s in range(SUBLANES):
            o_ref[s * grp:(s + 1) * grp, c * LANES:(c + 1) * LANES] = slab_out_ref[c, s * pitch:s * pitch + grp, :]


def _mix_ffn(fox, ret, x, w_fox, w_ret, g1, ln1_g, ln1_b, sc2, sh2, g2, w_up, conv_w, conv_b, w_down,
             ln2_g, ln2_b):
    s, d = x.shape
    tm = TM_FFN
    n_tiles = s // tm
    nxt = lambda w: pl.BlockSpec((tm, w), lambda i: (jnp.minimum(i + 1, n_tiles - 1), 0))
    vec = _const_spec((1, d))
    slab = pltpu.VMEM((d // LANES, tm + SUBLANES * SLAB_PAD_ROWS, LANES), F32)
    return pl.pallas_call(
        _mix_ffn_kernel,
        grid=(n_tiles,),
        in_specs=[nxt(FOX_WIDTH), nxt(RET_WIDTH), nxt(d),
                  _const_spec((tm, FOX_WIDTH)), _const_spec((tm, RET_WIDTH)), _const_spec((tm, d)),
                  _const_spec((FOX_WIDTH, d)), _const_spec((RET_WIDTH, d)), vec, vec, vec,
                  vec, vec, vec,
                  _const_spec((d, 2 * D_FF)), _const_spec((CONV_WIDTH, 2 * D_FF)),
                  _const_spec((1, 2 * D_FF)), _const_spec((D_FF, d)), vec, vec],
        out_specs=pl.BlockSpec((tm, d), lambda i: (i, 0)),
        out_shape=jax.ShapeDtypeStruct((s, d), F32),
        scratch_shapes=[pltpu.VMEM((2 * SUBLANES, 2 * D_FF), F32), slab, slab,
                        pltpu.VMEM((tm, D_FF), BF16), pltpu.VMEM((tm, d), F32)],
        compiler_params=pltpu.CompilerParams(dimension_semantics=("arbitrary",),
                                             vmem_limit_bytes=VMEM_LIMIT),
        name="mix_ffn",
    )(fox, ret, x, fox, ret, x, w_fox, w_ret, g1, ln1_g, ln1_b, sc2, sh2, g2, w_up, conv_w, conv_b,
      w_down, ln2_g, ln2_b)


def _placement_tables():
    e_q = np.zeros((LANES, FOX_WIDTH), np.float32)
    e_k = np.zeros((LANES, FOX_WIDTH), np.float32)
    for h in range(FOX_HEADS):
        base = (h // 2) * LANES + (FOX_HEAD_DIM if h % 2 == 0 else 0)
        for j in range(N_PIECES):
            e_q[j * FOX_HEADS + h, base + BIAS_CUM + j] = 1.0
            e_q[ONE_LANE, base + BIAS_ONE + j] = 1.0
            e_k[ONE_LANE, base + BIAS_CUM + j] = 1.0
            e_k[j * FOX_HEADS + h, base + BIAS_ONE + j] = -1.0
    return jnp.asarray(e_q.T, BF16), jnp.asarray(e_k, BF16)


def _rotation_tables(s, tm):
    dk = RET_HEAD_DIM
    inv_freq = ROPE_BASE ** (-np.arange(0, dk, 2, dtype=np.float64) / dk)
    inv_freq = np.concatenate([inv_freq, inv_freq])
    sign = np.concatenate([-np.ones(dk // 2), np.ones(dk // 2)])
    ang_a = (np.arange(s // tm, dtype=np.float64) * tm)[:, None] * inv_freq[None, :]
    ang_b = np.arange(tm, dtype=np.float64)[:, None] * inv_freq[None, :]
    rot_a = np.stack([np.cos(ang_a), np.sin(ang_a)], axis=1)
    rot_b = np.stack([np.cos(ang_b), np.sin(ang_b), sign * np.cos(ang_b), sign * np.sin(ang_b)])
    f32 = lambda a: jnp.asarray(a, F32)
    return f32(rot_a), f32(rot_b), f32(rot_a.transpose(0, 2, 1)), f32(rot_b.transpose(0, 2, 1))


def _decay_tables():
    c = RET_CHUNK
    log_gamma = np.log1p(-np.exp2(-5.0 - np.arange(RET_HEADS, dtype=np.float64)))
    idx = np.arange(c, dtype=np.float64)
    diff = idx[:, None] - idx[None, :]
    inner = np.where(diff[None] >= 0,
                     np.exp(np.maximum(diff, 0.0)[None] * log_gamma[:, None, None]), 0.0)
    xi = np.exp((idx[None, :] + 1.0) * log_gamma[:, None])
    zeta = np.exp((c - 1.0 - idx[None, :]) * log_gamma[:, None])
    g_chunk = np.exp(c * log_gamma)
    bshape = (RET_HEADS, c, RET_HEAD_DIM)
    f32 = lambda a: jnp.asarray(a, F32)
    return (f32(inner),
            f32(np.broadcast_to(xi[:, :, None], bshape)),
            f32(np.broadcast_to(zeta[:, None, :], bshape)),
            f32(np.broadcast_to(g_chunk[:, None, None], (RET_HEADS, 1, RET_HEAD_DIM))))


def kernel(x, c, w_ada, b_ada, w_in, b_f, w_out, ln1_g, ln1_b, w_up, conv_w, conv_b, w_down, ln2_g, ln2_b):
    b, s, d = x.shape
    assert (b, s, d) == (1, SEQ, D_MODEL) and w_ada.shape[0] == DEPTH
    xs = x[0]

    wi = w_in[0]
    o = 0
    parts = {}
    for name, width in (("fq", FOX_WIDTH), ("fk", FOX_WIDTH), ("fv", FOX_WIDTH), ("ff", FOX_HEADS),
                        ("rq", RET_WIDTH), ("rk", RET_WIDTH), ("rv", RET_WIDTH), ("rg", RET_WIDTH)):
        parts[name] = wi[:, o:o + width]
        o += width
    w_all = jnp.concatenate(
        [parts["fk"], jnp.pad(parts["ff"], ((0, 0), (0, LANES - FOX_HEADS))),
         parts["rq"], parts["rv"], parts["rg"]], axis=1).astype(BF16)
    w_t = jnp.concatenate([parts["fq"], parts["fv"], parts["rk"]], axis=1).T.astype(BF16)
    bf_row = jnp.pad(b_f[0][None, :], ((0, 0), (0, LANES - FOX_HEADS)))
    wo = w_out[0].astype(BF16)

    eq_t, e_k = _placement_tables()
    rot_a, rot_b, rot_at, rot_bt = _rotation_tables(s, TM_PROJ)
    inner, xi_b, zeta_b, gc_b = _decay_tables()

    mod = _adaln_mod(c.reshape(d, 1), w_ada[0], b_ada[0][None, :])
    sh1, sc1, g1, sh2, sc2, g2 = [mod[:, i * d:(i + 1) * d] for i in range(6)]

    qt0, qt1, k0, k1, vt, ret = _in_proj(
        xs, sc1, sh1, w_all, w_t, bf_row, rot_a, rot_b, rot_at, rot_bt, eq_t, e_k,
        inner, xi_b, zeta_b, gc_b)
    fox = _fox_attention(qt0, qt1, k0, k1, vt)
    out = _mix_ffn(fox, ret, xs, wo[:FOX_WIDTH], wo[FOX_WIDTH:], g1, ln1_g[0][None, :], ln1_b[0][None, :],
                   sc2, sh2, g2, w_up[0].astype(BF16), conv_w[0], conv_b[0][None, :],
                   w_down[0].astype(BF16), ln2_g[0][None, :], ln2_b[0][None, :])
    return out[None]
```

```python
import math

import jax
import jax.numpy as jnp
import numpy as np
from jax import lax
from jax.experimental import pallas as pl
from jax.experimental.pallas import tpu as pltpu

F32 = jnp.float32
BF16 = jnp.bfloat16

D_MODEL = 1024
SEQ = 16384
FOX_HEADS = 8
FOX_HEAD_DIM = 64
FOX_WIDTH = FOX_HEADS * FOX_HEAD_DIM
RET_HEADS = 4
RET_HEAD_DIM = 128
RET_WIDTH = RET_HEADS * RET_HEAD_DIM
D_FF = 2816
CONV_WIDTH = 3
RET_CHUNK = 128
ROPE_BASE = 10000.0
LN_EPS = 1e-5
GN_EPS = 1e-6
DEPTH = 1
ALPHA = (2 * DEPTH) ** 0.25

LANES = 128
SUBLANES = 8
BF16_ROWS = 16
VMEM_LIMIT = 56 * 1024 * 1024

FOX_PAIRS = FOX_HEADS // 2
BIAS_CUM = 0
BIAS_ONE = 3
N_PIECES = 3
ONE_LANE = N_PIECES * FOX_HEADS
PV_ROWS = FOX_HEAD_DIM + BF16_ROWS
NEG_BIG = -1e30
LOG2_E = math.log2(math.e)
RSQRT2 = 2.0 ** -0.5

COL_K = 0
COL_F = COL_K + FOX_WIDTH
COL_RQ = COL_F + LANES
COL_RV = COL_RQ + RET_WIDTH
COL_RG = COL_RV + RET_WIDTH
N_COLS = COL_RG + RET_WIDTH
ROW_QT = 0
ROW_VT = ROW_QT + FOX_WIDTH
ROW_RKT = ROW_VT + FOX_WIDTH
N_ROWS_T = ROW_RKT + RET_WIDTH

TM_PROJ = 512
TQ = 512
TK = 1024
TK_SUB = 256
PAIRS_PER_TRIP = 2
TM_FFN = 512
FF_CHUNK = 256
SLAB_PAD_ROWS = 4


def _const_spec(shape):
    return pl.BlockSpec(shape, lambda *_: (0,) * len(shape), pipeline_mode=pl.Buffered(1))


def _split3(x):
    p0 = x.astype(BF16).astype(F32)
    r1 = x - p0
    p1 = r1.astype(BF16).astype(F32)
    p2 = (r1 - p1).astype(BF16).astype(F32)
    return p0, p1, p2


def _layer_norm(y, g, b):
    mu = jnp.mean(y, axis=-1, keepdims=True)
    d = y - mu
    var = jnp.mean(d * d, axis=-1, keepdims=True)
    return d * lax.rsqrt(var + LN_EPS) * g + b


def _mod_kernel(c_ref, w_ref, b_ref, o_ref):
    c = c_ref[...]
    sc = c * jax.nn.sigmoid(c)
    o_ref[...] = jnp.sum(w_ref[...] * sc, axis=0, keepdims=True) + b_ref[...]


def _adaln_mod(c_col, w_ada, b_ada):
    d, n = w_ada.shape
    tn = 2048
    return pl.pallas_call(
        _mod_kernel,
        grid=(n // tn,),
        in_specs=[pl.BlockSpec((d, 1), lambda j: (0, 0)),
                  pl.BlockSpec((d, tn), lambda j: (0, j)),
                  pl.BlockSpec((1, tn), lambda j: (0, j))],
        out_specs=pl.BlockSpec((1, tn), lambda j: (0, j)),
        out_shape=jax.ShapeDtypeStruct((1, n), F32),
        compiler_params=pltpu.CompilerParams(dimension_semantics=("arbitrary",)),
        name="adaln_mod",
    )(c_col, w_ada, b_ada)


def _in_proj_kernel(x_ref, sc_ref, sh_ref, w_ref, wt_ref, bf_ref, ra_ref, rb_ref, rat_ref, rbt_ref,
                    eqt_ref, ek_ref, inner_ref, xi_ref, zeta_ref, gc_ref,
                    wfox_ref, wret_ref, wup_ref, wdown_ref,
                    qt0_out, qt1_out, k0_out, k1_out, vt_out, ret_out,
                    wfox_out, wret_out, wup_out, wdown_out,
                    carry_ref, r_ref):
    tm = x_ref.shape[0]

    @pl.when(pl.program_id(0) == 0)
    def _():
        carry_ref[...] = jnp.zeros_like(carry_ref)
        r_ref[...] = jnp.zeros_like(r_ref)

    h = (x_ref[...] * (1.0 + sc_ref[...]) + sh_ref[...]).astype(BF16)

    def proj(c0, width):
        return jnp.dot(h, w_ref[:, c0:c0 + width], preferred_element_type=F32)

    def proj_t(r0, height):
        return lax.dot_general(wt_ref[r0:r0 + height, :], h, (((1,), (1,)), ((), ())),
                               preferred_element_type=F32)

    lane = lax.broadcasted_iota(jnp.int32, (tm, LANES), 1)
    head_lane = lane < FOX_HEADS
    ff = proj(COL_F, LANES) + bf_ref[...]
    two = 2 * RET_HEAD_DIM
    rq_raw = [proj(COL_RQ + pair * two, two) for pair in range(RET_HEADS // 2)]
    rv = proj(COL_RV, RET_WIDTH).astype(BF16)
    rg = proj(COL_RG, RET_WIDTH)
    rkt_raw = proj_t(ROW_RKT, RET_WIDTH)

    logf = jnp.minimum(ff, 0.0) - jnp.log1p(jnp.exp(-jnp.abs(ff)))
    logf = jnp.where(head_lane, logf, 0.0)
    p0, p1, p2 = _split3(logf)
    pieces = p0 + pltpu.roll(p1, FOX_HEADS, 1) + pltpu.roll(p2, 2 * FOX_HEADS, 1)
    row = lax.broadcasted_iota(jnp.int32, (tm, tm), 0)
    col = lax.broadcasted_iota(jnp.int32, (tm, tm), 1)
    tril = jnp.where(row >= col, 1.0, 0.0).astype(BF16)
    c3 = jnp.dot(tril, pieces.astype(BF16), preferred_element_type=F32)

    half = RET_HEAD_DIM // 2
    ca, sa = ra_ref[0, 0:1, :], ra_ref[0, 1:2, :]
    cos = ca * rb_ref[0] - sa * rb_ref[1]
    sin = sa * rb_ref[2] + ca * rb_ref[3]
    cat, sat = rat_ref[0, :, 0:1], rat_ref[0, :, 1:2]
    cost = cat * rbt_ref[0] - sat * rbt_ref[1]
    sint = sat * rbt_ref[2] + cat * rbt_ref[3]
    kscale = RET_HEAD_DIM ** -0.5
    rq, rkt = [], []
    for a2 in rq_raw:
        for e in range(2):
            a = a2[:, e * RET_HEAD_DIM:(e + 1) * RET_HEAD_DIM]
            rq.append((a * cos + pltpu.roll(a, half, 1) * sin).astype(BF16))
    for hh in range(RET_HEADS):
        b = rkt_raw[hh * RET_HEAD_DIM:(hh + 1) * RET_HEAD_DIM]
        b_rot = jnp.concatenate([b[half:], b[:half]], axis=0)
        rkt.append(((b * cost + b_rot * sint) * kscale).astype(BF16))

    c = RET_CHUNK
    n_chunks = tm // c
    tiles = [(hh, ci) for hh in range(RET_HEADS) for ci in range(n_chunks)]

    def head_cols(hh):
        return slice(hh * RET_HEAD_DIM, (hh + 1) * RET_HEAD_DIM)

    def chunk_rows(ci):
        return slice(ci * c, (ci + 1) * c)

    intra, kv = {}, {}
    for hh, ci in tiles:
        kt = rkt[hh][:, chunk_rows(ci)]
        v = rv[chunk_rows(ci), head_cols(hh)]
        intra[hh, ci] = (jnp.dot(rq[hh][chunk_rows(ci)], kt, preferred_element_type=F32)
                         * inner_ref[hh]).astype(BF16)
        kzt = (kt.astype(F32) * zeta_ref[hh]).astype(BF16)
        kv[hh, ci] = jnp.dot(kzt, v, preferred_element_type=F32)

    wfox_out[...] = wfox_ref[...].astype(BF16)
    wret_out[...] = wret_ref[...].astype(BF16)
    wup_out[...] = wup_ref[...].astype(BF16)
    wdown_out[...] = wdown_ref[...].astype(BF16)

    k = proj(COL_K, FOX_WIDTH)
    cum = c3 + pltpu.roll(c3, LANES - FOX_HEADS, 1) + pltpu.roll(c3, LANES - 2 * FOX_HEADS, 1)
    cum = jnp.where(head_lane, cum, 0.0) + carry_ref[0:1, :]
    carry_ref[0:1, :] = cum[tm - 1:tm, :]
    c0_, c1_, c2_ = _split3(cum * LOG2_E)
    cum_pieces = c0_ + pltpu.roll(c1_, FOX_HEADS, 1) + pltpu.roll(c2_, 2 * FOX_HEADS, 1)
    cum_pieces = jnp.where(lane == ONE_LANE, 1.0, cum_pieces)
    kb = jnp.dot(cum_pieces.astype(BF16), ek_ref[...], preferred_element_type=F32)
    qbt = jnp.dot(eqt_ref[...], cum_pieces.T.astype(BF16), preferred_element_type=F32)
    even_own = (lax.broadcasted_iota(jnp.int32, (1, FOX_WIDTH), 1) % LANES) < FOX_HEAD_DIM
    even_own_t = (lax.broadcasted_iota(jnp.int32, (FOX_WIDTH, 1), 0) % LANES) < FOX_HEAD_DIM
    k0_out[...] = jnp.where(even_own, k, kb).astype(BF16)
    k1_out[...] = jnp.where(even_own, kb, k).astype(BF16)

    state = {}
    for hh in range(RET_HEADS):
        r_state = r_ref[hh]
        for ci in range(n_chunks):
            state[hh, ci] = r_state.astype(BF16)
            r_state = r_state * gc_ref[hh] + kv[hh, ci]
        r_ref[hh] = r_state

    for hh, ci in tiles:
        v = rv[chunk_rows(ci), head_cols(hh)]
        o = (jnp.dot(intra[hh, ci], v, preferred_element_type=F32)
             + jnp.dot(rq[hh][chunk_rows(ci)], state[hh, ci], preferred_element_type=F32) * xi_ref[hh])
        mu = jnp.mean(o, axis=-1, keepdims=True)
        d = o - mu
        var = jnp.mean(d * d, axis=-1, keepdims=True)
        on = d * lax.rsqrt(var + GN_EPS)
        g = rg[chunk_rows(ci), head_cols(hh)]
        ret_out[chunk_rows(ci), head_cols(hh)] = (g * jax.nn.sigmoid(g) * on).astype(BF16)

    tp = proj_t(ROW_QT, 2 * FOX_WIDTH)
    qt = tp[:FOX_WIDTH] * (FOX_HEAD_DIM ** -0.5 * LOG2_E)
    qt0_out[...] = jnp.where(even_own_t, qt, qbt).astype(BF16)
    qt1_out[...] = jnp.where(even_own_t, qbt, qt).astype(BF16)
    vt_out[...] = tp[FOX_WIDTH:].astype(BF16)


def _in_proj(x, sc1, sh1, w_all, w_t, bf_row, rot_a, rot_b, rot_at, rot_bt, eq_t, e_k,
             inner, xi_b, zeta_b, gc_b, w_out, w_up, w_down):
    s, d = x.shape
    tm = TM_PROJ
    n = s // tm
    row = lambda w: pl.BlockSpec((tm, w), lambda i: (i, 0))
    col = lambda h: pl.BlockSpec((h, tm), lambda i: (0, i))
    tab = _const_spec((RET_HEADS, RET_CHUNK, RET_CHUNK))
    fox_rows, ret_rows, up_rows, down_rows = FOX_WIDTH // n, RET_WIDTH // n, d // n, 2 * D_FF // n
    assert fox_rows % 16 == 0 and ret_rows % 16 == 0 and up_rows % 16 == 0 and down_rows % 16 == 0
    assert n % 2 == 0 and FOX_WIDTH % fox_rows == 0
    cast_specs = [pl.BlockSpec((fox_rows, d), lambda i: (i, 0)),
                  pl.BlockSpec((ret_rows, d), lambda i: (FOX_WIDTH // ret_rows + i, 0)),
                  pl.BlockSpec((up_rows, 2 * D_FF), lambda i: (i, 0)),
                  pl.BlockSpec((down_rows, d), lambda i: (i // 2, 0))]
    cast_out_specs = [pl.BlockSpec((fox_rows, d), lambda i: (i, 0)),
                      pl.BlockSpec((ret_rows, d), lambda i: (i, 0)),
                      pl.BlockSpec((up_rows, 2 * D_FF), lambda i: (i, 0)),
                      pl.BlockSpec((down_rows, d), lambda i: (i // 2, 0))]
    out_shapes = ([jax.ShapeDtypeStruct((FOX_WIDTH, s), BF16)] * 2
                  + [jax.ShapeDtypeStruct((s, FOX_WIDTH), BF16)] * 2
                  + [jax.ShapeDtypeStruct((FOX_WIDTH, s), BF16),
                     jax.ShapeDtypeStruct((s, RET_WIDTH), BF16),
                     jax.ShapeDtypeStruct((FOX_WIDTH, d), BF16),
                     jax.ShapeDtypeStruct((RET_WIDTH, d), BF16),
                     jax.ShapeDtypeStruct((d, 2 * D_FF), BF16),
                     jax.ShapeDtypeStruct((D_FF, d), BF16)])
    return pl.pallas_call(
        _in_proj_kernel,
        grid=(s // tm,),
        in_specs=[row(d), _const_spec((1, d)), _const_spec((1, d)),
                  _const_spec((d, N_COLS)), _const_spec((N_ROWS_T, d)), _const_spec((1, LANES)),
                  pl.BlockSpec((1, 2, RET_HEAD_DIM), lambda i: (i, 0, 0)),
                  _const_spec((4, tm, RET_HEAD_DIM)),
                  pl.BlockSpec((1, RET_HEAD_DIM, 2), lambda i: (i, 0, 0)),
                  _const_spec((4, RET_HEAD_DIM, tm)),
                  _const_spec((FOX_WIDTH, LANES)), _const_spec((LANES, FOX_WIDTH)),
                  tab, tab, tab, _const_spec((RET_HEADS, 1, RET_HEAD_DIM))] + cast_specs,
        out_specs=([col(FOX_WIDTH)] * 2 + [row(FOX_WIDTH)] * 2 + [col(FOX_WIDTH), row(RET_WIDTH)]
                   + cast_out_specs),
        out_shape=out_shapes,
        scratch_shapes=[pltpu.VMEM((SUBLANES, LANES), F32),
                        pltpu.VMEM((RET_HEADS, RET_HEAD_DIM, RET_HEAD_DIM), F32)],
        compiler_params=pltpu.CompilerParams(dimension_semantics=("arbitrary",),
                                             vmem_limit_bytes=VMEM_LIMIT),
        name="in_proj",
    )(x, sc1, sh1, w_all, w_t, bf_row, rot_a, rot_b, rot_at, rot_bt, eq_t, e_k,
      inner, xi_b, zeta_b, gc_b, w_out, w_out, w_up, w_down)


def _fox_kernel(qt0_ref, qt1_ref, qtn0_ref, qtn1_ref, k0_ref, k1_ref, vt_ref, o_ref, m_ref, acc_ref,
                sa_ref, ca_ref, sb_ref, cb_ref, sc_ref, cc_ref):
    i = pl.program_id(1)
    k_refs = (k0_ref, k1_ref)
    qt = [qt0_ref[...], qt1_ref[...]]
    qt_next = [qtn0_ref[...], qtn1_ref[...]]
    m_ref[...] = jnp.full_like(m_ref, NEG_BIG)
    acc_ref[...] = jnp.zeros_like(acc_ref)
    n_sub = TK // TK_SUB

    def qk(e, q_aug, start):
        return jnp.dot(k_refs[e][pl.ds(start, TK_SUB), :], q_aug, preferred_element_type=F32)

    def values(e, start, width):
        return jnp.concatenate(
            [vt_ref[e * FOX_HEAD_DIM:(e + 1) * FOX_HEAD_DIM, pl.ds(start, width)],
             jnp.ones((BF16_ROWS, width), BF16)], axis=0)

    def pv(e, st, m_new, start, width):
        pt = jnp.exp2(st - m_new).astype(BF16)
        return jnp.dot(values(e, start, width), pt, preferred_element_type=F32)

    def score_sub(e, q_aug, j, sub, s_ref, cmax):
        st = qk(e, q_aug, pl.multiple_of(j * TK + sub * TK_SUB, TK_SUB))
        s_ref[e, sub * TK_SUB:(sub + 1) * TK_SUB, :] = st
        cm = jnp.max(st, axis=0, keepdims=True)
        return cm if cmax is None else jnp.maximum(cmax, cm)

    def scores(j, e, s_ref, c_ref):
        cmax = None
        for sub in range(n_sub):
            cmax = score_sub(e, qt[e], j, sub, s_ref, cmax)
        c_ref[e] = cmax

    def stage(e, j_next, buf_next, j_cur, buf_cur):
        (s_next, c_next), (s_cur, c_cur) = buf_next, buf_cur
        m_prev = m_ref[e]
        m_new = jnp.maximum(m_prev, c_cur[e])
        acc = jnp.exp2(m_prev - m_new) * acc_ref[e]
        cmax = None
        for sub in range(n_sub):
            rows = slice(sub * TK_SUB, (sub + 1) * TK_SUB)
            cmax = score_sub(e, qt[e], j_next, sub, s_next, cmax)
            acc = acc + pv(e, s_cur[e, rows, :], m_new,
                           pl.multiple_of(j_cur * TK + sub * TK_SUB, TK_SUB), TK_SUB)
        c_next[e] = cmax
        acc_ref[e] = acc
        m_ref[e] = m_new

    def finish(j, s_ref, fill_next):
        n_pieces = TK // TQ
        tri = (lax.broadcasted_iota(jnp.int32, (TQ, TQ), 0)
               <= lax.broadcasted_iota(jnp.int32, (TQ, TQ), 1))
        for diag in range(n_pieces):
            @pl.when(i % n_pieces == diag)
            def _():
                for e in range(2):
                    cmax = None
                    for piece in range(n_pieces):
                        if fill_next:
                            for sub in range(piece * n_sub // n_pieces, (piece + 1) * n_sub // n_pieces):
                                cmax = score_sub(e, qt_next[e], 0, sub, sc_ref, cmax)
                        if piece > diag:
                            continue
                        st = s_ref[e, piece * TQ:(piece + 1) * TQ, :]
                        if piece == diag:
                            st = jnp.where(tri, st, NEG_BIG)
                        m_prev = m_ref[e]
                        m_new = jnp.maximum(m_prev, jnp.max(st, axis=0, keepdims=True))
                        start = pl.multiple_of(j * TK + piece * TQ, TQ)
                        acc_ref[e] = (jnp.exp2(m_prev - m_new) * acc_ref[e]
                                      + pv(e, st, m_new, start, TQ))
                        m_ref[e] = m_new
                    if fill_next:
                        cc_ref[e] = cmax

    buf_a = (sa_ref, ca_ref)
    buf_b = (sb_ref, cb_ref)
    buf_c = (sc_ref, cc_ref)

    nfull = (i * TQ) // TK
    prefilled = jnp.logical_and(i > 0, ((i - 1) * TQ) // TK >= 1)

    @pl.when(jnp.logical_not(prefilled))
    def _():
        for e in range(2):
            scores(0, e, *buf_c)

    @pl.when(nfull >= 1)
    def _():
        for e in range(2):
            stage(e, 1, buf_a, 0, buf_c)

    def pair(jj, carry):
        j = 2 * jj + 1
        for e in range(2):
            stage(e, j + 1, buf_b, j, buf_a)
        for e in range(2):
            stage(e, j + 2, buf_a, j + 1, buf_b)
        return carry

    def group(jg, carry):
        for u in range(PAIRS_PER_TRIP):
            carry = pair(PAIRS_PER_TRIP * jg + u, carry)
        return carry

    n_pairs = jnp.maximum(nfull - 1, 0) // 2
    n_groups = n_pairs // PAIRS_PER_TRIP
    lax.fori_loop(0, n_groups, group, 0)
    lax.fori_loop(PAIRS_PER_TRIP * n_groups, n_pairs, pair, 0)

    @pl.when(nfull == 0)
    def _():
        finish(nfull, sc_ref, False)

    @pl.when(nfull % 2 == 1)
    def _():
        finish(nfull, sa_ref, True)

    @pl.when(jnp.logical_and(nfull >= 2, nfull % 2 == 0))
    def _():
        for e in range(2):
            stage(e, nfull, buf_b, nfull - 1, buf_a)
        finish(nfull, sb_ref, True)

    outs = []
    for e in range(2):
        acc = acc_ref[e]
        outs.append(acc[:FOX_HEAD_DIM] / acc[FOX_HEAD_DIM:FOX_HEAD_DIM + 1])
    o_ref[...] = jnp.concatenate(outs, axis=0).T.astype(o_ref.dtype)


def _fox_attention(qt0, qt1, k0, k1, vt):
    s = k0.shape[0]
    assert TK % TQ == 0 and TK % TK_SUB == 0 and s % TK == 0
    n_q = s // TQ
    blk_t = pl.BlockSpec((LANES, TQ), lambda p, i: (p, i))
    nxt_t = pl.BlockSpec((LANES, TQ), lambda p, i: (p, jnp.minimum(i + 1, n_q - 1)))
    res = pl.BlockSpec((s, LANES), lambda p, i: (0, p))
    return pl.pallas_call(
        _fox_kernel,
        grid=(FOX_PAIRS, n_q),
        in_specs=[blk_t, blk_t, nxt_t, nxt_t, res, res, pl.BlockSpec((LANES, s), lambda p, i: (p, 0))],
        out_specs=pl.BlockSpec((TQ, LANES), lambda p, i: (i, p)),
        out_shape=jax.ShapeDtypeStruct((s, FOX_WIDTH), BF16),
        scratch_shapes=[pltpu.VMEM((2, 1, TQ), F32), pltpu.VMEM((2, PV_ROWS, TQ), F32),
                        pltpu.VMEM((2, TK, TQ), F32), pltpu.VMEM((2, 1, TQ), F32),
                        pltpu.VMEM((2, TK, TQ), F32), pltpu.VMEM((2, 1, TQ), F32),
                        pltpu.VMEM((2, TK, TQ), F32), pltpu.VMEM((2, 1, TQ), F32)],
        compiler_params=pltpu.CompilerParams(dimension_semantics=("arbitrary", "arbitrary"),
                                             vmem_limit_bytes=VMEM_LIMIT),
        name="fox_attention",
    )(qt0, qt1, qt0, qt1, k0, k1, vt)


def _mix_ffn_kernel(fox_ref, ret_ref, x_ref, fox0_ref, ret0_ref, x0_ref,
                    wf_ref, wr_ref, g1_ref, l1g_ref, l1b_ref,
                    sc_ref, sh_ref, g2_ref, wu_ref, cw_ref, cb_ref, wd_ref, l2g_ref, l2b_ref,
                    o_ref, carry_ref, slab_in_ref, slab_out_ref, y_ref, x1_ref):
    tm = x_ref.shape[0]
    grp = tm // SUBLANES
    pitch = grp + SLAB_PAD_ROWS
    n_slab = D_MODEL // LANES

    def front(fox, ret, x):
        mix = (jnp.dot(fox, wf_ref[...], preferred_element_type=F32)
               + jnp.dot(ret, wr_ref[...], preferred_element_type=F32))
        x1_nat = _layer_norm(ALPHA * x + g1_ref[...] * mix, l1g_ref[...], l1b_ref[...])
        for c in range(n_slab):
            for s in range(SUBLANES):
                slab_in_ref[c, s * pitch:s * pitch + grp, :] = x1_nat[s * grp:(s + 1) * grp,
                                                                      c * LANES:(c + 1) * LANES]
        return jnp.concatenate(
            [jnp.concatenate([slab_in_ref[c, pl.ds(v, SUBLANES, stride=pitch), :] for v in range(grp)],
                             axis=0)
             for c in range(n_slab)], axis=1)

    @pl.when(pl.program_id(0) == 0)
    def _():
        carry_ref[...] = jnp.zeros_like(carry_ref)
        x1_ref[...] = front(fox0_ref[...], ret0_ref[...], x0_ref[...])

    h = (x1_ref[...] * (1.0 + sc_ref[...]) + sh_ref[...]).astype(BF16)
    first_sublane = lax.broadcasted_iota(jnp.int32, (SUBLANES, FF_CHUNK), 0) == 0

    def conv_up(c0):
        cols = slice(c0, c0 + FF_CHUNK)
        up = jnp.dot(h, wu_ref[:, cols], preferred_element_type=F32)
        prev = carry_ref[:, cols]
        tail = []
        for g in range(2):
            rows = slice(tm - (2 - g) * SUBLANES, tm - (1 - g) * SUBLANES)
            tail.append(jnp.where(first_sublane,
                                  pltpu.roll(prev[g * SUBLANES:(g + 1) * SUBLANES], 1, 0),
                                  pltpu.roll(up[rows], 1, 0)))
        carry_ref[:, cols] = up[tm - 2 * SUBLANES:tm]
        back1 = jnp.concatenate([tail[1], up[:tm - SUBLANES]], axis=0)
        back2 = jnp.concatenate([tail[0], tail[1], up[:tm - 2 * SUBLANES]], axis=0)
        cw = cw_ref[:, cols] * RSQRT2
        return cb_ref[:, cols] * RSQRT2 + back2 * cw[0:1] + back1 * cw[1:2] + up * cw[2:3]

    for ci in range(D_FF // FF_CHUNK):
        a = conv_up(ci * FF_CHUNK)
        b = conv_up(D_FF + ci * FF_CHUNK)
        y_ref[:, ci * FF_CHUNK:(ci + 1) * FF_CHUNK] = (a * (1.0 + lax.erf(a)) * b).astype(BF16)

    x1_next = front(fox_ref[...], ret_ref[...], x_ref[...])

    ffn = jnp.dot(y_ref[...], wd_ref[...], preferred_element_type=F32)
    out = _layer_norm(ALPHA * x1_ref[...] + g2_ref[...] * ffn, l2g_ref[...], l2b_ref[...])
    x1_ref[...] = x1_next

    for c in range(n_slab):
        for v in range(grp):
            slab_out_ref[c, pl.ds(v, SUBLANES, stride=pitch), :] = out[v * SUBLANES:(v + 1) * SUBLANES,
                                                                       c * LANES:(c + 1) * LANES]
    for c in range(n_slab):
        for s in range(SUBLANES):
            o_ref[s * grp:(s + 1) * grp, c * LANES:(c + 1) * LANES] = slab_out_ref[c, s * pitch:s * pitch + grp, :]


def _mix_ffn(fox, ret, x, w_fox, w_ret, g1, ln1_g, ln1_b, sc2, sh2, g2, w_up, conv_w, conv_b, w_down,
             ln2_g, ln2_b):
    s, d = x.shape
    tm = TM_FFN
    n_tiles = s // tm
    nxt = lambda w: pl.BlockSpec((tm, w), lambda i: (jnp.minimum(i + 1, n_tiles - 1), 0))
    vec = _const_spec((1, d))
    slab = pltpu.VMEM((d // LANES, tm + SUBLANES * SLAB_PAD_ROWS, LANES), F32)
    return pl.pallas_call(
        _mix_ffn_kernel,
        grid=(n_tiles,),
        in_specs=[nxt(FOX_WIDTH), nxt(RET_WIDTH), nxt(d),
                  _const_spec((tm, FOX_WIDTH)), _const_spec((tm, RET_WIDTH)), _const_spec((tm, d)),
                  _const_spec((FOX_WIDTH, d)), _const_spec((RET_WIDTH, d)), vec, vec, vec,
                  vec, vec, vec,
                  _const_spec((d, 2 * D_FF)), _const_spec((CONV_WIDTH, 2 * D_FF)),
                  _const_spec((1, 2 * D_FF)), _const_spec((D_FF, d)), vec, vec],
        out_specs=pl.BlockSpec((tm, d), lambda i: (i, 0)),
        out_shape=jax.ShapeDtypeStruct((s, d), F32),
        scratch_shapes=[pltpu.VMEM((2 * SUBLANES, 2 * D_FF), F32), slab, slab,
                        pltpu.VMEM((tm, D_FF), BF16), pltpu.VMEM((tm, d), F32)],
        compiler_params=pltpu.CompilerParams(dimension_semantics=("arbitrary",),
                                             vmem_limit_bytes=VMEM_LIMIT),
        name="mix_ffn",
    )(fox, ret, x, fox, ret, x, w_fox, w_ret, g1, ln1_g, ln1_b, sc2, sh2, g2, w_up, conv_w, conv_b,
      w_down, ln2_g, ln2_b)


def _placement_tables():
    e_q = np.zeros((LANES, FOX_WIDTH), np.float32)
    e_k = np.zeros((LANES, FOX_WIDTH), np.float32)
    for h in range(FOX_HEADS):
        base = (h // 2) * LANES + (FOX_HEAD_DIM if h % 2 == 0 else 0)
        for j in range(N_PIECES):
            e_q[j * FOX_HEADS + h, base + BIAS_CUM + j] = 1.0
            e_q[ONE_LANE, base + BIAS_ONE + j] = 1.0
            e_k[ONE_LANE, base + BIAS_CUM + j] = 1.0
            e_k[j * FOX_HEADS + h, base + BIAS_ONE + j] = -1.0
    return jnp.asarray(e_q.T, BF16), jnp.asarray(e_k, BF16)


def _rotation_tables(s, tm):
    dk = RET_HEAD_DIM
    inv_freq = ROPE_BASE ** (-np.arange(0, dk, 2, dtype=np.float64) / dk)
    inv_freq = np.concatenate([inv_freq, inv_freq])
    sign = np.concatenate([-np.ones(dk // 2), np.ones(dk // 2)])
    ang_a = (np.arange(s // tm, dtype=np.float64) * tm)[:, None] * inv_freq[None, :]
    ang_b = np.arange(tm, dtype=np.float64)[:, None] * inv_freq[None, :]
    rot_a = np.stack([np.cos(ang_a), np.sin(ang_a)], axis=1)
    rot_b = np.stack([np.cos(ang_b), np.sin(ang_b), sign * np.cos(ang_b), sign * np.sin(ang_b)])
    f32 = lambda a: jnp.asarray(a, F32)
    return f32(rot_a), f32(rot_b), f32(rot_a.transpose(0, 2, 1)), f32(rot_b.transpose(0, 2, 1))


def _decay_tables():
    c = RET_CHUNK
    log_gamma = np.log1p(-np.exp2(-5.0 - np.arange(RET_HEADS, dtype=np.float64)))
    idx = np.arange(c, dtype=np.float64)
    diff = idx[:, None] - idx[None, :]
    inner = np.where(diff[None] >= 0,
                     np.exp(np.maximum(diff, 0.0)[None] * log_gamma[:, None, None]), 0.0)
    xi = np.exp((idx[None, :] + 1.0) * log_gamma[:, None])
    zeta = np.exp((c - 1.0 - idx[None, :]) * log_gamma[:, None])
    g_chunk = np.exp(c * log_gamma)
    bshape = (RET_HEADS, c, RET_HEAD_DIM)
    f32 = lambda a: jnp.asarray(a, F32)
    return (f32(inner),
            f32(np.broadcast_to(xi[:, :, None], bshape)),
            f32(np.broadcast_to(zeta[:, None, :], bshape)),
            f32(np.broadcast_to(g_chunk[:, None, None], (RET_HEADS, 1, RET_HEAD_DIM))))


def kernel(x, c, w_ada, b_ada, w_in, b_f, w_out, ln1_g, ln1_b, w_up, conv_w, conv_b, w_down, ln2_g, ln2_b):
    b, s, d = x.shape
    assert (b, s, d) == (1, SEQ, D_MODEL) and w_ada.shape[0] == DEPTH
    xs = x[0]

    wi = w_in[0]
    o = 0
    parts = {}
    for name, width in (("fq", FOX_WIDTH), ("fk", FOX_WIDTH), ("fv", FOX_WIDTH), ("ff", FOX_HEADS),
                        ("rq", RET_WIDTH), ("rk", RET_WIDTH), ("rv", RET_WIDTH), ("rg", RET_WIDTH)):
        parts[name] = wi[:, o:o + width]
        o += width
    w_all = jnp.concatenate(
        [parts["fk"], jnp.pad(parts["ff"], ((0, 0), (0, LANES - FOX_HEADS))),
         parts["rq"], parts["rv"], parts["rg"]], axis=1).astype(BF16)
    w_t = jnp.concatenate([parts["fq"], parts["fv"], parts["rk"]], axis=1).T.astype(BF16)
    bf_row = jnp.pad(b_f[0][None, :], ((0, 0), (0, LANES - FOX_HEADS)))

    eq_t, e_k = _placement_tables()
    rot_a, rot_b, rot_at, rot_bt = _rotation_tables(s, TM_PROJ)
    inner, xi_b, zeta_b, gc_b = _decay_tables()

    mod = _adaln_mod(c.reshape(d, 1), w_ada[0], b_ada[0][None, :])
    sh1, sc1, g1, sh2, sc2, g2 = [mod[:, i * d:(i + 1) * d] for i in range(6)]

    qt0, qt1, k0, k1, vt, ret, w_fox, w_ret, w_up_b, w_down_b = _in_proj(
        xs, sc1, sh1, w_all, w_t, bf_row, rot_a, rot_b, rot_at, rot_bt, eq_t, e_k,
        inner, xi_b, zeta_b, gc_b, w_out[0], w_up[0], w_down[0])
    fox = _fox_attention(qt0, qt1, k0, k1, vt)
    out = _mix_ffn(fox, ret, xs, w_fox, w_ret, g1, ln1_g[0][None, :], ln1_b[0][None, :],
                   sc2, sh2, g2, w_up_b, conv_w[0], conv_b[0][None, :],
                   w_down_b, ln2_g[0][None, :], ln2_b[0][None, :])
    return out[None]
```

```python
import math

import jax
import jax.numpy as jnp
import numpy as np
from jax import lax
from jax.experimental import pallas as pl
from jax.experimental.pallas import tpu as pltpu

F32 = jnp.float32
BF16 = jnp.bfloat16

D_MODEL = 1024
SEQ = 16384
FOX_HEADS = 8
FOX_HEAD_DIM = 64
FOX_WIDTH = FOX_HEADS * FOX_HEAD_DIM
RET_HEADS = 4
RET_HEAD_DIM = 128
RET_WIDTH = RET_HEADS * RET_HEAD_DIM
D_FF = 2816
CONV_WIDTH = 3
RET_CHUNK = 128
ROPE_BASE = 10000.0
LN_EPS = 1e-5
GN_EPS = 1e-6
DEPTH = 1
ALPHA = (2 * DEPTH) ** 0.25

LANES = 128
SUBLANES = 8
BF16_ROWS = 16
VMEM_LIMIT = 56 * 1024 * 1024

FOX_PAIRS = FOX_HEADS // 2
BIAS_CUM = 0
BIAS_ONE = 3
N_PIECES = 3
ONE_LANE = N_PIECES * FOX_HEADS
PV_ROWS = FOX_HEAD_DIM + BF16_ROWS
NEG_BIG = -1e30
LOG2_E = math.log2(math.e)
RSQRT2 = 2.0 ** -0.5

COL_K = 0
COL_F = COL_K + FOX_WIDTH
COL_RQ = COL_F + LANES
COL_RV = COL_RQ + RET_WIDTH
COL_RG = COL_RV + RET_WIDTH
N_COLS = COL_RG + RET_WIDTH
ROW_QT = 0
ROW_VT = ROW_QT + FOX_WIDTH
ROW_RKT = ROW_VT + FOX_WIDTH
N_ROWS_T = ROW_RKT + RET_WIDTH

TM_PROJ = 512
TQ = 512
TK = 1024
TK_SUB = 256
PAIRS_PER_TRIP = 2
TM_FFN = 512
FF_CHUNK = 256
SLAB_PAD_ROWS = 4


def _const_spec(shape):
    return pl.BlockSpec(shape, lambda *_: (0,) * len(shape), pipeline_mode=pl.Buffered(1))


def _split3(x):
    p0 = x.astype(BF16).astype(F32)
    r1 = x - p0
    p1 = r1.astype(BF16).astype(F32)
    p2 = (r1 - p1).astype(BF16).astype(F32)
    return p0, p1, p2


def _layer_norm(y, g, b):
    mu = jnp.mean(y, axis=-1, keepdims=True)
    d = y - mu
    var = jnp.mean(d * d, axis=-1, keepdims=True)
    return d * lax.rsqrt(var + LN_EPS) * g + b


def _mod_kernel(c_ref, w_ref, b_ref, o_ref):
    c = c_ref[...]
    sc = c * jax.nn.sigmoid(c)
    o_ref[...] = jnp.sum(w_ref[...] * sc, axis=0, keepdims=True) + b_ref[...]


def _adaln_mod(c_col, w_ada, b_ada, n):
    d = w_ada.shape[0]
    tn = 512
    return pl.pallas_call(
        _mod_kernel,
        grid=(n // tn,),
        in_specs=[pl.BlockSpec((d, 1), lambda j: (0, 0)),
                  pl.BlockSpec((d, tn), lambda j: (0, j)),
                  pl.BlockSpec((1, tn), lambda j: (0, j))],
        out_specs=pl.BlockSpec((1, tn), lambda j: (0, j)),
        out_shape=jax.ShapeDtypeStruct((1, n), F32),
        compiler_params=pltpu.CompilerParams(dimension_semantics=("arbitrary",)),
        name="adaln_mod",
    )(c_col, w_ada, b_ada)


def _in_proj_kernel(x_ref, sc_ref, sh_ref, w_ref, wt_ref, bf_ref, ra_ref, rb_ref, rat_ref, rbt_ref,
                    eqt_ref, ek_ref, inner_ref, xi_ref, zeta_ref, gc_ref,
                    wfox_ref, wret_ref, wup_ref, wdown_ref, c_ref, wada_a_ref, wada_b_ref, bada_ref,
                    qt0_out, qt1_out, k0_out, k1_out, vt_out, ret_out,
                    wfox_out, wret_out, wup_out, wdown_out, mod_out,
                    carry_ref, r_ref):
    tm = x_ref.shape[0]

    @pl.when(pl.program_id(0) == 0)
    def _():
        carry_ref[...] = jnp.zeros_like(carry_ref)
        r_ref[...] = jnp.zeros_like(r_ref)

    h = (x_ref[...] * (1.0 + sc_ref[...]) + sh_ref[...]).astype(BF16)

    def proj(c0, width):
        return jnp.dot(h, w_ref[:, c0:c0 + width], preferred_element_type=F32)

    def proj_t(r0, height):
        return lax.dot_general(wt_ref[r0:r0 + height, :], h, (((1,), (1,)), ((), ())),
                               preferred_element_type=F32)

    lane = lax.broadcasted_iota(jnp.int32, (tm, LANES), 1)
    head_lane = lane < FOX_HEADS
    ff = proj(COL_F, LANES) + bf_ref[...]
    two = 2 * RET_HEAD_DIM
    rq_raw = [proj(COL_RQ + pair * two, two) for pair in range(RET_HEADS // 2)]
    rv = proj(COL_RV, RET_WIDTH).astype(BF16)
    rg = proj(COL_RG, RET_WIDTH)
    rkt_raw = proj_t(ROW_RKT, RET_WIDTH)

    logf = jnp.minimum(ff, 0.0) - jnp.log1p(jnp.exp(-jnp.abs(ff)))
    logf = jnp.where(head_lane, logf, 0.0)
    p0, p1, p2 = _split3(logf)
    pieces = p0 + pltpu.roll(p1, FOX_HEADS, 1) + pltpu.roll(p2, 2 * FOX_HEADS, 1)
    row = lax.broadcasted_iota(jnp.int32, (tm, tm), 0)
    col = lax.broadcasted_iota(jnp.int32, (tm, tm), 1)
    tril = jnp.where(row >= col, 1.0, 0.0).astype(BF16)
    c3 = jnp.dot(tril, pieces.astype(BF16), preferred_element_type=F32)

    half = RET_HEAD_DIM // 2
    ca, sa = ra_ref[0, 0:1, :], ra_ref[0, 1:2, :]
    cos = ca * rb_ref[0] - sa * rb_ref[1]
    sin = sa * rb_ref[2] + ca * rb_ref[3]
    cat, sat = rat_ref[0, :, 0:1], rat_ref[0, :, 1:2]
    cost = cat * rbt_ref[0] - sat * rbt_ref[1]
    sint = sat * rbt_ref[2] + cat * rbt_ref[3]
    kscale = RET_HEAD_DIM ** -0.5
    rq, rkt = [], []
    for a2 in rq_raw:
        for e in range(2):
            a = a2[:, e * RET_HEAD_DIM:(e + 1) * RET_HEAD_DIM]
            rq.append((a * cos + pltpu.roll(a, half, 1) * sin).astype(BF16))
    for hh in range(RET_HEADS):
        b = rkt_raw[hh * RET_HEAD_DIM:(hh + 1) * RET_HEAD_DIM]
        b_rot = jnp.concatenate([b[half:], b[:half]], axis=0)
        rkt.append(((b * cost + b_rot * sint) * kscale).astype(BF16))

    c = RET_CHUNK
    n_chunks = tm // c
    tiles = [(hh, ci) for hh in range(RET_HEADS) for ci in range(n_chunks)]

    def head_cols(hh):
        return slice(hh * RET_HEAD_DIM, (hh + 1) * RET_HEAD_DIM)

    def chunk_rows(ci):
        return slice(ci * c, (ci + 1) * c)

    intra, kv = {}, {}
    for hh, ci in tiles:
        kt = rkt[hh][:, chunk_rows(ci)]
        v = rv[chunk_rows(ci), head_cols(hh)]
        intra[hh, ci] = (jnp.dot(rq[hh][chunk_rows(ci)], kt, preferred_element_type=F32)
                         * inner_ref[hh]).astype(BF16)
        kzt = (kt.astype(F32) * zeta_ref[hh]).astype(BF16)
        kv[hh, ci] = jnp.dot(kzt, v, preferred_element_type=F32)

    wfox_out[...] = wfox_ref[...].astype(BF16)
    wret_out[...] = wret_ref[...].astype(BF16)
    wup_out[...] = wup_ref[...].astype(BF16)
    wdown_out[...] = wdown_ref[...].astype(BF16)
    @pl.when(pl.program_id(0) == 0)
    def _():
        mod_out[...] = bada_ref[...]

    cs = c_ref[...]
    cs = cs * jax.nn.sigmoid(cs)
    half_w = wada_a_ref.shape[1]
    mod_out[:, :half_w] += jnp.sum(wada_a_ref[...] * cs, axis=0, keepdims=True)
    mod_out[:, half_w:] += jnp.sum(wada_b_ref[...] * cs, axis=0, keepdims=True)

    k = proj(COL_K, FOX_WIDTH)
    cum = c3 + pltpu.roll(c3, LANES - FOX_HEADS, 1) + pltpu.roll(c3, LANES - 2 * FOX_HEADS, 1)
    cum = jnp.where(head_lane, cum, 0.0) + carry_ref[0:1, :]
    carry_ref[0:1, :] = cum[tm - 1:tm, :]
    c0_, c1_, c2_ = _split3(cum * LOG2_E)
    cum_pieces = c0_ + pltpu.roll(c1_, FOX_HEADS, 1) + pltpu.roll(c2_, 2 * FOX_HEADS, 1)
    cum_pieces = jnp.where(lane == ONE_LANE, 1.0, cum_pieces)
    kb = jnp.dot(cum_pieces.astype(BF16), ek_ref[...], preferred_element_type=F32)
    qbt = jnp.dot(eqt_ref[...], cum_pieces.T.astype(BF16), preferred_element_type=F32)
    even_own = (lax.broadcasted_iota(jnp.int32, (1, FOX_WIDTH), 1) % LANES) < FOX_HEAD_DIM
    even_own_t = (lax.broadcasted_iota(jnp.int32, (FOX_WIDTH, 1), 0) % LANES) < FOX_HEAD_DIM
    k0_out[...] = jnp.where(even_own, k, kb).astype(BF16)
    k1_out[...] = jnp.where(even_own, kb, k).astype(BF16)

    state = {}
    for hh in range(RET_HEADS):
        r_state = r_ref[hh]
        for ci in range(n_chunks):
            state[hh, ci] = r_state.astype(BF16)
            r_state = r_state * gc_ref[hh] + kv[hh, ci]
        r_ref[hh] = r_state

    for hh, ci in tiles:
        v = rv[chunk_rows(ci), head_cols(hh)]
        o = (jnp.dot(intra[hh, ci], v, preferred_element_type=F32)
             + jnp.dot(rq[hh][chunk_rows(ci)], state[hh, ci], preferred_element_type=F32) * xi_ref[hh])
        mu = jnp.mean(o, axis=-1, keepdims=True)
        d = o - mu
        var = jnp.mean(d * d, axis=-1, keepdims=True)
        on = d * lax.rsqrt(var + GN_EPS)
        g = rg[chunk_rows(ci), head_cols(hh)]
        ret_out[chunk_rows(ci), head_cols(hh)] = (g * jax.nn.sigmoid(g) * on).astype(BF16)

    tp = proj_t(ROW_QT, 2 * FOX_WIDTH)
    qt = tp[:FOX_WIDTH] * (FOX_HEAD_DIM ** -0.5 * LOG2_E)
    qt0_out[...] = jnp.where(even_own_t, qt, qbt).astype(BF16)
    qt1_out[...] = jnp.where(even_own_t, qbt, qt).astype(BF16)
    vt_out[...] = tp[FOX_WIDTH:].astype(BF16)


def _in_proj(x, sc1, sh1, w_all, w_t, bf_row, rot_a, rot_b, rot_at, rot_bt, eq_t, e_k,
             inner, xi_b, zeta_b, gc_b, w_out, w_up, w_down, c_col, w_ada, b_late):
    s, d = x.shape
    tm = TM_PROJ
    n = s // tm
    row = lambda w: pl.BlockSpec((tm, w), lambda i: (i, 0))
    col = lambda h: pl.BlockSpec((h, tm), lambda i: (0, i))
    tab = _const_spec((RET_HEADS, RET_CHUNK, RET_CHUNK))
    fox_rows, ret_rows, up_rows, down_rows = FOX_WIDTH // n, RET_WIDTH // n, d // n, 2 * D_FF // n
    assert fox_rows % 16 == 0 and ret_rows % 16 == 0 and up_rows % 16 == 0 and down_rows % 16 == 0
    assert n % 2 == 0 and FOX_WIDTH % fox_rows == 0
    cast_specs = [pl.BlockSpec((fox_rows, d), lambda i: (i, 0)),
                  pl.BlockSpec((ret_rows, d), lambda i: (FOX_WIDTH // ret_rows + i, 0)),
                  pl.BlockSpec((up_rows, 2 * D_FF), lambda i: (i, 0)),
                  pl.BlockSpec((down_rows, d), lambda i: (i // 2, 0)),
                  pl.BlockSpec((up_rows, 1), lambda i: (i, 0)),
                  pl.BlockSpec((up_rows, 2 * d), lambda i: (i, 1)),
                  pl.BlockSpec((up_rows, 2 * d), lambda i: (i, 2)),
                  _const_spec((1, 4 * d))]
    cast_out_specs = [pl.BlockSpec((fox_rows, d), lambda i: (i, 0)),
                      pl.BlockSpec((ret_rows, d), lambda i: (i, 0)),
                      pl.BlockSpec((up_rows, 2 * D_FF), lambda i: (i, 0)),
                      pl.BlockSpec((down_rows, d), lambda i: (i // 2, 0)),
                      pl.BlockSpec((1, 4 * d), lambda i: (0, 0))]
    out_shapes = ([jax.ShapeDtypeStruct((FOX_WIDTH, s), BF16)] * 2
                  + [jax.ShapeDtypeStruct((s, FOX_WIDTH), BF16)] * 2
                  + [jax.ShapeDtypeStruct((FOX_WIDTH, s), BF16),
                     jax.ShapeDtypeStruct((s, RET_WIDTH), BF16),
                     jax.ShapeDtypeStruct((FOX_WIDTH, d), BF16),
                     jax.ShapeDtypeStruct((RET_WIDTH, d), BF16),
                     jax.ShapeDtypeStruct((d, 2 * D_FF), BF16),
                     jax.ShapeDtypeStruct((D_FF, d), BF16),
                     jax.ShapeDtypeStruct((1, 4 * d), F32)])
    return pl.pallas_call(
        _in_proj_kernel,
        grid=(s // tm,),
        in_specs=[row(d), _const_spec((1, d)), _const_spec((1, d)),
                  _const_spec((d, N_COLS)), _const_spec((N_ROWS_T, d)), _const_spec((1, LANES)),
                  pl.BlockSpec((1, 2, RET_HEAD_DIM), lambda i: (i, 0, 0)),
                  _const_spec((4, tm, RET_HEAD_DIM)),
                  pl.BlockSpec((1, RET_HEAD_DIM, 2), lambda i: (i, 0, 0)),
                  _const_spec((4, RET_HEAD_DIM, tm)),
                  _const_spec((FOX_WIDTH, LANES)), _const_spec((LANES, FOX_WIDTH)),
                  tab, tab, tab, _const_spec((RET_HEADS, 1, RET_HEAD_DIM))] + cast_specs,
        out_specs=([col(FOX_WIDTH)] * 2 + [row(FOX_WIDTH)] * 2 + [col(FOX_WIDTH), row(RET_WIDTH)]
                   + cast_out_specs),
        out_shape=out_shapes,
        scratch_shapes=[pltpu.VMEM((SUBLANES, LANES), F32),
                        pltpu.VMEM((RET_HEADS, RET_HEAD_DIM, RET_HEAD_DIM), F32)],
        compiler_params=pltpu.CompilerParams(dimension_semantics=("arbitrary",),
                                             vmem_limit_bytes=VMEM_LIMIT),
        name="in_proj",
    )(x, sc1, sh1, w_all, w_t, bf_row, rot_a, rot_b, rot_at, rot_bt, eq_t, e_k,
      inner, xi_b, zeta_b, gc_b, w_out, w_out, w_up, w_down, c_col, w_ada, w_ada, b_late)


def _fox_kernel(qt0_ref, qt1_ref, qtn0_ref, qtn1_ref, k0_ref, k1_ref, vt_ref, o_ref, m_ref, acc_ref,
                sa_ref, ca_ref, sb_ref, cb_ref, sc_ref, cc_ref):
    i = pl.program_id(1)
    k_refs = (k0_ref, k1_ref)
    qt = [qt0_ref[...], qt1_ref[...]]
    qt_next = [qtn0_ref[...], qtn1_ref[...]]
    m_ref[...] = jnp.full_like(m_ref, NEG_BIG)
    acc_ref[...] = jnp.zeros_like(acc_ref)
    n_sub = TK // TK_SUB

    def qk(e, q_aug, start):
        return jnp.dot(k_refs[e][pl.ds(start, TK_SUB), :], q_aug, preferred_element_type=F32)

    def values(e, start, width):
        return jnp.concatenate(
            [vt_ref[e * FOX_HEAD_DIM:(e + 1) * FOX_HEAD_DIM, pl.ds(start, width)],
             jnp.ones((BF16_ROWS, width), BF16)], axis=0)

    def pv(e, st, m_new, start, width):
        pt = jnp.exp2(st - m_new).astype(BF16)
        return jnp.dot(values(e, start, width), pt, preferred_element_type=F32)

    def score_sub(e, q_aug, j, sub, s_ref, cmax):
        st = qk(e, q_aug, pl.multiple_of(j * TK + sub * TK_SUB, TK_SUB))
        s_ref[e, sub * TK_SUB:(sub + 1) * TK_SUB, :] = st
        cm = jnp.max(st, axis=0, keepdims=True)
        return cm if cmax is None else jnp.maximum(cmax, cm)

    def scores(j, e, s_ref, c_ref):
        cmax = None
        for sub in range(n_sub):
            cmax = score_sub(e, qt[e], j, sub, s_ref, cmax)
        c_ref[e] = cmax

    def stage(e, j_next, buf_next, j_cur, buf_cur):
        (s_next, c_next), (s_cur, c_cur) = buf_next, buf_cur
        m_prev = m_ref[e]
        m_new = jnp.maximum(m_prev, c_cur[e])
        acc = jnp.exp2(m_prev - m_new) * acc_ref[e]
        cmax = None
        for sub in range(n_sub):
            rows = slice(sub * TK_SUB, (sub + 1) * TK_SUB)
            cmax = score_sub(e, qt[e], j_next, sub, s_next, cmax)
            acc = acc + pv(e, s_cur[e, rows, :], m_new,
                           pl.multiple_of(j_cur * TK + sub * TK_SUB, TK_SUB), TK_SUB)
        c_next[e] = cmax
        acc_ref[e] = acc
        m_ref[e] = m_new

    def finish(j, s_ref, fill_next):
        n_pieces = TK // TQ
        tri = (lax.broadcasted_iota(jnp.int32, (TQ, TQ), 0)
               <= lax.broadcasted_iota(jnp.int32, (TQ, TQ), 1))
        for diag in range(n_pieces):
            @pl.when(i % n_pieces == diag)
            def _():
                for e in range(2):
                    cmax = None
                    for piece in range(n_pieces):
                        if fill_next:
                            for sub in range(piece * n_sub // n_pieces, (piece + 1) * n_sub // n_pieces):
                                cmax = score_sub(e, qt_next[e], 0, sub, sc_ref, cmax)
                        if piece > diag:
                            continue
                        st = s_ref[e, piece * TQ:(piece + 1) * TQ, :]
                        if piece == diag:
                            st = jnp.where(tri, st, NEG_BIG)
                        m_prev = m_ref[e]
                        m_new = jnp.maximum(m_prev, jnp.max(st, axis=0, keepdims=True))
                        start = pl.multiple_of(j * TK + piece * TQ, TQ)
                        acc_ref[e] = (jnp.exp2(m_prev - m_new) * acc_ref[e]
                                      + pv(e, st, m_new, start, TQ))
                        m_ref[e] = m_new
                    if fill_next:
                        cc_ref[e] = cmax

    buf_a = (sa_ref, ca_ref)
    buf_b = (sb_ref, cb_ref)
    buf_c = (sc_ref, cc_ref)

    nfull = (i * TQ) // TK
    prefilled = jnp.logical_and(i > 0, ((i - 1) * TQ) // TK >= 1)

    @pl.when(jnp.logical_not(prefilled))
    def _():
        for e in range(2):
            scores(0, e, *buf_c)

    @pl.when(nfull >= 1)
    def _():
        for e in range(2):
            stage(e, 1, buf_a, 0, buf_c)

    def pair(jj, carry):
        j = 2 * jj + 1
        for e in range(2):
            stage(e, j + 1, buf_b, j, buf_a)
        for e in range(2):
            stage(e, j + 2, buf_a, j + 1, buf_b)
        return carry

    def group(jg, carry):
        for u in range(PAIRS_PER_TRIP):
            carry = pair(PAIRS_PER_TRIP * jg + u, carry)
        return carry

    n_pairs = jnp.maximum(nfull - 1, 0) // 2
    n_groups = n_pairs // PAIRS_PER_TRIP
    lax.fori_loop(0, n_groups, group, 0)
    lax.fori_loop(PAIRS_PER_TRIP * n_groups, n_pairs, pair, 0)

    @pl.when(nfull == 0)
    def _():
        finish(nfull, sc_ref, False)

    @pl.when(nfull % 2 == 1)
    def _():
        finish(nfull, sa_ref, True)

    @pl.when(jnp.logical_and(nfull >= 2, nfull % 2 == 0))
    def _():
        for e in range(2):
            stage(e, nfull, buf_b, nfull - 1, buf_a)
        finish(nfull, sb_ref, True)

    outs = []
    for e in range(2):
        acc = acc_ref[e]
        outs.append(acc[:FOX_HEAD_DIM] / acc[FOX_HEAD_DIM:FOX_HEAD_DIM + 1])
    o_ref[...] = jnp.concatenate(outs, axis=0).T.astype(o_ref.dtype)


def _fox_attention(qt0, qt1, k0, k1, vt):
    s = k0.shape[0]
    assert TK % TQ == 0 and TK % TK_SUB == 0 and s % TK == 0
    n_q = s // TQ
    blk_t = pl.BlockSpec((LANES, TQ), lambda p, i: (p, i))
    nxt_t = pl.BlockSpec((LANES, TQ), lambda p, i: (p, jnp.minimum(i + 1, n_q - 1)))
    res = pl.BlockSpec((s, LANES), lambda p, i: (0, p))
    return pl.pallas_call(
        _fox_kernel,
        grid=(FOX_PAIRS, n_q),
        in_specs=[blk_t, blk_t, nxt_t, nxt_t, res, res, pl.BlockSpec((LANES, s), lambda p, i: (p, 0))],
        out_specs=pl.BlockSpec((TQ, LANES), lambda p, i: (i, p)),
        out_shape=jax.ShapeDtypeStruct((s, FOX_WIDTH), BF16),
        scratch_shapes=[pltpu.VMEM((2, 1, TQ), F32), pltpu.VMEM((2, PV_ROWS, TQ), F32),
                        pltpu.VMEM((2, TK, TQ), F32), pltpu.VMEM((2, 1, TQ), F32),
                        pltpu.VMEM((2, TK, TQ), F32), pltpu.VMEM((2, 1, TQ), F32),
                        pltpu.VMEM((2, TK, TQ), F32), pltpu.VMEM((2, 1, TQ), F32)],
        compiler_params=pltpu.CompilerParams(dimension_semantics=("arbitrary", "arbitrary"),
                                             vmem_limit_bytes=VMEM_LIMIT),
        name="fox_attention",
    )(qt0, qt1, qt0, qt1, k0, k1, vt)


def _mix_ffn_kernel(fox_ref, ret_ref, x_ref, fox0_ref, ret0_ref, x0_ref,
                    wf_ref, wr_ref, g1_ref, l1g_ref, l1b_ref,
                    sc_ref, sh_ref, g2_ref, wu_ref, cw_ref, cb_ref, wd_ref, l2g_ref, l2b_ref,
                    o_ref, carry_ref, slab_in_ref, slab_out_ref, y_ref, x1_ref):
    tm = x_ref.shape[0]
    grp = tm // SUBLANES
    pitch = grp + SLAB_PAD_ROWS
    n_slab = D_MODEL // LANES

    def front(fox, ret, x):
        mix = (jnp.dot(fox, wf_ref[...], preferred_element_type=F32)
               + jnp.dot(ret, wr_ref[...], preferred_element_type=F32))
        x1_nat = _layer_norm(ALPHA * x + g1_ref[...] * mix, l1g_ref[...], l1b_ref[...])
        for c in range(n_slab):
            for s in range(SUBLANES):
                slab_in_ref[c, s * pitch:s * pitch + grp, :] = x1_nat[s * grp:(s + 1) * grp,
                                                                      c * LANES:(c + 1) * LANES]
        return jnp.concatenate(
            [jnp.concatenate([slab_in_ref[c, pl.ds(v, SUBLANES, stride=pitch), :] for v in range(grp)],
                             axis=0)
             for c in range(n_slab)], axis=1)

    @pl.when(pl.program_id(0) == 0)
    def _():
        carry_ref[...] = jnp.zeros_like(carry_ref)
        x1_ref[...] = front(fox0_ref[...], ret0_ref[...], x0_ref[...])

    h = (x1_ref[...] * (1.0 + sc_ref[...]) + sh_ref[...]).astype(BF16)
    first_sublane = lax.broadcasted_iota(jnp.int32, (SUBLANES, FF_CHUNK), 0) == 0

    def conv_up(c0):
        cols = slice(c0, c0 + FF_CHUNK)
        up = jnp.dot(h, wu_ref[:, cols], preferred_element_type=F32)
        prev = carry_ref[:, cols]
        tail = []
        for g in range(2):
            rows = slice(tm - (2 - g) * SUBLANES, tm - (1 - g) * SUBLANES)
            tail.append(jnp.where(first_sublane,
                                  pltpu.roll(prev[g * SUBLANES:(g + 1) * SUBLANES], 1, 0),
                                  pltpu.roll(up[rows], 1, 0)))
        carry_ref[:, cols] = up[tm - 2 * SUBLANES:tm]
        back1 = jnp.concatenate([tail[1], up[:tm - SUBLANES]], axis=0)
        back2 = jnp.concatenate([tail[0], tail[1], up[:tm - 2 * SUBLANES]], axis=0)
        cw = cw_ref[:, cols] * RSQRT2
        return cb_ref[:, cols] * RSQRT2 + back2 * cw[0:1] + back1 * cw[1:2] + up * cw[2:3]

    for ci in range(D_FF // FF_CHUNK):
        a = conv_up(ci * FF_CHUNK)
        b = conv_up(D_FF + ci * FF_CHUNK)
        y_ref[:, ci * FF_CHUNK:(ci + 1) * FF_CHUNK] = (a * (1.0 + lax.erf(a)) * b).astype(BF16)

    x1_next = front(fox_ref[...], ret_ref[...], x_ref[...])

    ffn = jnp.dot(y_ref[...], wd_ref[...], preferred_element_type=F32)
    out = _layer_norm(ALPHA * x1_ref[...] + g2_ref[...] * ffn, l2g_ref[...], l2b_ref[...])
    x1_ref[...] = x1_next

    for c in range(n_slab):
        for v in range(grp):
            slab_out_ref[c, pl.ds(v, SUBLANES, stride=pitch), :] = out[v * SUBLANES:(v + 1) * SUBLANES,
                                                                       c * LANES:(c + 1) * LANES]
    for c in range(n_slab):
        for s in range(SUBLANES):
            o_ref[s * grp:(s + 1) * grp, c * LANES:(c + 1) * LANES] = slab_out_ref[c, s * pitch:s * pitch + grp, :]


def _mix_ffn(fox, ret, x, w_fox, w_ret, g1, ln1_g, ln1_b, sc2, sh2, g2, w_up, conv_w, conv_b, w_down,
             ln2_g, ln2_b):
    s, d = x.shape
    tm = TM_FFN
    n_tiles = s // tm
    nxt = lambda w: pl.BlockSpec((tm, w), lambda i: (jnp.minimum(i + 1, n_tiles - 1), 0))
    vec = _const_spec((1, d))
    slab = pltpu.VMEM((d // LANES, tm + SUBLANES * SLAB_PAD_ROWS, LANES), F32)
    return pl.pallas_call(
        _mix_ffn_kernel,
        grid=(n_tiles,),
        in_specs=[nxt(FOX_WIDTH), nxt(RET_WIDTH), nxt(d),
                  _const_spec((tm, FOX_WIDTH)), _const_spec((tm, RET_WIDTH)), _const_spec((tm, d)),
                  _const_spec((FOX_WIDTH, d)), _const_spec((RET_WIDTH, d)), vec, vec, vec,
                  vec, vec, vec,
                  _const_spec((d, 2 * D_FF)), _const_spec((CONV_WIDTH, 2 * D_FF)),
                  _const_spec((1, 2 * D_FF)), _const_spec((D_FF, d)), vec, vec],
        out_specs=pl.BlockSpec((tm, d), lambda i: (i, 0)),
        out_shape=jax.ShapeDtypeStruct((s, d), F32),
        scratch_shapes=[pltpu.VMEM((2 * SUBLANES, 2 * D_FF), F32), slab, slab,
                        pltpu.VMEM((tm, D_FF), BF16), pltpu.VMEM((tm, d), F32)],
        compiler_params=pltpu.CompilerParams(dimension_semantics=("arbitrary",),
                                             vmem_limit_bytes=VMEM_LIMIT),
        name="mix_ffn",
    )(fox, ret, x, fox, ret, x, w_fox, w_ret, g1, ln1_g, ln1_b, sc2, sh2, g2, w_up, conv_w, conv_b,
      w_down, ln2_g, ln2_b)


def _placement_tables():
    e_q = np.zeros((LANES, FOX_WIDTH), np.float32)
    e_k = np.zeros((LANES, FOX_WIDTH), np.float32)
    for h in range(FOX_HEADS):
        base = (h // 2) * LANES + (FOX_HEAD_DIM if h % 2 == 0 else 0)
        for j in range(N_PIECES):
            e_q[j * FOX_HEADS + h, base + BIAS_CUM + j] = 1.0
            e_q[ONE_LANE, base + BIAS_ONE + j] = 1.0
            e_k[ONE_LANE, base + BIAS_CUM + j] = 1.0
            e_k[j * FOX_HEADS + h, base + BIAS_ONE + j] = -1.0
    return jnp.asarray(e_q.T, BF16), jnp.asarray(e_k, BF16)


def _rotation_tables(s, tm):
    dk = RET_HEAD_DIM
    inv_freq = ROPE_BASE ** (-np.arange(0, dk, 2, dtype=np.float64) / dk)
    inv_freq = np.concatenate([inv_freq, inv_freq])
    sign = np.concatenate([-np.ones(dk // 2), np.ones(dk // 2)])
    ang_a = (np.arange(s // tm, dtype=np.float64) * tm)[:, None] * inv_freq[None, :]
    ang_b = np.arange(tm, dtype=np.float64)[:, None] * inv_freq[None, :]
    rot_a = np.stack([np.cos(ang_a), np.sin(ang_a)], axis=1)
    rot_b = np.stack([np.cos(ang_b), np.sin(ang_b), sign * np.cos(ang_b), sign * np.sin(ang_b)])
    f32 = lambda a: jnp.asarray(a, F32)
    return f32(rot_a), f32(rot_b), f32(rot_a.transpose(0, 2, 1)), f32(rot_b.transpose(0, 2, 1))


def _decay_tables():
    c = RET_CHUNK
    log_gamma = np.log1p(-np.exp2(-5.0 - np.arange(RET_HEADS, dtype=np.float64)))
    idx = np.arange(c, dtype=np.float64)
    diff = idx[:, None] - idx[None, :]
    inner = np.where(diff[None] >= 0,
                     np.exp(np.maximum(diff, 0.0)[None] * log_gamma[:, None, None]), 0.0)
    xi = np.exp((idx[None, :] + 1.0) * log_gamma[:, None])
    zeta = np.exp((c - 1.0 - idx[None, :]) * log_gamma[:, None])
    g_chunk = np.exp(c * log_gamma)
    bshape = (RET_HEADS, c, RET_HEAD_DIM)
    f32 = lambda a: jnp.asarray(a, F32)
    return (f32(inner),
            f32(np.broadcast_to(xi[:, :, None], bshape)),
            f32(np.broadcast_to(zeta[:, None, :], bshape)),
            f32(np.broadcast_to(g_chunk[:, None, None], (RET_HEADS, 1, RET_HEAD_DIM))))


def kernel(x, c, w_ada, b_ada, w_in, b_f, w_out, ln1_g, ln1_b, w_up, conv_w, conv_b, w_down, ln2_g, ln2_b):
    b, s, d = x.shape
    assert (b, s, d) == (1, SEQ, D_MODEL) and w_ada.shape[0] == DEPTH
    xs = x[0]

    wi = w_in[0]
    o = 0
    parts = {}
    for name, width in (("fq", FOX_WIDTH), ("fk", FOX_WIDTH), ("fv", FOX_WIDTH), ("ff", FOX_HEADS),
                        ("rq", RET_WIDTH), ("rk", RET_WIDTH), ("rv", RET_WIDTH), ("rg", RET_WIDTH)):
        parts[name] = wi[:, o:o + width]
        o += width
    w_all = jnp.concatenate(
        [parts["fk"], jnp.pad(parts["ff"], ((0, 0), (0, LANES - FOX_HEADS))),
         parts["rq"], parts["rv"], parts["rg"]], axis=1).astype(BF16)
    w_t = jnp.concatenate([parts["fq"], parts["fv"], parts["rk"]], axis=1).T.astype(BF16)
    bf_row = jnp.pad(b_f[0][None, :], ((0, 0), (0, LANES - FOX_HEADS)))

    eq_t, e_k = _placement_tables()
    rot_a, rot_b, rot_at, rot_bt = _rotation_tables(s, TM_PROJ)
    inner, xi_b, zeta_b, gc_b = _decay_tables()

    c_col = c.reshape(d, 1)
    mod = _adaln_mod(c_col, w_ada[0], b_ada[0][None, :], 2 * d)
    sh1, sc1 = mod[:, :d], mod[:, d:]

    qt0, qt1, k0, k1, vt, ret, w_fox, w_ret, w_up_b, w_down_b, mod_late = _in_proj(
        xs, sc1, sh1, w_all, w_t, bf_row, rot_a, rot_b, rot_at, rot_bt, eq_t, e_k,
        inner, xi_b, zeta_b, gc_b, w_out[0], w_up[0], w_down[0],
        c_col, w_ada[0], b_ada[0][None, 2 * d:])
    g1, sh2, sc2, g2 = [mod_late[:, i * d:(i + 1) * d] for i in range(4)]
    fox = _fox_attention(qt0, qt1, k0, k1, vt)
    out = _mix_ffn(fox, ret, xs, w_fox, w_ret, g1, ln1_g[0][None, :], ln1_b[0][None, :],
                   sc2, sh2, g2, w_up_b, conv_w[0], conv_b[0][None, :],
                   w_down_b, ln2_g[0][None, :], ln2_b[0][None, :])
    return out[None]
```

```python
import math

import jax
import jax.numpy as jnp
import numpy as np
from jax import lax
from jax.experimental import pallas as pl
from jax.experimental.pallas import tpu as pltpu

F32 = jnp.float32
BF16 = jnp.bfloat16

D_MODEL = 1024
SEQ = 16384
FOX_HEADS = 8
FOX_HEAD_DIM = 64
FOX_WIDTH = FOX_HEADS * FOX_HEAD_DIM
RET_HEADS = 4
RET_HEAD_DIM = 128
RET_WIDTH = RET_HEADS * RET_HEAD_DIM
D_FF = 2816
CONV_WIDTH = 3
RET_CHUNK = 128
ROPE_BASE = 10000.0
LN_EPS = 1e-5
GN_EPS = 1e-6
DEPTH = 1
ALPHA = (2 * DEPTH) ** 0.25

LANES = 128
SUBLANES = 8
BF16_ROWS = 16
VMEM_LIMIT = 56 * 1024 * 1024

FOX_PAIRS = FOX_HEADS // 2
BIAS_CUM = 0
BIAS_ONE = 3
N_PIECES = 3
ONE_LANE = N_PIECES * FOX_HEADS
PV_ROWS = FOX_HEAD_DIM + BF16_ROWS
NEG_BIG = -1e30
LOG2_E = math.log2(math.e)
RSQRT2 = 2.0 ** -0.5

COL_K = 0
COL_F = COL_K + FOX_WIDTH
COL_RQ = COL_F + LANES
COL_RV = COL_RQ + RET_WIDTH
COL_RG = COL_RV + RET_WIDTH
N_COLS = COL_RG + RET_WIDTH
ROW_QT = 0
ROW_VT = ROW_QT + FOX_WIDTH
ROW_RKT = ROW_VT + FOX_WIDTH
N_ROWS_T = ROW_RKT + RET_WIDTH

TM_PROJ = 1024
TQ = 512
TK = 1024
TK_SUB = 256
PAIRS_PER_TRIP = 2
TM_FFN = 512
FF_CHUNK = 256
SLAB_PAD_ROWS = 4


def _const_spec(shape):
    return pl.BlockSpec(shape, lambda *_: (0,) * len(shape), pipeline_mode=pl.Buffered(1))


def _split3(x):
    p0 = x.astype(BF16).astype(F32)
    r1 = x - p0
    p1 = r1.astype(BF16).astype(F32)
    p2 = (r1 - p1).astype(BF16).astype(F32)
    return p0, p1, p2


def _layer_norm(y, g, b):
    mu = jnp.mean(y, axis=-1, keepdims=True)
    d = y - mu
    var = jnp.mean(d * d, axis=-1, keepdims=True)
    return d * lax.rsqrt(var + LN_EPS) * g + b


def _mod_kernel(c_ref, w_ref, b_ref, o_ref):
    c = c_ref[...]
    sc = c * jax.nn.sigmoid(c)
    o_ref[...] = jnp.sum(w_ref[...] * sc, axis=0, keepdims=True) + b_ref[...]


def _adaln_mod(c_col, w_ada, b_ada, n):
    d = w_ada.shape[0]
    tn = 512
    return pl.pallas_call(
        _mod_kernel,
        grid=(n // tn,),
        in_specs=[pl.BlockSpec((d, 1), lambda j: (0, 0)),
                  pl.BlockSpec((d, tn), lambda j: (0, j)),
                  pl.BlockSpec((1, tn), lambda j: (0, j))],
        out_specs=pl.BlockSpec((1, tn), lambda j: (0, j)),
        out_shape=jax.ShapeDtypeStruct((1, n), F32),
        compiler_params=pltpu.CompilerParams(dimension_semantics=("arbitrary",)),
        name="adaln_mod",
    )(c_col, w_ada, b_ada)


def _in_proj_kernel(x_ref, sc_ref, sh_ref, w_ref, wt_ref, bf_ref, ra_ref, rb_ref, rat_ref, rbt_ref,
                    eqt_ref, ek_ref, inner_ref, xi_ref, zeta_ref, gc_ref,
                    wfox_ref, wret_ref, wup_ref, wdown_ref, c_ref, wada_a_ref, wada_b_ref, bada_ref,
                    qt0_out, qt1_out, k0_out, k1_out, vt_out, ret_out,
                    wfox_out, wret_out, wup_out, wdown_out, mod_out,
                    carry_ref, r_ref):
    tm = x_ref.shape[0]

    @pl.when(pl.program_id(0) == 0)
    def _():
        carry_ref[...] = jnp.zeros_like(carry_ref)
        r_ref[...] = jnp.zeros_like(r_ref)

    h = (x_ref[...] * (1.0 + sc_ref[...]) + sh_ref[...]).astype(BF16)

    def proj(c0, width):
        return jnp.dot(h, w_ref[:, c0:c0 + width], preferred_element_type=F32)

    def proj_t(r0, height):
        return lax.dot_general(wt_ref[r0:r0 + height, :], h, (((1,), (1,)), ((), ())),
                               preferred_element_type=F32)

    lane = lax.broadcasted_iota(jnp.int32, (tm, LANES), 1)
    head_lane = lane < FOX_HEADS
    ff = proj(COL_F, LANES) + bf_ref[...]
    two = 2 * RET_HEAD_DIM
    rq_raw = [proj(COL_RQ + pair * two, two) for pair in range(RET_HEADS // 2)]
    rv = proj(COL_RV, RET_WIDTH).astype(BF16)
    rg = proj(COL_RG, RET_WIDTH)
    rkt_raw = proj_t(ROW_RKT, RET_WIDTH)

    logf = jnp.minimum(ff, 0.0) - jnp.log1p(jnp.exp(-jnp.abs(ff)))
    logf = jnp.where(head_lane, logf, 0.0)
    p0, p1, p2 = _split3(logf)
    pieces = p0 + pltpu.roll(p1, FOX_HEADS, 1) + pltpu.roll(p2, 2 * FOX_HEADS, 1)
    row = lax.broadcasted_iota(jnp.int32, (tm, tm), 0)
    col = lax.broadcasted_iota(jnp.int32, (tm, tm), 1)
    tril = jnp.where(row >= col, 1.0, 0.0).astype(BF16)
    c3 = jnp.dot(tril, pieces.astype(BF16), preferred_element_type=F32)

    half = RET_HEAD_DIM // 2
    ca, sa = ra_ref[0, 0:1, :], ra_ref[0, 1:2, :]
    cos = ca * rb_ref[0] - sa * rb_ref[1]
    sin = sa * rb_ref[2] + ca * rb_ref[3]
    cat, sat = rat_ref[0, :, 0:1], rat_ref[0, :, 1:2]
    cost = cat * rbt_ref[0] - sat * rbt_ref[1]
    sint = sat * rbt_ref[2] + cat * rbt_ref[3]
    kscale = RET_HEAD_DIM ** -0.5
    rq, rkt = [], []
    for a2 in rq_raw:
        for e in range(2):
            a = a2[:, e * RET_HEAD_DIM:(e + 1) * RET_HEAD_DIM]
            rq.append((a * cos + pltpu.roll(a, half, 1) * sin).astype(BF16))
    for hh in range(RET_HEADS):
        b = rkt_raw[hh * RET_HEAD_DIM:(hh + 1) * RET_HEAD_DIM]
        b_rot = jnp.concatenate([b[half:], b[:half]], axis=0)
        rkt.append(((b * cost + b_rot * sint) * kscale).astype(BF16))

    c = RET_CHUNK
    n_chunks = tm // c
    tiles = [(hh, ci) for hh in range(RET_HEADS) for ci in range(n_chunks)]

    def head_cols(hh):
        return slice(hh * RET_HEAD_DIM, (hh + 1) * RET_HEAD_DIM)

    def chunk_rows(ci):
        return slice(ci * c, (ci + 1) * c)

    intra, kv = {}, {}
    for hh, ci in tiles:
        kt = rkt[hh][:, chunk_rows(ci)]
        v = rv[chunk_rows(ci), head_cols(hh)]
        intra[hh, ci] = (jnp.dot(rq[hh][chunk_rows(ci)], kt, preferred_element_type=F32)
                         * inner_ref[hh]).astype(BF16)
        kzt = (kt.astype(F32) * zeta_ref[hh]).astype(BF16)
        kv[hh, ci] = jnp.dot(kzt, v, preferred_element_type=F32)

    wfox_out[...] = wfox_ref[...].astype(BF16)
    wret_out[...] = wret_ref[...].astype(BF16)
    wup_out[...] = wup_ref[...].astype(BF16)
    wdown_out[...] = wdown_ref[...].astype(BF16)
    @pl.when(pl.program_id(0) == 0)
    def _():
        mod_out[...] = bada_ref[...]

    cs = c_ref[...]
    cs = cs * jax.nn.sigmoid(cs)
    half_w = wada_a_ref.shape[1]
    mod_out[:, :half_w] += jnp.sum(wada_a_ref[...] * cs, axis=0, keepdims=True)
    mod_out[:, half_w:] += jnp.sum(wada_b_ref[...] * cs, axis=0, keepdims=True)

    k = proj(COL_K, FOX_WIDTH)
    cum = c3 + pltpu.roll(c3, LANES - FOX_HEADS, 1) + pltpu.roll(c3, LANES - 2 * FOX_HEADS, 1)
    cum = jnp.where(head_lane, cum, 0.0) + carry_ref[0:1, :]
    carry_ref[0:1, :] = cum[tm - 1:tm, :]
    c0_, c1_, c2_ = _split3(cum * LOG2_E)
    cum_pieces = c0_ + pltpu.roll(c1_, FOX_HEADS, 1) + pltpu.roll(c2_, 2 * FOX_HEADS, 1)
    cum_pieces = jnp.where(lane == ONE_LANE, 1.0, cum_pieces)
    kb = jnp.dot(cum_pieces.astype(BF16), ek_ref[...], preferred_element_type=F32)
    qbt = jnp.dot(eqt_ref[...], cum_pieces.T.astype(BF16), preferred_element_type=F32)
    even_own = (lax.broadcasted_iota(jnp.int32, (1, FOX_WIDTH), 1) % LANES) < FOX_HEAD_DIM
    even_own_t = (lax.broadcasted_iota(jnp.int32, (FOX_WIDTH, 1), 0) % LANES) < FOX_HEAD_DIM
    k0_out[...] = jnp.where(even_own, k, kb).astype(BF16)
    k1_out[...] = jnp.where(even_own, kb, k).astype(BF16)

    state = {}
    for hh in range(RET_HEADS):
        r_state = r_ref[hh]
        for ci in range(n_chunks):
            state[hh, ci] = r_state.astype(BF16)
            r_state = r_state * gc_ref[hh] + kv[hh, ci]
        r_ref[hh] = r_state

    for hh, ci in tiles:
        v = rv[chunk_rows(ci), head_cols(hh)]
        o = (jnp.dot(intra[hh, ci], v, preferred_element_type=F32)
             + jnp.dot(rq[hh][chunk_rows(ci)], state[hh, ci], preferred_element_type=F32) * xi_ref[hh])
        mu = jnp.mean(o, axis=-1, keepdims=True)
        d = o - mu
        var = jnp.mean(d * d, axis=-1, keepdims=True)
        on = d * lax.rsqrt(var + GN_EPS)
        g = rg[chunk_rows(ci), head_cols(hh)]
        ret_out[chunk_rows(ci), head_cols(hh)] = (g * jax.nn.sigmoid(g) * on).astype(BF16)

    tp = proj_t(ROW_QT, 2 * FOX_WIDTH)
    qt = tp[:FOX_WIDTH] * (FOX_HEAD_DIM ** -0.5 * LOG2_E)
    qt0_out[...] = jnp.where(even_own_t, qt, qbt).astype(BF16)
    qt1_out[...] = jnp.where(even_own_t, qbt, qt).astype(BF16)
    vt_out[...] = tp[FOX_WIDTH:].astype(BF16)


def _in_proj(x, sc1, sh1, w_all, w_t, bf_row, rot_a, rot_b, rot_at, rot_bt, eq_t, e_k,
             inner, xi_b, zeta_b, gc_b, w_out, w_up, w_down, c_col, w_ada, b_late):
    s, d = x.shape
    tm = TM_PROJ
    n = s // tm
    row = lambda w: pl.BlockSpec((tm, w), lambda i: (i, 0))
    col = lambda h: pl.BlockSpec((h, tm), lambda i: (0, i))
    tab = _const_spec((RET_HEADS, RET_CHUNK, RET_CHUNK))
    fox_rows, ret_rows, up_rows, down_rows = FOX_WIDTH // n, RET_WIDTH // n, d // n, 2 * D_FF // n
    assert fox_rows % 16 == 0 and ret_rows % 16 == 0 and up_rows % 16 == 0 and down_rows % 16 == 0
    assert n % 2 == 0 and FOX_WIDTH % fox_rows == 0
    cast_specs = [pl.BlockSpec((fox_rows, d), lambda i: (i, 0)),
                  pl.BlockSpec((ret_rows, d), lambda i: (FOX_WIDTH // ret_rows + i, 0)),
                  pl.BlockSpec((up_rows, 2 * D_FF), lambda i: (i, 0)),
                  pl.BlockSpec((down_rows, d), lambda i: (i // 2, 0)),
                  pl.BlockSpec((up_rows, 1), lambda i: (i, 0)),
                  pl.BlockSpec((up_rows, 2 * d), lambda i: (i, 1)),
                  pl.BlockSpec((up_rows, 2 * d), lambda i: (i, 2)),
                  _const_spec((1, 4 * d))]
    cast_out_specs = [pl.BlockSpec((fox_rows, d), lambda i: (i, 0)),
                      pl.BlockSpec((ret_rows, d), lambda i: (i, 0)),
                      pl.BlockSpec((up_rows, 2 * D_FF), lambda i: (i, 0)),
                      pl.BlockSpec((down_rows, d), lambda i: (i // 2, 0)),
                      pl.BlockSpec((1, 4 * d), lambda i: (0, 0))]
    out_shapes = ([jax.ShapeDtypeStruct((FOX_WIDTH, s), BF16)] * 2
                  + [jax.ShapeDtypeStruct((s, FOX_WIDTH), BF16)] * 2
                  + [jax.ShapeDtypeStruct((FOX_WIDTH, s), BF16),
                     jax.ShapeDtypeStruct((s, RET_WIDTH), BF16),
                     jax.ShapeDtypeStruct((FOX_WIDTH, d), BF16),
                     jax.ShapeDtypeStruct((RET_WIDTH, d), BF16),
                     jax.ShapeDtypeStruct((d, 2 * D_FF), BF16),
                     jax.ShapeDtypeStruct((D_FF, d), BF16),
                     jax.ShapeDtypeStruct((1, 4 * d), F32)])
    return pl.pallas_call(
        _in_proj_kernel,
        grid=(s // tm,),
        in_specs=[row(d), _const_spec((1, d)), _const_spec((1, d)),
                  _const_spec((d, N_COLS)), _const_spec((N_ROWS_T, d)), _const_spec((1, LANES)),
                  pl.BlockSpec((1, 2, RET_HEAD_DIM), lambda i: (i, 0, 0)),
                  _const_spec((4, tm, RET_HEAD_DIM)),
                  pl.BlockSpec((1, RET_HEAD_DIM, 2), lambda i: (i, 0, 0)),
                  _const_spec((4, RET_HEAD_DIM, tm)),
                  _const_spec((FOX_WIDTH, LANES)), _const_spec((LANES, FOX_WIDTH)),
                  tab, tab, tab, _const_spec((RET_HEADS, 1, RET_HEAD_DIM))] + cast_specs,
        out_specs=([col(FOX_WIDTH)] * 2 + [row(FOX_WIDTH)] * 2 + [col(FOX_WIDTH), row(RET_WIDTH)]
                   + cast_out_specs),
        out_shape=out_shapes,
        scratch_shapes=[pltpu.VMEM((SUBLANES, LANES), F32),
                        pltpu.VMEM((RET_HEADS, RET_HEAD_DIM, RET_HEAD_DIM), F32)],
        compiler_params=pltpu.CompilerParams(dimension_semantics=("arbitrary",),
                                             vmem_limit_bytes=VMEM_LIMIT),
        name="in_proj",
    )(x, sc1, sh1, w_all, w_t, bf_row, rot_a, rot_b, rot_at, rot_bt, eq_t, e_k,
      inner, xi_b, zeta_b, gc_b, w_out, w_out, w_up, w_down, c_col, w_ada, w_ada, b_late)


def _fox_kernel(qt0_ref, qt1_ref, qtn0_ref, qtn1_ref, k0_ref, k1_ref, vt_ref, o_ref, m_ref, acc_ref,
                sa_ref, ca_ref, sb_ref, cb_ref, sc_ref, cc_ref):
    i = pl.program_id(1)
    k_refs = (k0_ref, k1_ref)
    qt = [qt0_ref[...], qt1_ref[...]]
    qt_next = [qtn0_ref[...], qtn1_ref[...]]
    m_ref[...] = jnp.full_like(m_ref, NEG_BIG)
    acc_ref[...] = jnp.zeros_like(acc_ref)
    n_sub = TK // TK_SUB

    def qk(e, q_aug, start):
        return jnp.dot(k_refs[e][pl.ds(start, TK_SUB), :], q_aug, preferred_element_type=F32)

    def values(e, start, width):
        return jnp.concatenate(
            [vt_ref[e * FOX_HEAD_DIM:(e + 1) * FOX_HEAD_DIM, pl.ds(start, width)],
             jnp.ones((BF16_ROWS, width), BF16)], axis=0)

    def pv(e, st, m_new, start, width):
        pt = jnp.exp2(st - m_new).astype(BF16)
        return jnp.dot(values(e, start, width), pt, preferred_element_type=F32)

    def score_sub(e, q_aug, j, sub, s_ref, cmax):
        st = qk(e, q_aug, pl.multiple_of(j * TK + sub * TK_SUB, TK_SUB))
        s_ref[e, sub * TK_SUB:(sub + 1) * TK_SUB, :] = st
        cm = jnp.max(st, axis=0, keepdims=True)
        return cm if cmax is None else jnp.maximum(cmax, cm)

    def scores(j, e, s_ref, c_ref):
        cmax = None
        for sub in range(n_sub):
            cmax = score_sub(e, qt[e], j, sub, s_ref, cmax)
        c_ref[e] = cmax

    def stage(e, j_next, buf_next, j_cur, buf_cur):
        (s_next, c_next), (s_cur, c_cur) = buf_next, buf_cur
        m_prev = m_ref[e]
        m_new = jnp.maximum(m_prev, c_cur[e])
        acc = jnp.exp2(m_prev - m_new) * acc_ref[e]
        cmax = None
        for sub in range(n_sub):
            rows = slice(sub * TK_SUB, (sub + 1) * TK_SUB)
            cmax = score_sub(e, qt[e], j_next, sub, s_next, cmax)
            acc = acc + pv(e, s_cur[e, rows, :], m_new,
                           pl.multiple_of(j_cur * TK + sub * TK_SUB, TK_SUB), TK_SUB)
        c_next[e] = cmax
        acc_ref[e] = acc
        m_ref[e] = m_new

    def finish(j, s_ref, fill_next):
        n_pieces = TK // TQ
        tri = (lax.broadcasted_iota(jnp.int32, (TQ, TQ), 0)
               <= lax.broadcasted_iota(jnp.int32, (TQ, TQ), 1))
        for diag in range(n_pieces):
            @pl.when(i % n_pieces == diag)
            def _():
                for e in range(2):
                    cmax = None
                    for piece in range(n_pieces):
                        if fill_next:
                            for sub in range(piece * n_sub // n_pieces, (piece + 1) * n_sub // n_pieces):
                                cmax = score_sub(e, qt_next[e], 0, sub, sc_ref, cmax)
                        if piece > diag:
                            continue
                        st = s_ref[e, piece * TQ:(piece + 1) * TQ, :]
                        if piece == diag:
                            st = jnp.where(tri, st, NEG_BIG)
                        m_prev = m_ref[e]
                        m_new = jnp.maximum(m_prev, jnp.max(st, axis=0, keepdims=True))
                        start = pl.multiple_of(j * TK + piece * TQ, TQ)
                        acc_ref[e] = (jnp.exp2(m_prev - m_new) * acc_ref[e]
                                      + pv(e, st, m_new, start, TQ))
                        m_ref[e] = m_new
                    if fill_next:
                        cc_ref[e] = cmax

    buf_a = (sa_ref, ca_ref)
    buf_b = (sb_ref, cb_ref)
    buf_c = (sc_ref, cc_ref)

    nfull = (i * TQ) // TK
    prefilled = jnp.logical_and(i > 0, ((i - 1) * TQ) // TK >= 1)

    @pl.when(jnp.logical_not(prefilled))
    def _():
        for e in range(2):
            scores(0, e, *buf_c)

    @pl.when(nfull >= 1)
    def _():
        for e in range(2):
            stage(e, 1, buf_a, 0, buf_c)

    def pair(jj, carry):
        j = 2 * jj + 1
        for e in range(2):
            stage(e, j + 1, buf_b, j, buf_a)
        for e in range(2):
            stage(e, j + 2, buf_a, j + 1, buf_b)
        return carry

    def group(jg, carry):
        for u in range(PAIRS_PER_TRIP):
            carry = pair(PAIRS_PER_TRIP * jg + u, carry)
        return carry

    n_pairs = jnp.maximum(nfull - 1, 0) // 2
    n_groups = n_pairs // PAIRS_PER_TRIP
    lax.fori_loop(0, n_groups, group, 0)
    lax.fori_loop(PAIRS_PER_TRIP * n_groups, n_pairs, pair, 0)

    @pl.when(nfull == 0)
    def _():
        finish(nfull, sc_ref, False)

    @pl.when(nfull % 2 == 1)
    def _():
        finish(nfull, sa_ref, True)

    @pl.when(jnp.logical_and(nfull >= 2, nfull % 2 == 0))
    def _():
        for e in range(2):
            stage(e, nfull, buf_b, nfull - 1, buf_a)
        finish(nfull, sb_ref, True)

    outs = []
    for e in range(2):
        acc = acc_ref[e]
        outs.append(acc[:FOX_HEAD_DIM] / acc[FOX_HEAD_DIM:FOX_HEAD_DIM + 1])
    o_ref[...] = jnp.concatenate(outs, axis=0).T.astype(o_ref.dtype)


def _fox_attention(qt0, qt1, k0, k1, vt):
    s = k0.shape[0]
    assert TK % TQ == 0 and TK % TK_SUB == 0 and s % TK == 0
    n_q = s // TQ
    blk_t = pl.BlockSpec((LANES, TQ), lambda p, i: (p, i))
    nxt_t = pl.BlockSpec((LANES, TQ), lambda p, i: (p, jnp.minimum(i + 1, n_q - 1)))
    res = pl.BlockSpec((s, LANES), lambda p, i: (0, p))
    return pl.pallas_call(
        _fox_kernel,
        grid=(FOX_PAIRS, n_q),
        in_specs=[blk_t, blk_t, nxt_t, nxt_t, res, res, pl.BlockSpec((LANES, s), lambda p, i: (p, 0))],
        out_specs=pl.BlockSpec((TQ, LANES), lambda p, i: (i, p)),
        out_shape=jax.ShapeDtypeStruct((s, FOX_WIDTH), BF16),
        scratch_shapes=[pltpu.VMEM((2, 1, TQ), F32), pltpu.VMEM((2, PV_ROWS, TQ), F32),
                        pltpu.VMEM((2, TK, TQ), F32), pltpu.VMEM((2, 1, TQ), F32),
                        pltpu.VMEM((2, TK, TQ), F32), pltpu.VMEM((2, 1, TQ), F32),
                        pltpu.VMEM((2, TK, TQ), F32), pltpu.VMEM((2, 1, TQ), F32)],
        compiler_params=pltpu.CompilerParams(dimension_semantics=("arbitrary", "arbitrary"),
                                             vmem_limit_bytes=VMEM_LIMIT),
        name="fox_attention",
    )(qt0, qt1, qt0, qt1, k0, k1, vt)


def _mix_ffn_kernel(fox_ref, ret_ref, x_ref, fox0_ref, ret0_ref, x0_ref,
                    wf_ref, wr_ref, g1_ref, l1g_ref, l1b_ref,
                    sc_ref, sh_ref, g2_ref, wu_ref, cw_ref, cb_ref, wd_ref, l2g_ref, l2b_ref,
                    o_ref, carry_ref, slab_in_ref, slab_out_ref, y_ref, x1_ref):
    tm = x_ref.shape[0]
    grp = tm // SUBLANES
    pitch = grp + SLAB_PAD_ROWS
    n_slab = D_MODEL // LANES

    def front(fox, ret, x):
        mix = (jnp.dot(fox, wf_ref[...], preferred_element_type=F32)
               + jnp.dot(ret, wr_ref[...], preferred_element_type=F32))
        x1_nat = _layer_norm(ALPHA * x + g1_ref[...] * mix, l1g_ref[...], l1b_ref[...])
        for c in range(n_slab):
            for s in range(SUBLANES):
                slab_in_ref[c, s * pitch:s * pitch + grp, :] = x1_nat[s * grp:(s + 1) * grp,
                                                                      c * LANES:(c + 1) * LANES]
        return jnp.concatenate(
            [jnp.concatenate([slab_in_ref[c, pl.ds(v, SUBLANES, stride=pitch), :] for v in range(grp)],
                             axis=0)
             for c in range(n_slab)], axis=1)

    @pl.when(pl.program_id(0) == 0)
    def _():
        carry_ref[...] = jnp.zeros_like(carry_ref)
        x1_ref[...] = front(fox0_ref[...], ret0_ref[...], x0_ref[...])

    h = (x1_ref[...] * (1.0 + sc_ref[...]) + sh_ref[...]).astype(BF16)
    first_sublane = lax.broadcasted_iota(jnp.int32, (SUBLANES, FF_CHUNK), 0) == 0

    def conv_up(c0):
        cols = slice(c0, c0 + FF_CHUNK)
        up = jnp.dot(h, wu_ref[:, cols], preferred_element_type=F32)
        prev = carry_ref[:, cols]
        tail = []
        for g in range(2):
            rows = slice(tm - (2 - g) * SUBLANES, tm - (1 - g) * SUBLANES)
            tail.append(jnp.where(first_sublane,
                                  pltpu.roll(prev[g * SUBLANES:(g + 1) * SUBLANES], 1, 0),
                                  pltpu.roll(up[rows], 1, 0)))
        carry_ref[:, cols] = up[tm - 2 * SUBLANES:tm]
        back1 = jnp.concatenate([tail[1], up[:tm - SUBLANES]], axis=0)
        back2 = jnp.concatenate([tail[0], tail[1], up[:tm - 2 * SUBLANES]], axis=0)
        cw = cw_ref[:, cols] * RSQRT2
        return cb_ref[:, cols] * RSQRT2 + back2 * cw[0:1] + back1 * cw[1:2] + up * cw[2:3]

    for ci in range(D_FF // FF_CHUNK):
        a = conv_up(ci * FF_CHUNK)
        b = conv_up(D_FF + ci * FF_CHUNK)
        y_ref[:, ci * FF_CHUNK:(ci + 1) * FF_CHUNK] = (a * (1.0 + lax.erf(a)) * b).astype(BF16)

    x1_next = front(fox_ref[...], ret_ref[...], x_ref[...])

    ffn = jnp.dot(y_ref[...], wd_ref[...], preferred_element_type=F32)
    out = _layer_norm(ALPHA * x1_ref[...] + g2_ref[...] * ffn, l2g_ref[...], l2b_ref[...])
    x1_ref[...] = x1_next

    for c in range(n_slab):
        for v in range(grp):
            slab_out_ref[c, pl.ds(v, SUBLANES, stride=pitch), :] = out[v * SUBLANES:(v + 1) * SUBLANES,
                                                                       c * LANES:(c + 1) * LANES]
    for c in range(n_slab):
        for s in range(SUBLANES):
            o_ref[s * grp:(s + 1) * grp, c * LANES:(c + 1) * LANES] = slab_out_ref[c, s * pitch:s * pitch + grp, :]


def _mix_ffn(fox, ret, x, w_fox, w_ret, g1, ln1_g, ln1_b, sc2, sh2, g2, w_up, conv_w, conv_b, w_down,
             ln2_g, ln2_b):
    s, d = x.shape
    tm = TM_FFN
    n_tiles = s // tm
    nxt = lambda w: pl.BlockSpec((tm, w), lambda i: (jnp.minimum(i + 1, n_tiles - 1), 0))
    vec = _const_spec((1, d))
    slab = pltpu.VMEM((d // LANES, tm + SUBLANES * SLAB_PAD_ROWS, LANES), F32)
    return pl.pallas_call(
        _mix_ffn_kernel,
        grid=(n_tiles,),
        in_specs=[nxt(FOX_WIDTH), nxt(RET_WIDTH), nxt(d),
                  _const_spec((tm, FOX_WIDTH)), _const_spec((tm, RET_WIDTH)), _const_spec((tm, d)),
                  _const_spec((FOX_WIDTH, d)), _const_spec((RET_WIDTH, d)), vec, vec, vec,
                  vec, vec, vec,
                  _const_spec((d, 2 * D_FF)), _const_spec((CONV_WIDTH, 2 * D_FF)),
                  _const_spec((1, 2 * D_FF)), _const_spec((D_FF, d)), vec, vec],
        out_specs=pl.BlockSpec((tm, d), lambda i: (i, 0)),
        out_shape=jax.ShapeDtypeStruct((s, d), F32),
        scratch_shapes=[pltpu.VMEM((2 * SUBLANES, 2 * D_FF), F32), slab, slab,
                        pltpu.VMEM((tm, D_FF), BF16), pltpu.VMEM((tm, d), F32)],
        compiler_params=pltpu.CompilerParams(dimension_semantics=("arbitrary",),
                                             vmem_limit_bytes=VMEM_LIMIT),
        name="mix_ffn",
    )(fox, ret, x, fox, ret, x, w_fox, w_ret, g1, ln1_g, ln1_b, sc2, sh2, g2, w_up, conv_w, conv_b,
      w_down, ln2_g, ln2_b)


def _placement_tables():
    e_q = np.zeros((LANES, FOX_WIDTH), np.float32)
    e_k = np.zeros((LANES, FOX_WIDTH), np.float32)
    for h in range(FOX_HEADS):
        base = (h // 2) * LANES + (FOX_HEAD_DIM if h % 2 == 0 else 0)
        for j in range(N_PIECES):
            e_q[j * FOX_HEADS + h, base + BIAS_CUM + j] = 1.0
            e_q[ONE_LANE, base + BIAS_ONE + j] = 1.0
            e_k[ONE_LANE, base + BIAS_CUM + j] = 1.0
            e_k[j * FOX_HEADS + h, base + BIAS_ONE + j] = -1.0
    return jnp.asarray(e_q.T, BF16), jnp.asarray(e_k, BF16)


def _rotation_tables(s, tm):
    dk = RET_HEAD_DIM
    inv_freq = ROPE_BASE ** (-np.arange(0, dk, 2, dtype=np.float64) / dk)
    inv_freq = np.concatenate([inv_freq, inv_freq])
    sign = np.concatenate([-np.ones(dk // 2), np.ones(dk // 2)])
    ang_a = (np.arange(s // tm, dtype=np.float64) * tm)[:, None] * inv_freq[None, :]
    ang_b = np.arange(tm, dtype=np.float64)[:, None] * inv_freq[None, :]
    rot_a = np.stack([np.cos(ang_a), np.sin(ang_a)], axis=1)
    rot_b = np.stack([np.cos(ang_b), np.sin(ang_b), sign * np.cos(ang_b), sign * np.sin(ang_b)])
    f32 = lambda a: jnp.asarray(a, F32)
    return f32(rot_a), f32(rot_b), f32(rot_a.transpose(0, 2, 1)), f32(rot_b.transpose(0, 2, 1))


def _decay_tables():
    c = RET_CHUNK
    log_gamma = np.log1p(-np.exp2(-5.0 - np.arange(RET_HEADS, dtype=np.float64)))
    idx = np.arange(c, dtype=np.float64)
    diff = idx[:, None] - idx[None, :]
    inner = np.where(diff[None] >= 0,
                     np.exp(np.maximum(diff, 0.0)[None] * log_gamma[:, None, None]), 0.0)
    xi = np.exp((idx[None, :] + 1.0) * log_gamma[:, None])
    zeta = np.exp((c - 1.0 - idx[None, :]) * log_gamma[:, None])
    g_chunk = np.exp(c * log_gamma)
    bshape = (RET_HEADS, c, RET_HEAD_DIM)
    f32 = lambda a: jnp.asarray(a, F32)
    return (f32(inner),
            f32(np.broadcast_to(xi[:, :, None], bshape)),
            f32(np.broadcast_to(zeta[:, None, :], bshape)),
            f32(np.broadcast_to(g_chunk[:, None, None], (RET_HEADS, 1, RET_HEAD_DIM))))


def kernel(x, c, w_ada, b_ada, w_in, b_f, w_out, ln1_g, ln1_b, w_up, conv_w, conv_b, w_down, ln2_g, ln2_b):
    b, s, d = x.shape
    assert (b, s, d) == (1, SEQ, D_MODEL) and w_ada.shape[0] == DEPTH
    xs = x[0]

    wi = w_in[0]
    o = 0
    parts = {}
    for name, width in (("fq", FOX_WIDTH), ("fk", FOX_WIDTH), ("fv", FOX_WIDTH), ("ff", FOX_HEADS),
                        ("rq", RET_WIDTH), ("rk", RET_WIDTH), ("rv", RET_WIDTH), ("rg", RET_WIDTH)):
        parts[name] = wi[:, o:o + width]
        o += width
    w_all = jnp.concatenate(
        [parts["fk"], jnp.pad(parts["ff"], ((0, 0), (0, LANES - FOX_HEADS))),
         parts["rq"], parts["rv"], parts["rg"]], axis=1).astype(BF16)
    w_t = jnp.concatenate([parts["fq"], parts["fv"], parts["rk"]], axis=1).T.astype(BF16)
    bf_row = jnp.pad(b_f[0][None, :], ((0, 0), (0, LANES - FOX_HEADS)))

    eq_t, e_k = _placement_tables()
    rot_a, rot_b, rot_at, rot_bt = _rotation_tables(s, TM_PROJ)
    inner, xi_b, zeta_b, gc_b = _decay_tables()

    c_col = c.reshape(d, 1)
    mod = _adaln_mod(c_col, w_ada[0], b_ada[0][None, :], 2 * d)
    sh1, sc1 = mod[:, :d], mod[:, d:]

    qt0, qt1, k0, k1, vt, ret, w_fox, w_ret, w_up_b, w_down_b, mod_late = _in_proj(
        xs, sc1, sh1, w_all, w_t, bf_row, rot_a, rot_b, rot_at, rot_bt, eq_t, e_k,
        inner, xi_b, zeta_b, gc_b, w_out[0], w_up[0], w_down[0],
        c_col, w_ada[0], b_ada[0][None, 2 * d:])
    g1, sh2, sc2, g2 = [mod_late[:, i * d:(i + 1) * d] for i in range(4)]
    fox = _fox_attention(qt0, qt1, k0, k1, vt)
    out = _mix_ffn(fox, ret, xs, w_fox, w_ret, g1, ln1_g[0][None, :], ln1_b[0][None, :],
                   sc2, sh2, g2, w_up_b, conv_w[0], conv_b[0][None, :],
                   w_down_b, ln2_g[0][None, :], ln2_b[0][None, :])
    return out[None]
```

```python
import math

import jax
import jax.numpy as jnp
import numpy as np
from jax import lax
from jax.experimental import pallas as pl
from jax.experimental.pallas import tpu as pltpu

F32 = jnp.float32
BF16 = jnp.bfloat16

D_MODEL = 1024
SEQ = 16384
FOX_HEADS = 8
FOX_HEAD_DIM = 64
FOX_WIDTH = FOX_HEADS * FOX_HEAD_DIM
RET_HEADS = 4
RET_HEAD_DIM = 128
RET_WIDTH = RET_HEADS * RET_HEAD_DIM
D_FF = 2816
CONV_WIDTH = 3
RET_CHUNK = 128
ROPE_BASE = 10000.0
LN_EPS = 1e-5
GN_EPS = 1e-6
DEPTH = 1
ALPHA = (2 * DEPTH) ** 0.25

LANES = 128
SUBLANES = 8
BF16_ROWS = 16
VMEM_LIMIT = 56 * 1024 * 1024

FOX_PAIRS = FOX_HEADS // 2
BIAS_CUM = 0
BIAS_ONE = 3
N_PIECES = 3
ONE_LANE = N_PIECES * FOX_HEADS
PV_ROWS = FOX_HEAD_DIM + BF16_ROWS
NEG_BIG = -1e30
LOG2_E = math.log2(math.e)
RSQRT2 = 2.0 ** -0.5

COL_K = 0
COL_F = COL_K + FOX_WIDTH
COL_RQ = COL_F + LANES
COL_RV = COL_RQ + RET_WIDTH
COL_RG = COL_RV + RET_WIDTH
N_COLS = COL_RG + RET_WIDTH
ROW_QT = 0
ROW_VT = ROW_QT + FOX_WIDTH
ROW_RKT = ROW_VT + FOX_WIDTH
N_ROWS_T = ROW_RKT + RET_WIDTH

TM_PROJ = 512
TQ = 512
TK = 1024
TK_SUB = 256
PAIRS_PER_TRIP = 2
TM_FFN = 512
FF_CHUNK = 256
SLAB_PAD_ROWS = 4


def _const_spec(shape):
    return pl.BlockSpec(shape, lambda *_: (0,) * len(shape), pipeline_mode=pl.Buffered(1))


def _vec_spec(width, k):
    return pl.BlockSpec((1, width), lambda *_: (0, k), pipeline_mode=pl.Buffered(1))


def _split3(x):
    p0 = x.astype(BF16).astype(F32)
    r1 = x - p0
    p1 = r1.astype(BF16).astype(F32)
    p2 = (r1 - p1).astype(BF16).astype(F32)
    return p0, p1, p2


def _layer_norm(y, g, b):
    mu = jnp.mean(y, axis=-1, keepdims=True)
    d = y - mu
    var = jnp.mean(d * d, axis=-1, keepdims=True)
    return d * lax.rsqrt(var + LN_EPS) * g + b


def _mod_kernel(c_ref, w_ref, b_ref, o_ref):
    c = c_ref[...]
    sc = c * jax.nn.sigmoid(c)
    o_ref[...] = jnp.sum(w_ref[...] * sc, axis=0, keepdims=True) + b_ref[...]


def _adaln_mod(c_col, w_ada, b_ada, n):
    d = w_ada.shape[0]
    tn = 512
    return pl.pallas_call(
        _mod_kernel,
        grid=(n // tn,),
        in_specs=[pl.BlockSpec((d, 1), lambda j: (0, 0)),
                  pl.BlockSpec((d, tn), lambda j: (0, j)),
                  pl.BlockSpec((1, tn), lambda j: (0, j))],
        out_specs=pl.BlockSpec((1, tn), lambda j: (0, j)),
        out_shape=jax.ShapeDtypeStruct((1, n), F32),
        compiler_params=pltpu.CompilerParams(dimension_semantics=("arbitrary",)),
        name="adaln_mod",
    )(c_col, w_ada, b_ada)


def _in_proj_kernel(x_ref, sc_ref, sh_ref, w_ref, wt_ref, bf_ref, ra_ref, rb_ref, rat_ref, rbt_ref,
                    eqt_ref, ek_ref, inner_ref, xi_ref, zeta_ref, gc_ref,
                    wfox_ref, wret_ref, wup_ref, wdown_ref, c_ref, wada_a_ref, wada_b_ref, bada_ref,
                    qt0_out, qt1_out, k0_out, k1_out, vt_out, ret_out,
                    wfox_out, wret_out, wup_out, wdown_out, mod_out,
                    carry_ref, r_ref):
    tm = x_ref.shape[0]

    @pl.when(pl.program_id(0) == 0)
    def _():
        carry_ref[...] = jnp.zeros_like(carry_ref)
        r_ref[...] = jnp.zeros_like(r_ref)

    h = (x_ref[...] * (1.0 + sc_ref[...]) + sh_ref[...]).astype(BF16)

    def proj(c0, width):
        return jnp.dot(h, w_ref[:, c0:c0 + width], preferred_element_type=F32)

    def proj_t(r0, height):
        return lax.dot_general(wt_ref[r0:r0 + height, :], h, (((1,), (1,)), ((), ())),
                               preferred_element_type=F32)

    lane = lax.broadcasted_iota(jnp.int32, (tm, LANES), 1)
    head_lane = lane < FOX_HEADS
    ff = proj(COL_F, LANES) + bf_ref[...]
    two = 2 * RET_HEAD_DIM
    rq_raw = [proj(COL_RQ + pair * two, two) for pair in range(RET_HEADS // 2)]
    rv = proj(COL_RV, RET_WIDTH).astype(BF16)
    rg = proj(COL_RG, RET_WIDTH)
    rkt_raw = proj_t(ROW_RKT, RET_WIDTH)

    logf = jnp.minimum(ff, 0.0) - jnp.log1p(jnp.exp(-jnp.abs(ff)))
    logf = jnp.where(head_lane, logf, 0.0)
    p0, p1, p2 = _split3(logf)
    pieces = p0 + pltpu.roll(p1, FOX_HEADS, 1) + pltpu.roll(p2, 2 * FOX_HEADS, 1)
    row = lax.broadcasted_iota(jnp.int32, (tm, tm), 0)
    col = lax.broadcasted_iota(jnp.int32, (tm, tm), 1)
    tril = jnp.where(row >= col, 1.0, 0.0).astype(BF16)
    c3 = jnp.dot(tril, pieces.astype(BF16), preferred_element_type=F32)

    half = RET_HEAD_DIM // 2
    ca, sa = ra_ref[0, 0:1, :], ra_ref[0, 1:2, :]
    cos = ca * rb_ref[0] - sa * rb_ref[1]
    sin = sa * rb_ref[2] + ca * rb_ref[3]
    cat, sat = rat_ref[0, :, 0:1], rat_ref[0, :, 1:2]
    cost = cat * rbt_ref[0] - sat * rbt_ref[1]
    sint = sat * rbt_ref[2] + cat * rbt_ref[3]
    kscale = RET_HEAD_DIM ** -0.5
    rq, rkt = [], []
    for a2 in rq_raw:
        for e in range(2):
            a = a2[:, e * RET_HEAD_DIM:(e + 1) * RET_HEAD_DIM]
            rq.append((a * cos + pltpu.roll(a, half, 1) * sin).astype(BF16))
    for hh in range(RET_HEADS):
        b = rkt_raw[hh * RET_HEAD_DIM:(hh + 1) * RET_HEAD_DIM]
        b_rot = jnp.concatenate([b[half:], b[:half]], axis=0)
        rkt.append(((b * cost + b_rot * sint) * kscale).astype(BF16))

    c = RET_CHUNK
    n_chunks = tm // c
    tiles = [(hh, ci) for hh in range(RET_HEADS) for ci in range(n_chunks)]

    def head_cols(hh):
        return slice(hh * RET_HEAD_DIM, (hh + 1) * RET_HEAD_DIM)

    def chunk_rows(ci):
        return slice(ci * c, (ci + 1) * c)

    intra, kv = {}, {}
    for hh, ci in tiles:
        kt = rkt[hh][:, chunk_rows(ci)]
        v = rv[chunk_rows(ci), head_cols(hh)]
        intra[hh, ci] = (jnp.dot(rq[hh][chunk_rows(ci)], kt, preferred_element_type=F32)
                         * inner_ref[hh]).astype(BF16)
        kzt = (kt.astype(F32) * zeta_ref[hh]).astype(BF16)
        kv[hh, ci] = jnp.dot(kzt, v, preferred_element_type=F32)

    wfox_out[...] = wfox_ref[...].astype(BF16)
    wret_out[...] = wret_ref[...].astype(BF16)
    wup_out[...] = wup_ref[...].astype(BF16)
    wdown_out[...] = wdown_ref[...].astype(BF16)
    @pl.when(pl.program_id(0) == 0)
    def _():
        mod_out[...] = bada_ref[...]

    cs = c_ref[...]
    cs = cs * jax.nn.sigmoid(cs)
    half_w = wada_a_ref.shape[1]
    mod_out[:, :half_w] += jnp.sum(wada_a_ref[...] * cs, axis=0, keepdims=True)
    mod_out[:, half_w:] += jnp.sum(wada_b_ref[...] * cs, axis=0, keepdims=True)

    k = proj(COL_K, FOX_WIDTH)
    cum = c3 + pltpu.roll(c3, LANES - FOX_HEADS, 1) + pltpu.roll(c3, LANES - 2 * FOX_HEADS, 1)
    cum = jnp.where(head_lane, cum, 0.0) + carry_ref[0:1, :]
    carry_ref[0:1, :] = cum[tm - 1:tm, :]
    c0_, c1_, c2_ = _split3(cum * LOG2_E)
    cum_pieces = c0_ + pltpu.roll(c1_, FOX_HEADS, 1) + pltpu.roll(c2_, 2 * FOX_HEADS, 1)
    cum_pieces = jnp.where(lane == ONE_LANE, 1.0, cum_pieces)
    kb = jnp.dot(cum_pieces.astype(BF16), ek_ref[...], preferred_element_type=F32)
    qbt = jnp.dot(eqt_ref[...], cum_pieces.T.astype(BF16), preferred_element_type=F32)
    even_own = (lax.broadcasted_iota(jnp.int32, (1, FOX_WIDTH), 1) % LANES) < FOX_HEAD_DIM
    even_own_t = (lax.broadcasted_iota(jnp.int32, (FOX_WIDTH, 1), 0) % LANES) < FOX_HEAD_DIM
    k0_out[...] = jnp.where(even_own, k, kb).astype(BF16)
    k1_out[...] = jnp.where(even_own, kb, k).astype(BF16)

    state = {}
    for hh in range(RET_HEADS):
        r_state = r_ref[hh]
        for ci in range(n_chunks):
            state[hh, ci] = r_state.astype(BF16)
            r_state = r_state * gc_ref[hh] + kv[hh, ci]
        r_ref[hh] = r_state

    for hh, ci in tiles:
        v = rv[chunk_rows(ci), head_cols(hh)]
        o = (jnp.dot(intra[hh, ci], v, preferred_element_type=F32)
             + jnp.dot(rq[hh][chunk_rows(ci)], state[hh, ci], preferred_element_type=F32) * xi_ref[hh])
        mu = jnp.mean(o, axis=-1, keepdims=True)
        d = o - mu
        var = jnp.mean(d * d, axis=-1, keepdims=True)
        on = d * lax.rsqrt(var + GN_EPS)
        g = rg[chunk_rows(ci), head_cols(hh)]
        ret_out[chunk_rows(ci), head_cols(hh)] = (g * jax.nn.sigmoid(g) * on).astype(BF16)

    tp = proj_t(ROW_QT, 2 * FOX_WIDTH)
    qt = tp[:FOX_WIDTH] * (FOX_HEAD_DIM ** -0.5 * LOG2_E)
    qt0_out[...] = jnp.where(even_own_t, qt, qbt).astype(BF16)
    qt1_out[...] = jnp.where(even_own_t, qbt, qt).astype(BF16)
    vt_out[...] = tp[FOX_WIDTH:].astype(BF16)


def _in_proj(x, sc1, sh1, w_all, w_t, bf_row, rot_a, rot_b, rot_at, rot_bt, eq_t, e_k,
             inner, xi_b, zeta_b, gc_b, w_out, w_up, w_down, c_col, w_ada, b_late):
    s, d = x.shape
    tm = TM_PROJ
    n = s // tm
    row = lambda w: pl.BlockSpec((tm, w), lambda i: (i, 0))
    col = lambda h: pl.BlockSpec((h, tm), lambda i: (0, i))
    tab = _const_spec((RET_HEADS, RET_CHUNK, RET_CHUNK))
    fox_rows, ret_rows, up_rows, down_rows = FOX_WIDTH // n, RET_WIDTH // n, d // n, 2 * D_FF // n
    assert fox_rows % 16 == 0 and ret_rows % 16 == 0 and up_rows % 16 == 0 and down_rows % 16 == 0
    assert n % 2 == 0 and FOX_WIDTH % fox_rows == 0
    cast_specs = [pl.BlockSpec((fox_rows, d), lambda i: (i, 0)),
                  pl.BlockSpec((ret_rows, d), lambda i: (FOX_WIDTH // ret_rows + i, 0)),
                  pl.BlockSpec((up_rows, 2 * D_FF), lambda i: (i, 0)),
                  pl.BlockSpec((down_rows, d), lambda i: (i // 2, 0)),
                  pl.BlockSpec((up_rows, 1), lambda i: (i, 0)),
                  pl.BlockSpec((up_rows, 2 * d), lambda i: (i, 1)),
                  pl.BlockSpec((up_rows, 2 * d), lambda i: (i, 2)),
                  _const_spec((1, 4 * d))]
    cast_out_specs = [pl.BlockSpec((fox_rows, d), lambda i: (i, 0)),
                      pl.BlockSpec((ret_rows, d), lambda i: (i, 0)),
                      pl.BlockSpec((up_rows, 2 * D_FF), lambda i: (i, 0)),
                      pl.BlockSpec((down_rows, d), lambda i: (i // 2, 0)),
                      pl.BlockSpec((1, 4 * d), lambda i: (0, 0))]
    out_shapes = ([jax.ShapeDtypeStruct((FOX_WIDTH, s), BF16)] * 2
                  + [jax.ShapeDtypeStruct((s, FOX_WIDTH), BF16)] * 2
                  + [jax.ShapeDtypeStruct((FOX_WIDTH, s), BF16),
                     jax.ShapeDtypeStruct((s, RET_WIDTH), BF16),
                     jax.ShapeDtypeStruct((FOX_WIDTH, d), BF16),
                     jax.ShapeDtypeStruct((RET_WIDTH, d), BF16),
                     jax.ShapeDtypeStruct((d, 2 * D_FF), BF16),
                     jax.ShapeDtypeStruct((D_FF, d), BF16),
                     jax.ShapeDtypeStruct((1, 4 * d), F32)])
    return pl.pallas_call(
        _in_proj_kernel,
        grid=(s // tm,),
        in_specs=[row(d), _vec_spec(d, 1), _vec_spec(d, 0),
                  _const_spec((d, N_COLS)), _const_spec((N_ROWS_T, d)), _const_spec((1, LANES)),
                  pl.BlockSpec((1, 2, RET_HEAD_DIM), lambda i: (i, 0, 0)),
                  _const_spec((4, tm, RET_HEAD_DIM)),
                  pl.BlockSpec((1, RET_HEAD_DIM, 2), lambda i: (i, 0, 0)),
                  _const_spec((4, RET_HEAD_DIM, tm)),
                  _const_spec((FOX_WIDTH, LANES)), _const_spec((LANES, FOX_WIDTH)),
                  tab, tab, tab, _const_spec((RET_HEADS, 1, RET_HEAD_DIM))] + cast_specs,
        out_specs=([col(FOX_WIDTH)] * 2 + [row(FOX_WIDTH)] * 2 + [col(FOX_WIDTH), row(RET_WIDTH)]
                   + cast_out_specs),
        out_shape=out_shapes,
        scratch_shapes=[pltpu.VMEM((SUBLANES, LANES), F32),
                        pltpu.VMEM((RET_HEADS, RET_HEAD_DIM, RET_HEAD_DIM), F32)],
        compiler_params=pltpu.CompilerParams(dimension_semantics=("arbitrary",),
                                             vmem_limit_bytes=VMEM_LIMIT),
        name="in_proj",
    )(x, sc1, sh1, w_all, w_t, bf_row, rot_a, rot_b, rot_at, rot_bt, eq_t, e_k,
      inner, xi_b, zeta_b, gc_b, w_out, w_out, w_up, w_down, c_col, w_ada, w_ada, b_late)


def _fox_kernel(qt0_ref, qt1_ref, qtn0_ref, qtn1_ref, k0_ref, k1_ref, vt_ref, o_ref, m_ref, acc_ref,
                sa_ref, ca_ref, sb_ref, cb_ref, sc_ref, cc_ref):
    i = pl.program_id(1)
    k_refs = (k0_ref, k1_ref)
    qt = [qt0_ref[...], qt1_ref[...]]
    qt_next = [qtn0_ref[...], qtn1_ref[...]]
    m_ref[...] = jnp.full_like(m_ref, NEG_BIG)
    acc_ref[...] = jnp.zeros_like(acc_ref)
    n_sub = TK // TK_SUB

    def qk(e, q_aug, start):
        return jnp.dot(k_refs[e][pl.ds(start, TK_SUB), :], q_aug, preferred_element_type=F32)

    def values(e, start, width):
        return jnp.concatenate(
            [vt_ref[e * FOX_HEAD_DIM:(e + 1) * FOX_HEAD_DIM, pl.ds(start, width)],
             jnp.ones((BF16_ROWS, width), BF16)], axis=0)

    def pv(e, st, m_new, start, width):
        pt = jnp.exp2(st - m_new).astype(BF16)
        return jnp.dot(values(e, start, width), pt, preferred_element_type=F32)

    def score_sub(e, q_aug, j, sub, s_ref, cmax):
        st = qk(e, q_aug, pl.multiple_of(j * TK + sub * TK_SUB, TK_SUB))
        s_ref[e, sub * TK_SUB:(sub + 1) * TK_SUB, :] = st
        cm = jnp.max(st, axis=0, keepdims=True)
        return cm if cmax is None else jnp.maximum(cmax, cm)

    def scores(j, e, s_ref, c_ref):
        cmax = None
        for sub in range(n_sub):
            cmax = score_sub(e, qt[e], j, sub, s_ref, cmax)
        c_ref[e] = cmax

    def stage(e, j_next, buf_next, j_cur, buf_cur):
        (s_next, c_next), (s_cur, c_cur) = buf_next, buf_cur
        m_prev = m_ref[e]
        m_new = jnp.maximum(m_prev, c_cur[e])
        acc = jnp.exp2(m_prev - m_new) * acc_ref[e]
        cmax = None
        for sub in range(n_sub):
            rows = slice(sub * TK_SUB, (sub + 1) * TK_SUB)
            cmax = score_sub(e, qt[e], j_next, sub, s_next, cmax)
            acc = acc + pv(e, s_cur[e, rows, :], m_new,
                           pl.multiple_of(j_cur * TK + sub * TK_SUB, TK_SUB), TK_SUB)
        c_next[e] = cmax
        acc_ref[e] = acc
        m_ref[e] = m_new

    def finish(j, s_ref, fill_next):
        n_pieces = TK // TQ
        tri = (lax.broadcasted_iota(jnp.int32, (TQ, TQ), 0)
               <= lax.broadcasted_iota(jnp.int32, (TQ, TQ), 1))
        for diag in range(n_pieces):
            @pl.when(i % n_pieces == diag)
            def _():
                for e in range(2):
                    cmax = None
                    for piece in range(n_pieces):
                        if fill_next:
                            for sub in range(piece * n_sub // n_pieces, (piece + 1) * n_sub // n_pieces):
                                cmax = score_sub(e, qt_next[e], 0, sub, sc_ref, cmax)
                        if piece > diag:
                            continue
                        st = s_ref[e, piece * TQ:(piece + 1) * TQ, :]
                        if piece == diag:
                            st = jnp.where(tri, st, NEG_BIG)
                        m_prev = m_ref[e]
                        m_new = jnp.maximum(m_prev, jnp.max(st, axis=0, keepdims=True))
                        start = pl.multiple_of(j * TK + piece * TQ, TQ)
                        acc_ref[e] = (jnp.exp2(m_prev - m_new) * acc_ref[e]
                                      + pv(e, st, m_new, start, TQ))
                        m_ref[e] = m_new
                    if fill_next:
                        cc_ref[e] = cmax

    buf_a = (sa_ref, ca_ref)
    buf_b = (sb_ref, cb_ref)
    buf_c = (sc_ref, cc_ref)

    nfull = (i * TQ) // TK
    prefilled = jnp.logical_and(i > 0, ((i - 1) * TQ) // TK >= 1)

    @pl.when(jnp.logical_not(prefilled))
    def _():
        for e in range(2):
            scores(0, e, *buf_c)

    @pl.when(nfull >= 1)
    def _():
        for e in range(2):
            stage(e, 1, buf_a, 0, buf_c)

    def pair(jj, carry):
        j = 2 * jj + 1
        for e in range(2):
            stage(e, j + 1, buf_b, j, buf_a)
        for e in range(2):
            stage(e, j + 2, buf_a, j + 1, buf_b)
        return carry

    def group(jg, carry):
        for u in range(PAIRS_PER_TRIP):
            carry = pair(PAIRS_PER_TRIP * jg + u, carry)
        return carry

    n_pairs = jnp.maximum(nfull - 1, 0) // 2
    n_groups = n_pairs // PAIRS_PER_TRIP
    lax.fori_loop(0, n_groups, group, 0)
    lax.fori_loop(PAIRS_PER_TRIP * n_groups, n_pairs, pair, 0)

    @pl.when(nfull == 0)
    def _():
        finish(nfull, sc_ref, False)

    @pl.when(nfull % 2 == 1)
    def _():
        finish(nfull, sa_ref, True)

    @pl.when(jnp.logical_and(nfull >= 2, nfull % 2 == 0))
    def _():
        for e in range(2):
            stage(e, nfull, buf_b, nfull - 1, buf_a)
        finish(nfull, sb_ref, True)

    outs = []
    for e in range(2):
        acc = acc_ref[e]
        outs.append(acc[:FOX_HEAD_DIM] / acc[FOX_HEAD_DIM:FOX_HEAD_DIM + 1])
    o_ref[...] = jnp.concatenate(outs, axis=0).T.astype(o_ref.dtype)


def _fox_attention(qt0, qt1, k0, k1, vt):
    s = k0.shape[0]
    assert TK % TQ == 0 and TK % TK_SUB == 0 and s % TK == 0
    n_q = s // TQ
    blk_t = pl.BlockSpec((LANES, TQ), lambda p, i: (p, i))
    nxt_t = pl.BlockSpec((LANES, TQ), lambda p, i: (p, jnp.minimum(i + 1, n_q - 1)))
    res = pl.BlockSpec((s, LANES), lambda p, i: (0, p))
    return pl.pallas_call(
        _fox_kernel,
        grid=(FOX_PAIRS, n_q),
        in_specs=[blk_t, blk_t, nxt_t, nxt_t, res, res, pl.BlockSpec((LANES, s), lambda p, i: (p, 0))],
        out_specs=pl.BlockSpec((TQ, LANES), lambda p, i: (i, p)),
        out_shape=jax.ShapeDtypeStruct((s, FOX_WIDTH), BF16),
        scratch_shapes=[pltpu.VMEM((2, 1, TQ), F32), pltpu.VMEM((2, PV_ROWS, TQ), F32),
                        pltpu.VMEM((2, TK, TQ), F32), pltpu.VMEM((2, 1, TQ), F32),
                        pltpu.VMEM((2, TK, TQ), F32), pltpu.VMEM((2, 1, TQ), F32),
                        pltpu.VMEM((2, TK, TQ), F32), pltpu.VMEM((2, 1, TQ), F32)],
        compiler_params=pltpu.CompilerParams(dimension_semantics=("arbitrary", "arbitrary"),
                                             vmem_limit_bytes=VMEM_LIMIT),
        name="fox_attention",
    )(qt0, qt1, qt0, qt1, k0, k1, vt)


def _mix_ffn_kernel(fox_ref, ret_ref, x_ref, fox0_ref, ret0_ref, x0_ref,
                    wf_ref, wr_ref, g1_ref, l1g_ref, l1b_ref,
                    sc_ref, sh_ref, g2_ref, wu_ref, cw_ref, cb_ref, wd_ref, l2g_ref, l2b_ref,
                    o_ref, carry_ref, slab_in_ref, slab_out_ref, y_ref, x1_ref):
    tm = x_ref.shape[0]
    grp = tm // SUBLANES
    pitch = grp + SLAB_PAD_ROWS
    n_slab = D_MODEL // LANES

    def front(fox, ret, x):
        mix = (jnp.dot(fox, wf_ref[...], preferred_element_type=F32)
               + jnp.dot(ret, wr_ref[...], preferred_element_type=F32))
        x1_nat = _layer_norm(ALPHA * x + g1_ref[...] * mix, l1g_ref[...], l1b_ref[...])
        for c in range(n_slab):
            for s in range(SUBLANES):
                slab_in_ref[c, s * pitch:s * pitch + grp, :] = x1_nat[s * grp:(s + 1) * grp,
                                                                      c * LANES:(c + 1) * LANES]
        return jnp.concatenate(
            [jnp.concatenate([slab_in_ref[c, pl.ds(v, SUBLANES, stride=pitch), :] for v in range(grp)],
                             axis=0)
             for c in range(n_slab)], axis=1)

    @pl.when(pl.program_id(0) == 0)
    def _():
        carry_ref[...] = jnp.zeros_like(carry_ref)
        x1_ref[...] = front(fox0_ref[...], ret0_ref[...], x0_ref[...])

    h = (x1_ref[...] * (1.0 + sc_ref[...]) + sh_ref[...]).astype(BF16)
    first_sublane = lax.broadcasted_iota(jnp.int32, (SUBLANES, FF_CHUNK), 0) == 0

    def conv_up(c0):
        cols = slice(c0, c0 + FF_CHUNK)
        up = jnp.dot(h, wu_ref[:, cols], preferred_element_type=F32)
        prev = carry_ref[:, cols]
        tail = []
        for g in range(2):
            rows = slice(tm - (2 - g) * SUBLANES, tm - (1 - g) * SUBLANES)
            tail.append(jnp.where(first_sublane,
                                  pltpu.roll(prev[g * SUBLANES:(g + 1) * SUBLANES], 1, 0),
                                  pltpu.roll(up[rows], 1, 0)))
        carry_ref[:, cols] = up[tm - 2 * SUBLANES:tm]
        back1 = jnp.concatenate([tail[1], up[:tm - SUBLANES]], axis=0)
        back2 = jnp.concatenate([tail[0], tail[1], up[:tm - 2 * SUBLANES]], axis=0)
        cw = cw_ref[:, cols] * RSQRT2
        return cb_ref[:, cols] * RSQRT2 + back2 * cw[0:1] + back1 * cw[1:2] + up * cw[2:3]

    for ci in range(D_FF // FF_CHUNK):
        a = conv_up(ci * FF_CHUNK)
        b = conv_up(D_FF + ci * FF_CHUNK)
        y_ref[:, ci * FF_CHUNK:(ci + 1) * FF_CHUNK] = (a * (1.0 + lax.erf(a)) * b).astype(BF16)

    x1_next = front(fox_ref[...], ret_ref[...], x_ref[...])

    ffn = jnp.dot(y_ref[...], wd_ref[...], preferred_element_type=F32)
    out = _layer_norm(ALPHA * x1_ref[...] + g2_ref[...] * ffn, l2g_ref[...], l2b_ref[...])
    x1_ref[...] = x1_next

    for c in range(n_slab):
        for v in range(grp):
            slab_out_ref[c, pl.ds(v, SUBLANES, stride=pitch), :] = out[v * SUBLANES:(v + 1) * SUBLANES,
                                                                       c * LANES:(c + 1) * LANES]
    for c in range(n_slab):
        for s in range(SUBLANES):
            o_ref[s * grp:(s + 1) * grp, c * LANES:(c + 1) * LANES] = slab_out_ref[c, s * pitch:s * pitch + grp, :]


def _mix_ffn(fox, ret, x, w_fox, w_ret, g1, ln1_g, ln1_b, sc2, sh2, g2, w_up, conv_w, conv_b, w_down,
             ln2_g, ln2_b):
    s, d = x.shape
    tm = TM_FFN
    n_tiles = s // tm
    nxt = lambda w: pl.BlockSpec((tm, w), lambda i: (jnp.minimum(i + 1, n_tiles - 1), 0))
    vec = _const_spec((1, d))
    slab = pltpu.VMEM((d // LANES, tm + SUBLANES * SLAB_PAD_ROWS, LANES), F32)
    return pl.pallas_call(
        _mix_ffn_kernel,
        grid=(n_tiles,),
        in_specs=[nxt(FOX_WIDTH), nxt(RET_WIDTH), nxt(d),
                  _const_spec((tm, FOX_WIDTH)), _const_spec((tm, RET_WIDTH)), _const_spec((tm, d)),
                  _const_spec((FOX_WIDTH, d)), _const_spec((RET_WIDTH, d)), _vec_spec(d, 0), vec, vec,
                  _vec_spec(d, 2), _vec_spec(d, 1), _vec_spec(d, 3),
                  _const_spec((d, 2 * D_FF)), _const_spec((CONV_WIDTH, 2 * D_FF)),
                  _const_spec((1, 2 * D_FF)), _const_spec((D_FF, d)), vec, vec],
        out_specs=pl.BlockSpec((tm, d), lambda i: (i, 0)),
        out_shape=jax.ShapeDtypeStruct((s, d), F32),
        scratch_shapes=[pltpu.VMEM((2 * SUBLANES, 2 * D_FF), F32), slab, slab,
                        pltpu.VMEM((tm, D_FF), BF16), pltpu.VMEM((tm, d), F32)],
        compiler_params=pltpu.CompilerParams(dimension_semantics=("arbitrary",),
                                             vmem_limit_bytes=VMEM_LIMIT),
        name="mix_ffn",
    )(fox, ret, x, fox, ret, x, w_fox, w_ret, g1, ln1_g, ln1_b, sc2, sh2, g2, w_up, conv_w, conv_b,
      w_down, ln2_g, ln2_b)


def _placement_tables():
    e_q = np.zeros((LANES, FOX_WIDTH), np.float32)
    e_k = np.zeros((LANES, FOX_WIDTH), np.float32)
    for h in range(FOX_HEADS):
        base = (h // 2) * LANES + (FOX_HEAD_DIM if h % 2 == 0 else 0)
        for j in range(N_PIECES):
            e_q[j * FOX_HEADS + h, base + BIAS_CUM + j] = 1.0
            e_q[ONE_LANE, base + BIAS_ONE + j] = 1.0
            e_k[ONE_LANE, base + BIAS_CUM + j] = 1.0
            e_k[j * FOX_HEADS + h, base + BIAS_ONE + j] = -1.0
    return jnp.asarray(e_q.T, BF16), jnp.asarray(e_k, BF16)


def _rotation_tables(s, tm):
    dk = RET_HEAD_DIM
    inv_freq = ROPE_BASE ** (-np.arange(0, dk, 2, dtype=np.float64) / dk)
    inv_freq = np.concatenate([inv_freq, inv_freq])
    sign = np.concatenate([-np.ones(dk // 2), np.ones(dk // 2)])
    ang_a = (np.arange(s // tm, dtype=np.float64) * tm)[:, None] * inv_freq[None, :]
    ang_b = np.arange(tm, dtype=np.float64)[:, None] * inv_freq[None, :]
    rot_a = np.stack([np.cos(ang_a), np.sin(ang_a)], axis=1)
    rot_b = np.stack([np.cos(ang_b), np.sin(ang_b), sign * np.cos(ang_b), sign * np.sin(ang_b)])
    f32 = lambda a: jnp.asarray(a, F32)
    return f32(rot_a), f32(rot_b), f32(rot_a.transpose(0, 2, 1)), f32(rot_b.transpose(0, 2, 1))


def _decay_tables():
    c = RET_CHUNK
    log_gamma = np.log1p(-np.exp2(-5.0 - np.arange(RET_HEADS, dtype=np.float64)))
    idx = np.arange(c, dtype=np.float64)
    diff = idx[:, None] - idx[None, :]
    inner = np.where(diff[None] >= 0,
                     np.exp(np.maximum(diff, 0.0)[None] * log_gamma[:, None, None]), 0.0)
    xi = np.exp((idx[None, :] + 1.0) * log_gamma[:, None])
    zeta = np.exp((c - 1.0 - idx[None, :]) * log_gamma[:, None])
    g_chunk = np.exp(c * log_gamma)
    bshape = (RET_HEADS, c, RET_HEAD_DIM)
    f32 = lambda a: jnp.asarray(a, F32)
    return (f32(inner),
            f32(np.broadcast_to(xi[:, :, None], bshape)),
            f32(np.broadcast_to(zeta[:, None, :], bshape)),
            f32(np.broadcast_to(g_chunk[:, None, None], (RET_HEADS, 1, RET_HEAD_DIM))))


def kernel(x, c, w_ada, b_ada, w_in, b_f, w_out, ln1_g, ln1_b, w_up, conv_w, conv_b, w_down, ln2_g, ln2_b):
    b, s, d = x.shape
    assert (b, s, d) == (1, SEQ, D_MODEL) and w_ada.shape[0] == DEPTH
    xs = x[0]

    wi = w_in[0]
    o = 0
    parts = {}
    for name, width in (("fq", FOX_WIDTH), ("fk", FOX_WIDTH), ("fv", FOX_WIDTH), ("ff", FOX_HEADS),
                        ("rq", RET_WIDTH), ("rk", RET_WIDTH), ("rv", RET_WIDTH), ("rg", RET_WIDTH)):
        parts[name] = wi[:, o:o + width]
        o += width
    w_all = jnp.concatenate(
        [parts["fk"], jnp.pad(parts["ff"], ((0, 0), (0, LANES - FOX_HEADS))),
         parts["rq"], parts["rv"], parts["rg"]], axis=1).astype(BF16)
    w_t = jnp.concatenate([parts["fq"], parts["fv"], parts["rk"]], axis=1).T.astype(BF16)
    bf_row = jnp.pad(b_f[0][None, :], ((0, 0), (0, LANES - FOX_HEADS)))

    eq_t, e_k = _placement_tables()
    rot_a, rot_b, rot_at, rot_bt = _rotation_tables(s, TM_PROJ)
    inner, xi_b, zeta_b, gc_b = _decay_tables()

    c_col = c.reshape(d, 1)
    mod = _adaln_mod(c_col, w_ada[0], b_ada[0][None, :], 2 * d)
    sh1 = sc1 = mod

    qt0, qt1, k0, k1, vt, ret, w_fox, w_ret, w_up_b, w_down_b, mod_late = _in_proj(
        xs, sc1, sh1, w_all, w_t, bf_row, rot_a, rot_b, rot_at, rot_bt, eq_t, e_k,
        inner, xi_b, zeta_b, gc_b, w_out[0], w_up[0], w_down[0],
        c_col, w_ada[0], b_ada[0][None, 2 * d:])
    g1 = sh2 = sc2 = g2 = mod_late
    fox = _fox_attention(qt0, qt1, k0, k1, vt)
    out = _mix_ffn(fox, ret, xs, w_fox, w_ret, g1, ln1_g[0][None, :], ln1_b[0][None, :],
                   sc2, sh2, g2, w_up_b, conv_w[0], conv_b[0][None, :],
                   w_down_b, ln2_g[0][None, :], ln2_b[0][None, :])
    return out[None]
```
